```python
import jax, jax.numpy as jnp
from jax import lax
import numpy as np

D_MODEL = 1024
BATCH = 8
SEQ = 2048
DEPTH = 4

GRID_W = 64
CTX_LEN = 256
ATT_HEADS = 8
ATT_KV_HEADS = 2
ATT_HEAD_DIM = 64
ATT_WIDTH = ATT_HEADS * ATT_HEAD_DIM
KV_WIDTH = ATT_KV_HEADS * ATT_HEAD_DIM
WINDOW = 128
BLOCK = 128
ROPE_BASE = 10000.0
M_HEADS = 4
M_HEAD_DIM = 128
M_WIDTH = M_HEADS * M_HEAD_DIM
CONV_K = 5
CHUNK = 128
N_DIR = 2
MIX_WIDTH = ATT_WIDTH + M_WIDTH
EPS = 1e-6
SPLIT_SIZES = (ATT_WIDTH, KV_WIDTH, KV_WIDTH, ATT_WIDTH, M_WIDTH, M_WIDTH, M_WIDTH, M_WIDTH, M_WIDTH, N_DIR * M_HEADS, N_DIR * M_HEADS)
IN_COLS = 2 * ATT_WIDTH + 2 * KV_WIDTH + 5 * M_WIDTH + 2 * N_DIR * M_HEADS

kernel_name = "hymba_style_window_gqa_bidir_mlstm_prefix_dit"


def rmsnorm(x, g):
    xf = x.astype(jnp.float32)
    y = xf * lax.rsqrt(jnp.mean(xf * xf, axis=-1, keepdims=True) + EPS)
    return (y * g.astype(jnp.float32)).astype(x.dtype)


def project(xn, w_in):
    idx = np.cumsum(SPLIT_SIZES)[:-1].tolist()
    return jnp.split(xn @ w_in, idx, axis=-1)


def axial_rope_tables(T):
    rows = T // GRID_W
    row = jnp.repeat(jnp.arange(rows), GRID_W).astype(jnp.float32)
    col = jnp.tile(jnp.arange(GRID_W), rows).astype(jnp.float32)
    half = ATT_HEAD_DIM // 2
    inv = ROPE_BASE ** (-jnp.arange(0, half, 2, dtype=jnp.float32) / half)
    ang_r = row[:, None] * inv
    ang_c = col[:, None] * inv
    return (jnp.cos(ang_r)[:, None, :], jnp.sin(ang_r)[:, None, :],
            jnp.cos(ang_c)[:, None, :], jnp.sin(ang_c)[:, None, :])


def rope_2d(u, tables):
    cos_r, sin_r, cos_c, sin_c = [t.astype(u.dtype) for t in tables]

    def rot(a, cos, sin):
        a1, a2 = jnp.split(a, 2, axis=-1)
        return jnp.concatenate([a1 * cos - a2 * sin, a2 * cos + a1 * sin], axis=-1)

    ur, uc = jnp.split(u, 2, axis=-1)
    return jnp.concatenate([rot(ur, cos_r, sin_r), rot(uc, cos_c, sin_c)], axis=-1)


def window_ctx_attention(q, k, v, kc, vc, sink):
    B, T, H, dh = q.shape
    G = H // ATT_KV_HEADS
    NB = T // BLOCK
    scale = dh ** -0.5
    qb = q.reshape(B, NB, BLOCK, ATT_KV_HEADS, G, dh)
    pad = ((0, 0), (BLOCK, BLOCK), (0, 0), (0, 0))
    kp = jnp.pad(k, pad).reshape(B, NB + 2, BLOCK, ATT_KV_HEADS, dh)
    vp = jnp.pad(v, pad).reshape(B, NB + 2, BLOCK, ATT_KV_HEADS, dh)
    kw = jnp.concatenate([kp[:, :-2], kp[:, 1:-1], kp[:, 2:]], axis=2)
    vw = jnp.concatenate([vp[:, :-2], vp[:, 1:-1], vp[:, 2:]], axis=2)
    s_win = jnp.einsum('bnqhgd,bnkhd->bhgnqk', qb, kw).astype(jnp.float32) * scale
    qpos = jnp.arange(NB)[:, None, None] * BLOCK + jnp.arange(BLOCK)[None, :, None]
    kpos = (jnp.arange(NB)[:, None, None] - 1) * BLOCK + jnp.arange(3 * BLOCK)[None, None, :]
    valid = (jnp.abs(qpos - kpos) <= WINDOW) & (kpos >= 0) & (kpos < T)
    s_win = jnp.where(valid, s_win, -jnp.inf)
    s_ctx = jnp.einsum('bnqhgd,bchd->bhgnqc', qb, kc).astype(jnp.float32) * scale
    s_sink = jnp.broadcast_to(sink.astype(jnp.float32).reshape(1, ATT_KV_HEADS, G, 1, 1, 1),
                              s_win.shape[:-1] + (1,))
    p = jax.nn.softmax(jnp.concatenate([s_win, s_ctx, s_sink], axis=-1), axis=-1)
    Lw = 3 * BLOCK
    Lc = kc.shape[1]
    p_win = p[..., :Lw].astype(v.dtype)
    p_ctx = p[..., Lw:Lw + Lc].astype(v.dtype)
    out = (jnp.einsum('bhgnqk,bnkhd->bnqhgd', p_win, vw)
           + jnp.einsum('bhgnqc,bchd->bnqhgd', p_ctx, vc))
    return out.reshape(B, T, H * dh)


def ctx_attention(qc, kc, vc, sink):
    B, Lc, H, dh = qc.shape
    G = H // ATT_KV_HEADS
    qg = qc.reshape(B, Lc, ATT_KV_HEADS, G, dh)
    s = jnp.einsum('bqhgd,bkhd->bhgqk', qg, kc).astype(jnp.float32) * dh ** -0.5
    s_sink = jnp.broadcast_to(sink.astype(jnp.float32).reshape(1, ATT_KV_HEADS, G, 1, 1),
                              s.shape[:-1] + (1,))
    p = jax.nn.softmax(jnp.concatenate([s, s_sink], axis=-1), axis=-1)
    out = jnp.einsum('bhgqk,bkhd->bqhgd', p[..., :Lc].astype(vc.dtype), vc)
    return out.reshape(B, Lc, H * dh)


def dwconv(u, w, b):
    C = u.shape[-1]
    y = lax.conv_general_dilated(u, w[:, None, :].astype(u.dtype), window_strides=(1,),
                                 padding=[(CONV_K // 2, CONV_K // 2)],
                                 dimension_numbers=('NWC', 'WIO', 'NWC'), feature_group_count=C)
    return y + b.astype(u.dtype)


def mlstm_init_state(B):
    N = N_DIR * B * M_HEADS
    return (jnp.zeros((N, M_HEAD_DIM, M_HEAD_DIM), jnp.float32),
            jnp.zeros((N, M_HEAD_DIM), jnp.float32),
            jnp.full((N,), -jnp.inf, jnp.float32))


def mlstm_chunkwise(q, k, v, i_pre, f_pre, state):
    N, T, dh = q.shape
    NC = T // CHUNK
    f32 = jnp.float32

    def chunks(u):
        return u.astype(f32).reshape((N, NC, CHUNK) + u.shape[2:]).swapaxes(0, 1)

    qc = chunks(q)
    kc = chunks(k) * dh ** -0.5
    vc = chunks(v)
    lf = chunks(jax.nn.log_sigmoid(f_pre.astype(f32)))
    ic = chunks(i_pre)
    lower = jnp.tril(jnp.ones((CHUNK, CHUNK), bool))

    def step(carry, inp):
        C, n, m = carry
        qq, kk, vv, lfc, ii = inp
        b = jnp.cumsum(lfc, axis=-1)
        dmat = jnp.where(lower, b[:, :, None] - b[:, None, :] + ii[:, None, :], -jnp.inf)
        inter = b + m[:, None]
        m_t = jnp.maximum(inter, jnp.max(dmat, axis=-1))
        w_inter = jnp.exp(inter - m_t)
        qk = jnp.einsum('ntd,nsd->nts', qq, kk) * jnp.exp(dmat - m_t[:, :, None])
        num = (w_inter[..., None] * jnp.einsum('nvd,ntd->ntv', C, qq)
               + jnp.einsum('nts,nsv->ntv', qk, vv))
        den = w_inter * jnp.einsum('nd,ntd->nt', n, qq) + qk.sum(-1)
        h = num / jnp.maximum(jnp.abs(den), jnp.exp(-m_t))[..., None]
        bL = b[:, -1]
        g = bL[:, None] - b + ii
        m_new = jnp.maximum(bL + m, jnp.max(g, axis=-1))
        a = jnp.exp(bL + m - m_new)
        w = jnp.exp(g - m_new[:, None])
        C_new = a[:, None, None] * C + jnp.einsum('ns,nsv,nsd->nvd', w, vv, kk)
        n_new = a[:, None] * n + jnp.einsum('ns,nsd->nd', w, kk)
        return (C_new, n_new, m_new), h

    state, h = lax.scan(step, state, (qc, kc, vc, lf, ic))
    h = h.swapaxes(0, 1).reshape(N, T, dh)
    return h.astype(q.dtype), state


def mlstm_bidir(q, k, v, i_pre, f_pre, state):
    B, T, _ = q.shape

    def heads(u):
        u = u.reshape(B, T, M_HEADS, M_HEAD_DIM).transpose(0, 2, 1, 3)
        return jnp.stack([u, jnp.flip(u, axis=2)]).reshape(N_DIR * B * M_HEADS, T, M_HEAD_DIM)

    def gates(gt):
        gt = gt.reshape(B, T, N_DIR, M_HEADS).transpose(2, 0, 3, 1)
        return jnp.stack([gt[0], jnp.flip(gt[1], axis=-1)]).reshape(N_DIR * B * M_HEADS, T)

    h, state = mlstm_chunkwise(heads(q), heads(k), heads(v), gates(i_pre), gates(f_pre), state)
    h = h.reshape(N_DIR, B, M_HEADS, T, M_HEAD_DIM)
    h = h[0] + jnp.flip(h[1], axis=2)
    return h.transpose(0, 2, 1, 3).reshape(B, T, M_WIDTH), state


def mlstm_post(h, o_pre, z, head_g):
    B, T, _ = h.shape
    h = jax.nn.sigmoid(o_pre) * h
    hh = rmsnorm(h.reshape(B, T, M_HEADS, M_HEAD_DIM), head_g.reshape(M_HEADS, M_HEAD_DIM))
    return hh.reshape(B, T, M_WIDTH) * jax.nn.silu(z)


def mlstm_qk(mq, mk, conv_w, conv_b):
    qk = jax.nn.silu(dwconv(jnp.concatenate([mq, mk], axis=-1), conv_w, conv_b))
    return jnp.split(qk, 2, axis=-1)


def hybrid_layer(x, ctx, mod_x, mod_c, tables, norm_g, w_in, conv_w, conv_b, gate_b, sink,
                 head_g, w_out, last):
    shift_x, scale_x, gate_x = mod_x
    shift_c, scale_c, gate_c = mod_c
    B, T, _ = x.shape
    Lc = ctx.shape[1]
    b_i, b_f = jnp.split(gate_b, 2)
    xn = rmsnorm(x, norm_g) * (1 + scale_x) + shift_x
    cn = rmsnorm(ctx, norm_g) * (1 + scale_c) + shift_c
    aq, ak, av, az, mq, mk, mv, mo, mz, mi, mf = project(xn, w_in)
    caq, cak, cav, caz, cmq, cmk, cmv, cmo, cmz, cmi, cmf = project(cn, w_in)

    q = rope_2d(aq.reshape(B, T, ATT_HEADS, ATT_HEAD_DIM), tables)
    k = rope_2d(ak.reshape(B, T, ATT_KV_HEADS, ATT_HEAD_DIM), tables)
    v = av.reshape(B, T, ATT_KV_HEADS, ATT_HEAD_DIM)
    kc = cak.reshape(B, Lc, ATT_KV_HEADS, ATT_HEAD_DIM)
    vc = cav.reshape(B, Lc, ATT_KV_HEADS, ATT_HEAD_DIM)
    att_x = window_ctx_attention(q, k, v, kc, vc, sink) * jax.nn.silu(az)

    cq, ck = mlstm_qk(cmq, cmk, conv_w, conv_b)
    xq, xk = mlstm_qk(mq, mk, conv_w, conv_b)
    h_ctx, ctx_state = mlstm_bidir(cq, ck, cmv, cmi + b_i, cmf + b_f, mlstm_init_state(B))
    h_lat, _ = mlstm_bidir(xq, xk, mv, mi + b_i, mf + b_f, ctx_state)
    m_x = mlstm_post(h_lat, mo, mz, head_g)

    out_x = jnp.concatenate([att_x, m_x], axis=-1) @ w_out
    x = x + gate_x * out_x
    if not last:
        qc = caq.reshape(B, Lc, ATT_HEADS, ATT_HEAD_DIM)
        att_c = ctx_attention(qc, kc, vc, sink) * jax.nn.silu(caz)
        m_c = mlstm_post(h_ctx, cmo, cmz, head_g)
        out_c = jnp.concatenate([att_c, m_c], axis=-1) @ w_out
        ctx = ctx + gate_c * out_c
    return x, ctx


def setup_inputs(seed: int = 0) -> dict:
    key = jax.random.key(seed)
    ks = jax.random.split(key, 16)
    f32 = jnp.float32
    nrm = lambda k, s: jax.random.normal(k, s, f32)
    f_bias = jnp.tile(jnp.linspace(3.0, 6.0, M_HEADS), N_DIR)
    gate_b = jnp.concatenate([0.1 * nrm(ks[8], (DEPTH, N_DIR * M_HEADS)),
                              f_bias[None, :] + 0.1 * nrm(ks[9], (DEPTH, N_DIR * M_HEADS))], axis=-1)
    return {
        "x": nrm(ks[0], (BATCH, SEQ, D_MODEL)),
        "c": nrm(ks[1], (BATCH, D_MODEL)),
        "ctx": nrm(ks[2], (BATCH, CTX_LEN, D_MODEL)),
        "c_ctx": nrm(ks[3], (D_MODEL,)),
        "w_ada": 0.5 * D_MODEL ** -0.5 * nrm(ks[4], (DEPTH, D_MODEL, 3 * D_MODEL)),
        "b_ada": 0.01 * nrm(ks[5], (DEPTH, 3 * D_MODEL)),
        "norm_g": 1.0 + 0.02 * nrm(ks[6], (DEPTH, D_MODEL)),
        "w_in": D_MODEL ** -0.5 * nrm(ks[7], (DEPTH, D_MODEL, IN_COLS)),
        "conv_w": CONV_K ** -0.5 * nrm(ks[10], (DEPTH, CONV_K, 2 * M_WIDTH)),
        "conv_b": 0.01 * nrm(ks[11], (DEPTH, 2 * M_WIDTH)),
        "gate_b": gate_b,
        "sink": 0.5 * nrm(ks[12], (DEPTH, ATT_HEADS)),
        "head_g": 1.0 + 0.02 * nrm(ks[13], (DEPTH, M_WIDTH)),
        "w_out": MIX_WIDTH ** -0.5 * nrm(ks[14], (DEPTH, MIX_WIDTH, D_MODEL)),
        "final_g": 1.0 + 0.02 * nrm(ks[15], (D_MODEL,)),
    }


def reference(x, c, ctx, c_ctx, w_ada, b_ada, norm_g, w_in, conv_w, conv_b, gate_b, sink,
              head_g, w_out, final_g):
    T = x.shape[1]
    tables = axial_rope_tables(T)
    sc = jax.nn.silu(c)
    scc = jax.nn.silu(c_ctx)
    for l in range(DEPTH):
        mod_x = jnp.split((sc @ w_ada[l] + b_ada[l])[:, None, :], 3, axis=-1)
        mod_c = jnp.split(scc @ w_ada[l] + b_ada[l], 3, axis=-1)
        x, ctx = hybrid_layer(x, ctx, mod_x, mod_c, tables, norm_g[l], w_in[l], conv_w[l],
                              conv_b[l], gate_b[l], sink[l], head_g[l], w_out[l],
                              last=(l == DEPTH - 1))
    return rmsnorm(x, final_g)
```

```python
import functools

import numpy as np
import jax
import jax.numpy as jnp
from jax import lax
from jax.experimental import pallas as pl
from jax.experimental.pallas import tpu as pltpu

D_MODEL = 1024
GRID_W = 64
ATT_HEADS = 8
ATT_KV_HEADS = 2
ATT_GROUP = ATT_HEADS // ATT_KV_HEADS
ATT_HEAD_DIM = 64
ATT_WIDTH = ATT_HEADS * ATT_HEAD_DIM
KV_WIDTH = ATT_KV_HEADS * ATT_HEAD_DIM
WINDOW = 128
BLOCK = 128
ROPE_BASE = 10000.0
M_HEADS = 4
M_HEAD_DIM = 128
M_WIDTH = M_HEADS * M_HEAD_DIM
CONV_K = 5
CHUNK = 128
N_DIR = 2
N_GATES = 2 * N_DIR * M_HEADS
EPS = 1e-6

LANES = 128
SUBLANES = 8
HALO = SUBLANES
ROW_TILE = 256
VMEM_LIMIT = 48 * 1024 * 1024

C_Q = 0
C_KX = C_Q + ATT_WIDTH
C_V = C_KX + ATT_WIDTH
C_AZ = C_V + KV_WIDTH
C_MQK = C_AZ + ATT_WIDTH
C_MV = C_MQK + 2 * M_WIDTH
C_MOZ = C_MV + M_WIDTH
C_G = C_MOZ + 2 * M_WIDTH
N_PROJ = C_G + LANES
KV_GROUP_COLS = ATT_GROUP * ATT_HEAD_DIM


def _sigmoid(x):
    return 1.0 / (1.0 + jnp.exp(-x))


def _silu(x):
    return x * _sigmoid(x)


def _log_sigmoid(x):
    return jnp.minimum(x, 0.0) - jnp.log(1.0 + jnp.exp(-jnp.abs(x)))


def _proj_columns():
    q0, k0 = 0, ATT_WIDTH
    v0 = k0 + KV_WIDTH
    z0 = v0 + KV_WIDTH
    m0 = z0 + ATT_WIDTH
    quarter = ATT_HEAD_DIM // 4

    def dims(second):
        base = quarter if second else 0
        return np.concatenate([base + np.arange(quarter), 2 * quarter + base + np.arange(quarter)])

    cols = []
    for kvh in range(ATT_KV_HEADS):
        for second in (False, True):
            for g in range(ATT_GROUP):
                cols.append(q0 + (kvh * ATT_GROUP + g) * ATT_HEAD_DIM + dims(second))
    for kvh in range(ATT_KV_HEADS):
        for second in (False, True):
            for g in range(ATT_GROUP):
                cols.append(k0 + kvh * ATT_HEAD_DIM + dims(second))
    cols.append(v0 + np.arange(KV_WIDTH))
    cols.append(z0 + np.arange(ATT_WIDTH))
    cols.append(m0 + np.arange(5 * M_WIDTH))
    return np.concatenate(cols).astype(np.int32)


def _rope_tables(t_len, ctx_len):
    rows = t_len // GRID_W
    row = jnp.repeat(jnp.arange(rows), GRID_W).astype(jnp.float32)
    col = jnp.tile(jnp.arange(GRID_W), rows).astype(jnp.float32)
    half = ATT_HEAD_DIM // 2
    inv = ROPE_BASE ** (-jnp.arange(0, half, 2, dtype=jnp.float32) / half)
    ang = jnp.concatenate([row[:, None] * inv, col[:, None] * inv], axis=-1)
    ang = jnp.tile(ang, (1, LANES // half))
    cos = jnp.concatenate([jnp.ones((ctx_len, LANES), jnp.float32), jnp.cos(ang)], axis=0)
    sin = jnp.concatenate([jnp.zeros((ctx_len, LANES), jnp.float32), jnp.sin(ang)], axis=0)
    return cos, sin


def _mod_kernel(c_ref, w_ref, b_ref, o_ref):
    sc = _silu(c_ref[...])
    o_ref[0] = jnp.dot(sc, w_ref[0], preferred_element_type=jnp.float32,
                       precision=lax.Precision.HIGHEST) + b_ref[0]


def _modulation(cc, w_ada, b_ada):
    depth, d, d3 = w_ada.shape
    rows = cc.shape[0]
    nt = d3 // d
    return pl.pallas_call(
        _mod_kernel,
        grid=(depth, nt),
        in_specs=[
            pl.BlockSpec((rows, d), lambda l, n: (0, 0)),
            pl.BlockSpec((1, d, d), lambda l, n: (l, 0, n)),
            pl.BlockSpec((1, 1, d), lambda l, n: (l, 0, n)),
        ],
        out_specs=pl.BlockSpec((1, rows, d), lambda l, n: (l, 0, n)),
        out_shape=jax.ShapeDtypeStruct((depth, rows, d3), jnp.float32),
        compiler_params=pltpu.CompilerParams(dimension_semantics=("arbitrary", "arbitrary")),
        name="adaln_modulation",
    )(cc, w_ada, b_ada.reshape(depth, 1, d3))


def _proj_kernel(xp_ref, x_ref, xn_ref, mod_ref, g_ref, w_ref, cos_ref, sin_ref, cw_ref, cb_ref,
                 ks_ref, gb_ref,
                 q_ref, kx_ref, v_ref, az_ref, mqk_ref, mv_ref, moz_ref, gate_ref,
                 *, tm, tiles_ctx, tiles_total):
    j = pl.program_id(1)
    seg_first = jnp.logical_or(j == 0, j == tiles_ctx)
    seg_last = jnp.logical_or(j == tiles_ctx - 1, j == tiles_total - 1)

    xe = jnp.concatenate([xp_ref[0], x_ref[0], xn_ref[0]], axis=0)
    ms = jnp.mean(xe * xe, axis=-1, keepdims=True)
    y = xe * lax.rsqrt(ms + EPS) * g_ref[...]
    shift = mod_ref[0, 0:1, :]
    scale = mod_ref[0, 1:2, :]
    xn_ext = (y * (1.0 + scale) + shift).astype(jnp.bfloat16)
    xn = xn_ext[HALO:HALO + tm]

    def proj(rows, c0, c1):
        return jnp.dot(rows, w_ref[:, c0:c1], preferred_element_type=jnp.float32)

    cos = cos_ref[...]
    sin = sin_ref[...]
    for grp in range(2 * ATT_KV_HEADS):
        yq = proj(xn, grp * KV_GROUP_COLS, (grp + 1) * KV_GROUP_COLS)
        first, second = yq[:, :LANES], yq[:, LANES:]
        dst = q_ref if grp < ATT_KV_HEADS else kx_ref
        c0 = (grp % ATT_KV_HEADS) * KV_GROUP_COLS
        dst[0, :, c0:c0 + LANES] = (first * cos - second * sin).astype(jnp.bfloat16)
        dst[0, :, c0 + LANES:c0 + 2 * LANES] = (second * cos + first * sin).astype(jnp.bfloat16)

    v_ref[0] = proj(xn, C_V, C_AZ).astype(jnp.bfloat16)
    az_ref[0] = proj(xn, C_AZ, C_MQK)

    ye = proj(xn_ext, C_MQK, C_MV)
    row = lax.broadcasted_iota(jnp.int32, (tm + 2 * HALO, 1), 0)
    keep = jnp.logical_and(jnp.logical_or(row >= HALO, jnp.logical_not(seg_first)),
                           jnp.logical_or(row < HALO + tm, jnp.logical_not(seg_last)))
    ye = jnp.where(keep, ye, 0.0)
    acc = jnp.broadcast_to(cb_ref[...], (tm, 2 * M_WIDTH))
    for t in range(CONV_K):
        lo = HALO - CONV_K // 2 + t
        acc = acc + cw_ref[t:t + 1, :] * ye[lo:lo + tm]
    mqk_ref[0] = (_silu(acc) * ks_ref[...]).astype(jnp.bfloat16)

    mv_ref[0] = proj(xn, C_MV, C_MOZ).astype(jnp.bfloat16)
    moz_ref[0] = proj(xn, C_MOZ, C_G)
    gate_ref[0] = proj(xn, C_G, N_PROJ) + gb_ref[...]


def _projection(h, mod_l, norm_g, w_p, cos, sin, conv_w, conv_b, kscale, gate_b, *, ctx_len):
    bsz, s_len, d = h.shape
    tm = ROW_TILE
    tiles_total = s_len // tm
    tiles_ctx = ctx_len // tm
    hb = tm // HALO
    n_halo = s_len // HALO
    ctx_row = mod_l.shape[0] - 1

    def row_map(b, j):
        return (b, j, 0)

    kernel = functools.partial(_proj_kernel, tm=tm, tiles_ctx=tiles_ctx, tiles_total=tiles_total)
    out_shapes = (
        jax.ShapeDtypeStruct((bsz, s_len, ATT_WIDTH), jnp.bfloat16),
        jax.ShapeDtypeStruct((bsz, s_len, ATT_WIDTH), jnp.bfloat16),
        jax.ShapeDtypeStruct((bsz, s_len, KV_WIDTH), jnp.bfloat16),
        jax.ShapeDtypeStruct((bsz, s_len, ATT_WIDTH), jnp.float32),
        jax.ShapeDtypeStruct((bsz, s_len, 2 * M_WIDTH), jnp.bfloat16),
        jax.ShapeDtypeStruct((bsz, s_len, M_WIDTH), jnp.bfloat16),
        jax.ShapeDtypeStruct((bsz, s_len, 2 * M_WIDTH), jnp.float32),
        jax.ShapeDtypeStruct((bsz, s_len, LANES), jnp.float32),
    )
    return pl.pallas_call(
        kernel,
        grid=(bsz, tiles_total),
        in_specs=[
            pl.BlockSpec((1, HALO, d), lambda b, j: (b, jnp.maximum(j * hb - 1, 0), 0)),
            pl.BlockSpec((1, tm, d), row_map),
            pl.BlockSpec((1, HALO, d), lambda b, j: (b, jnp.minimum((j + 1) * hb, n_halo - 1), 0)),
            pl.BlockSpec((1, 3, d), lambda b, j: (jnp.where(j < tiles_ctx, ctx_row, b), 0, 0)),
            pl.BlockSpec((1, d), lambda b, j: (0, 0)),
            pl.BlockSpec((d, N_PROJ), lambda b, j: (0, 0), pipeline_mode=pl.Buffered(1)),
            pl.BlockSpec((tm, LANES), lambda b, j: (j, 0)),
            pl.BlockSpec((tm, LANES), lambda b, j: (j, 0)),
            pl.BlockSpec((CONV_K, 2 * M_WIDTH), lambda b, j: (0, 0)),
            pl.BlockSpec((1, 2 * M_WIDTH), lambda b, j: (0, 0)),
            pl.BlockSpec((1, 2 * M_WIDTH), lambda b, j: (0, 0)),
            pl.BlockSpec((1, LANES), lambda b, j: (0, 0)),
        ],
        out_specs=[
            pl.BlockSpec((1, tm, ATT_WIDTH), row_map),
            pl.BlockSpec((1, tm, ATT_WIDTH), row_map),
            pl.BlockSpec((1, tm, KV_WIDTH), row_map),
            pl.BlockSpec((1, tm, ATT_WIDTH), row_map),
            pl.BlockSpec((1, tm, 2 * M_WIDTH), row_map),
            pl.BlockSpec((1, tm, M_WIDTH), row_map),
            pl.BlockSpec((1, tm, 2 * M_WIDTH), row_map),
            pl.BlockSpec((1, tm, LANES), row_map),
        ],
        out_shape=out_shapes,
        compiler_params=pltpu.CompilerParams(
            dimension_semantics=("arbitrary", "arbitrary"), vmem_limit_bytes=VMEM_LIMIT),
        name="norm_mod_projection",
    )(h, h, h, mod_l, norm_g, w_p, cos, sin, conv_w, conv_b, kscale, gate_b)


def _attn_kernel(sink_ref, q_ref, kp_ref, kc_ref, kn_ref, kctx_ref, vp_ref, vc_ref, vn_ref, vctx_ref,
                 az_ref, o_ref, *, blk0, ctx_blocks, total_blocks):
    qb = pl.program_id(1) + blk0
    is_ctx = qb < ctx_blocks
    n = qb - ctx_blocks
    lat_blocks = total_blocks - ctx_blocks
    ctx_len = ctx_blocks * BLOCK
    win = 3 * BLOCK
    neg = -jnp.inf

    i = lax.broadcasted_iota(jnp.int32, (BLOCK, win), 0)
    jx = lax.broadcasted_iota(jnp.int32, (BLOCK, win), 1)
    lo = jnp.where(n == 0, BLOCK, 0)
    hi = jnp.where(n == lat_blocks - 1, 2 * BLOCK, win)
    valid = (jx >= i) & (jx <= i + 2 * WINDOW) & (jx >= lo) & (jx < hi) & jnp.logical_not(is_ctx)
    bias = jnp.concatenate([jnp.where(valid, 0.0, neg), jnp.zeros((BLOCK, ctx_len), jnp.float32)], axis=1)
    bias = jnp.concatenate([bias] * ATT_GROUP, axis=0)

    k_all = jnp.concatenate([kp_ref[0], kc_ref[0], kn_ref[0], kctx_ref[0]], axis=0)
    v_all = jnp.concatenate([vp_ref[0], vc_ref[0], vn_ref[0], vctx_ref[0]], axis=0).astype(jnp.float32)
    v_rot = pltpu.roll(v_all, ATT_HEAD_DIM, axis=1)
    lane_v = lax.broadcasted_iota(jnp.int32, (1, LANES), 1)

    lane_q = lax.broadcasted_iota(jnp.int32, (1, KV_GROUP_COLS), 1)
    head_of_qlane = (lane_q % LANES) // (LANES // ATT_GROUP)
    head_of_olane = lane_q // ATT_HEAD_DIM
    row_head = lax.broadcasted_iota(jnp.int32, (ATT_GROUP * BLOCK, 1), 0) // BLOCK
    scale = ATT_HEAD_DIM ** -0.5

    for kvh in range(ATT_KV_HEADS):
        c0 = kvh * KV_GROUP_COLS
        qg = q_ref[0, :, c0:c0 + KV_GROUP_COLS]
        qm = jnp.concatenate(
            [jnp.where(head_of_qlane == g, qg, jnp.zeros_like(qg)) for g in range(ATT_GROUP)], axis=0)
        kx = k_all[:, c0:c0 + KV_GROUP_COLS]
        s = lax.dot_general(qm, kx, (((1,), (1,)), ((), ())),
                            preferred_element_type=jnp.float32) * scale + bias
        sink = jnp.zeros((ATT_GROUP * BLOCK, 1), jnp.float32)
        for g in range(ATT_GROUP):
            sink = jnp.where(row_head == g, sink_ref[kvh * ATT_GROUP + g], sink)
        m = jnp.maximum(jnp.max(s, axis=-1, keepdims=True), sink)
        e = jnp.exp(s - m)
        denom = jnp.sum(e, axis=-1, keepdims=True) + jnp.exp(sink - m)
        p = (e / denom).astype(jnp.bfloat16)
        own_half = (lane_v < ATT_HEAD_DIM) if kvh == 0 else (lane_v >= ATT_HEAD_DIM)
        v_two = jnp.where(own_half, v_all, v_rot).astype(jnp.bfloat16)
        v_exp = jnp.concatenate([v_two, v_two], axis=1)
        o = jnp.dot(p, v_exp, preferred_element_type=jnp.float32)
        out = jnp.zeros((BLOCK, KV_GROUP_COLS), jnp.float32)
        for g in range(ATT_GROUP):
            out = out + jnp.where(head_of_olane == g, o[g * BLOCK:(g + 1) * BLOCK], 0.0)
        az = az_ref[0, :, c0:c0 + KV_GROUP_COLS]
        o_ref[0, :, c0:c0 + KV_GROUP_COLS] = (out * _silu(az)).astype(jnp.bfloat16)


def _attention(sink, q, kx, v, az, *, ctx_len, skip_ctx):
    bsz, s_len, _ = q.shape
    ctx_blocks = ctx_len // BLOCK
    total_blocks = s_len // BLOCK
    blk0 = ctx_blocks if skip_ctx else 0
    lo, hi = ctx_blocks, total_blocks - 1

    def at(off):
        return lambda b, j: (b, jnp.clip(j + blk0 + off, lo, hi), 0)

    def cur(b, j):
        return (b, j + blk0, 0)

    def ctx(b, j):
        return (b, 0, 0)

    kernel = functools.partial(_attn_kernel, blk0=blk0, ctx_blocks=ctx_blocks, total_blocks=total_blocks)
    return pl.pallas_call(
        kernel,
        grid=(bsz, total_blocks - blk0),
        in_specs=[
            pl.BlockSpec(memory_space=pltpu.SMEM),
            pl.BlockSpec((1, BLOCK, ATT_WIDTH), cur),
            pl.BlockSpec((1, BLOCK, ATT_WIDTH), at(-1)),
            pl.BlockSpec((1, BLOCK, ATT_WIDTH), at(0)),
            pl.BlockSpec((1, BLOCK, ATT_WIDTH), at(1)),
            pl.BlockSpec((1, ctx_len, ATT_WIDTH), ctx),
            pl.BlockSpec((1, BLOCK, KV_WIDTH), at(-1)),
            pl.BlockSpec((1, BLOCK, KV_WIDTH), at(0)),
            pl.BlockSpec((1, BLOCK, KV_WIDTH), at(1)),
            pl.BlockSpec((1, ctx_len, KV_WIDTH), ctx),
            pl.BlockSpec((1, BLOCK, ATT_WIDTH), cur),
        ],
        out_specs=pl.BlockSpec((1, BLOCK, ATT_WIDTH), cur),
        out_shape=jax.ShapeDtypeStruct((bsz, s_len, ATT_WIDTH), jnp.bfloat16),
        compiler_params=pltpu.CompilerParams(
            dimension_semantics=("arbitrary", "arbitrary"), vmem_limit_bytes=VMEM_LIMIT),
        name="window_ctx_attention",
    )(sink, q, kx, kx, kx, kx, v, v, v, v, az)


def _mlstm_kernel(qkf_ref, vf_ref, gf_ref, qkb_ref, vb_ref, gb_ref, hf_ref, hb_ref,
                  ct_ref, n_ref, m_ref):
    @pl.when(pl.program_id(1) == 0)
    def _():
        ct_ref[...] = jnp.zeros_like(ct_ref)
        n_ref[...] = jnp.zeros_like(n_ref)
        m_ref[...] = jnp.full_like(m_ref, -jnp.inf)

    rows = lax.broadcasted_iota(jnp.int32, (CHUNK, CHUNK), 0)
    cols = lax.broadcasted_iota(jnp.int32, (CHUNK, CHUNK), 1)
    lower = cols <= rows
    upper = cols >= rows

    for d, (qk_ref, v_ref, g_ref, h_ref) in enumerate(
            ((qkf_ref, vf_ref, gf_ref, hf_ref), (qkb_ref, vb_ref, gb_ref, hb_ref))):
        valid = lower if d == 0 else upper
        valid_t = upper if d == 0 else lower
        gates = g_ref[0]
        gates_t = gates.T
        for hd in range(M_HEADS):
            r = d * M_HEADS + hd
            i_col = gates[:, r:r + 1]
            i_row = gates_t[r:r + 1, :]
            lf_col = _log_sigmoid(gates[:, N_DIR * M_HEADS + r:N_DIR * M_HEADS + r + 1])
            lf_row = _log_sigmoid(gates_t[N_DIR * M_HEADS + r:N_DIR * M_HEADS + r + 1, :])
            b_col = jnp.sum(jnp.where(valid, lf_row, 0.0), axis=1, keepdims=True)
            b_row = jnp.sum(jnp.where(valid_t, lf_col, 0.0), axis=0, keepdims=True)
            b_tot = jnp.sum(lf_row, axis=1, keepdims=True)
            m_prev = m_ref[r:r + 1, 0:1]

            qq = qk_ref[0, :, hd * M_HEAD_DIM:(hd + 1) * M_HEAD_DIM]
            kk = qk_ref[0, :, M_WIDTH + hd * M_HEAD_DIM:M_WIDTH + (hd + 1) * M_HEAD_DIM]
            vv = v_ref[0, :, hd * M_HEAD_DIM:(hd + 1) * M_HEAD_DIM]
            ct = ct_ref[r]
            nvec = n_ref[r:r + 1, :]

            dmat = jnp.where(valid, b_col - b_row + i_row, -jnp.inf)
            inter = b_col + m_prev
            m_t = jnp.maximum(inter, jnp.max(dmat, axis=1, keepdims=True))
            w_inter = jnp.exp(inter - m_t)
            qk = lax.dot_general(qq, kk, (((1,), (1,)), ((), ())),
                                 preferred_element_type=jnp.float32) * jnp.exp(dmat - m_t)
            num = (w_inter * jnp.dot(qq, ct.astype(jnp.bfloat16), preferred_element_type=jnp.float32)
                   + jnp.dot(qk.astype(jnp.bfloat16), vv, preferred_element_type=jnp.float32))
            den = (w_inter * jnp.sum(qq.astype(jnp.float32) * nvec, axis=1, keepdims=True)
                   + jnp.sum(qk, axis=1, keepdims=True))
            hout = num / jnp.maximum(jnp.abs(den), jnp.exp(-m_t))
            h_ref[0, :, hd * M_HEAD_DIM:(hd + 1) * M_HEAD_DIM] = hout

            g_col = b_tot - b_col + i_col
            m_new = jnp.maximum(b_tot + m_prev, jnp.max(g_col, axis=0, keepdims=True))
            a = jnp.exp(b_tot + m_prev - m_new)
            kw = kk.astype(jnp.float32) * jnp.exp(g_col - m_new)
            ct_ref[r] = a * ct + jnp.dot(kw.T.astype(jnp.bfloat16), vv, preferred_element_type=jnp.float32)
            n_ref[r:r + 1, :] = a * nvec + jnp.sum(kw, axis=0, keepdims=True)
            m_ref[r:r + 1, :] = jnp.broadcast_to(m_new, (1, LANES))


def _mlstm(mqk, mv, gates, *, ctx_len):
    bsz, s_len, _ = mqk.shape
    nc = s_len // CHUNK
    ncc = ctx_len // CHUNK

    def fwd(b, j):
        return (b, j, 0)

    def bwd(b, j):
        return (b, jnp.where(j < ncc, ncc - 1 - j, nc - 1 + ncc - j), 0)

    nstate = N_DIR * M_HEADS
    return pl.pallas_call(
        _mlstm_kernel,
        grid=(bsz, nc),
        in_specs=[
            pl.BlockSpec((1, CHUNK, 2 * M_WIDTH), fwd),
            pl.BlockSpec((1, CHUNK, M_WIDTH), fwd),
            pl.BlockSpec((1, CHUNK, LANES), fwd),
            pl.BlockSpec((1, CHUNK, 2 * M_WIDTH), bwd),
            pl.BlockSpec((1, CHUNK, M_WIDTH), bwd),
            pl.BlockSpec((1, CHUNK, LANES), bwd),
        ],
        out_specs=[
            pl.BlockSpec((1, CHUNK, M_WIDTH), fwd),
            pl.BlockSpec((1, CHUNK, M_WIDTH), bwd),
        ],
        out_shape=(jax.ShapeDtypeStruct((bsz, s_len, M_WIDTH), jnp.float32),
                   jax.ShapeDtypeStruct((bsz, s_len, M_WIDTH), jnp.float32)),
        scratch_shapes=[
            pltpu.VMEM((nstate, M_HEAD_DIM, M_HEAD_DIM), jnp.float32),
            pltpu.VMEM((nstate, M_HEAD_DIM), jnp.float32),
            pltpu.VMEM((nstate, LANES), jnp.float32),
        ],
        compiler_params=pltpu.CompilerParams(
            dimension_semantics=("arbitrary", "arbitrary"), vmem_limit_bytes=VMEM_LIMIT),
        name="mlstm_bidirectional_scan",
    )(mqk, mv, gates, mqk, mv, gates)


def _out_kernel(x_ref, att_ref, hf_ref, hb_ref, moz_ref, hg_ref, w_ref, mod_ref, *rest, final):
    if final:
        fg_ref, o_ref = rest
    else:
        (o_ref,) = rest
    hs = _sigmoid(moz_ref[0, :, :M_WIDTH]) * (hf_ref[0] + hb_ref[0])
    parts = []
    for hd in range(M_HEADS):
        hh = hs[:, hd * M_HEAD_DIM:(hd + 1) * M_HEAD_DIM]
        ms = jnp.mean(hh * hh, axis=-1, keepdims=True)
        parts.append(hh * lax.rsqrt(ms + EPS))
    mx = jnp.concatenate(parts, axis=1) * hg_ref[...] * _silu(moz_ref[0, :, M_WIDTH:])
    out = (jnp.dot(att_ref[0], w_ref[:ATT_WIDTH, :], preferred_element_type=jnp.float32)
           + jnp.dot(mx.astype(jnp.bfloat16), w_ref[ATT_WIDTH:, :], preferred_element_type=jnp.float32))
    xnew = x_ref[0] + mod_ref[0, 2:3, :] * out
    if final:
        ms = jnp.mean(xnew * xnew, axis=-1, keepdims=True)
        o_ref[0] = xnew * lax.rsqrt(ms + EPS) * fg_ref[...]
    else:
        o_ref[0] = xnew


def _output(h, att, hf, hb, moz, head_g, w_out, mod_l, final_g, *, ctx_len, final):
    bsz, s_len, d = h.shape
    tm = ROW_TILE
    tiles_total = s_len // tm
    tiles_ctx = ctx_len // tm
    t0 = tiles_ctx if final else 0
    ctx_row = mod_l.shape[0] - 1

    def row_map(b, j):
        return (b, j + t0, 0)

    in_specs = [
        pl.BlockSpec((1, tm, d), row_map),
        pl.BlockSpec((1, tm, ATT_WIDTH), row_map),
        pl.BlockSpec((1, tm, M_WIDTH), row_map),
        pl.BlockSpec((1, tm, M_WIDTH), row_map),
        pl.BlockSpec((1, tm, 2 * M_WIDTH), row_map),
        pl.BlockSpec((1, M_WIDTH), lambda b, j: (0, 0)),
        pl.BlockSpec((d, d), lambda b, j: (0, 0), pipeline_mode=pl.Buffered(1)),
        pl.BlockSpec((1, 3, d), lambda b, j: (jnp.where(j + t0 < tiles_ctx, ctx_row, b), 0, 0)),
    ]
    args = [h, att, hf, hb, moz, head_g, w_out, mod_l]
    if final:
        in_specs.append(pl.BlockSpec((1, d), lambda b, j: (0, 0)))
        args.append(final_g)
        out_rows = s_len - ctx_len
    else:
        out_rows = s_len
    return pl.pallas_call(
        functools.partial(_out_kernel, final=final),
        grid=(bsz, tiles_total - t0),
        in_specs=in_specs,
        out_specs=pl.BlockSpec((1, tm, d), lambda b, j: (b, j, 0)),
        out_shape=jax.ShapeDtypeStruct((bsz, out_rows, d), jnp.float32),
        compiler_params=pltpu.CompilerParams(
            dimension_semantics=("arbitrary", "arbitrary"), vmem_limit_bytes=VMEM_LIMIT),
        name="gate_norm_out_projection",
    )(*args)


def kernel(x, c, ctx, c_ctx, w_ada, b_ada, norm_g, w_in, conv_w, conv_b, gate_b, sink, head_g, w_out, final_g):
    bsz, t_len, d = x.shape
    ctx_len = ctx.shape[1]
    depth = w_in.shape[0]
    assert d == D_MODEL and t_len % ROW_TILE == 0 and ctx_len % ROW_TILE == 0 and t_len % GRID_W == 0

    h = jnp.concatenate([ctx, x], axis=1)
    mod_rows = -(-(bsz + 1) // SUBLANES) * SUBLANES
    cc = jnp.zeros((mod_rows, d), jnp.float32).at[:bsz].set(c).at[mod_rows - 1].set(c_ctx)
    mod = _modulation(cc, w_ada, b_ada).reshape(depth, mod_rows, 3, d)

    cols = _proj_columns()
    gate_cols = w_in[:, :, w_in.shape[2] - N_GATES:]
    w_p = jnp.concatenate(
        [jnp.take(w_in, cols, axis=2), gate_cols,
         jnp.zeros((depth, d, LANES - N_GATES), w_in.dtype)], axis=2).astype(jnp.bfloat16)
    w_o = w_out.astype(jnp.bfloat16)
    cos, sin = _rope_tables(t_len, ctx_len)
    kscale = jnp.concatenate([jnp.ones((1, M_WIDTH), jnp.float32),
                              jnp.full((1, M_WIDTH), M_HEAD_DIM ** -0.5, jnp.float32)], axis=1)
    gate_bias = jnp.pad(gate_b, ((0, 0), (0, LANES - N_GATES)))

    for l in range(depth):
        last = l == depth - 1
        q, kx, v, az, mqk, mv, moz, gates = _projection(
            h, mod[l], norm_g[l][None], w_p[l], cos, sin, conv_w[l], conv_b[l][None], kscale,
            gate_bias[l][None], ctx_len=ctx_len)
        att = _attention(sink[l], q, kx, v, az, ctx_len=ctx_len, skip_ctx=last)
        hf, hb = _mlstm(mqk, mv, gates, ctx_len=ctx_len)
        h = _output(h, att, hf, hb, moz, head_g[l][None], w_o[l], mod[l], final_g[None],
                    ctx_len=ctx_len, final=last)
    return h
```

```python
import functools

import numpy as np
import jax
import jax.numpy as jnp
from jax import lax
from jax.experimental import pallas as pl
from jax.experimental.pallas import tpu as pltpu

D_MODEL = 1024
GRID_W = 64
ATT_HEADS = 8
ATT_KV_HEADS = 2
ATT_GROUP = ATT_HEADS // ATT_KV_HEADS
ATT_HEAD_DIM = 64
ATT_WIDTH = ATT_HEADS * ATT_HEAD_DIM
KV_WIDTH = ATT_KV_HEADS * ATT_HEAD_DIM
WINDOW = 128
BLOCK = 128
ROPE_BASE = 10000.0
M_HEADS = 4
M_HEAD_DIM = 128
M_WIDTH = M_HEADS * M_HEAD_DIM
CONV_K = 5
CHUNK = 128
N_DIR = 2
N_GATES = 2 * N_DIR * M_HEADS
EPS = 1e-6

LANES = 128
SUBLANES = 8
HALO = SUBLANES
ROW_TILE = 256
VMEM_LIMIT = 48 * 1024 * 1024

C_Q = 0
C_KX = C_Q + ATT_WIDTH
C_V = C_KX + ATT_WIDTH
C_AZ = C_V + KV_WIDTH
C_MQK = C_AZ + ATT_WIDTH
C_MV = C_MQK + 2 * M_WIDTH
C_MOZ = C_MV + M_WIDTH
C_G = C_MOZ + 2 * M_WIDTH
N_PROJ = C_G + LANES
KV_GROUP_COLS = ATT_GROUP * ATT_HEAD_DIM


def _sigmoid(x):
    return 1.0 / (1.0 + jnp.exp(-x))


def _silu(x):
    return x * _sigmoid(x)


def _log_sigmoid(x):
    return jnp.minimum(x, 0.0) - jnp.log(1.0 + jnp.exp(-jnp.abs(x)))


def _proj_columns():
    q0, k0 = 0, ATT_WIDTH
    v0 = k0 + KV_WIDTH
    z0 = v0 + KV_WIDTH
    m0 = z0 + ATT_WIDTH
    quarter = ATT_HEAD_DIM // 4

    def dims(second):
        base = quarter if second else 0
        return np.concatenate([base + np.arange(quarter), 2 * quarter + base + np.arange(quarter)])

    cols = []
    for kvh in range(ATT_KV_HEADS):
        for second in (False, True):
            for g in range(ATT_GROUP):
                cols.append(q0 + (kvh * ATT_GROUP + g) * ATT_HEAD_DIM + dims(second))
    for kvh in range(ATT_KV_HEADS):
        for second in (False, True):
            for g in range(ATT_GROUP):
                cols.append(k0 + kvh * ATT_HEAD_DIM + dims(second))
    cols.append(v0 + np.arange(KV_WIDTH))
    cols.append(z0 + np.arange(ATT_WIDTH))
    cols.append(m0 + np.arange(5 * M_WIDTH))
    return np.concatenate(cols).astype(np.int32)


def _rope_tables(t_len, ctx_len):
    rows = t_len // GRID_W
    row = jnp.repeat(jnp.arange(rows), GRID_W).astype(jnp.float32)
    col = jnp.tile(jnp.arange(GRID_W), rows).astype(jnp.float32)
    half = ATT_HEAD_DIM // 2
    inv = ROPE_BASE ** (-jnp.arange(0, half, 2, dtype=jnp.float32) / half)
    ang = jnp.concatenate([row[:, None] * inv, col[:, None] * inv], axis=-1)
    ang = jnp.tile(ang, (1, LANES // half))
    cos = jnp.concatenate([jnp.ones((ctx_len, LANES), jnp.float32), jnp.cos(ang)], axis=0)
    sin = jnp.concatenate([jnp.zeros((ctx_len, LANES), jnp.float32), jnp.sin(ang)], axis=0)
    return cos, sin


def _mod_kernel(c_ref, w_ref, b_ref, o_ref):
    sc = _silu(c_ref[...])
    o_ref[0] = jnp.dot(sc, w_ref[0], preferred_element_type=jnp.float32,
                       precision=lax.Precision.HIGHEST) + b_ref[0]


def _modulation(cc, w_ada, b_ada):
    depth, d, d3 = w_ada.shape
    rows = cc.shape[0]
    nt = d3 // d
    return pl.pallas_call(
        _mod_kernel,
        grid=(depth, nt),
        in_specs=[
            pl.BlockSpec((rows, d), lambda l, n: (0, 0)),
            pl.BlockSpec((1, d, d), lambda l, n: (l, 0, n)),
            pl.BlockSpec((1, 1, d), lambda l, n: (l, 0, n)),
        ],
        out_specs=pl.BlockSpec((1, rows, d), lambda l, n: (l, 0, n)),
        out_shape=jax.ShapeDtypeStruct((depth, rows, d3), jnp.float32),
        compiler_params=pltpu.CompilerParams(dimension_semantics=("arbitrary", "arbitrary")),
        name="adaln_modulation",
    )(cc, w_ada, b_ada.reshape(depth, 1, d3))


def _proj_kernel(xp_ref, x_ref, xn_ref, mod_ref, g_ref, w_ref, cos_ref, sin_ref, cw_ref, cb_ref,
                 ks_ref, gb_ref,
                 q_ref, kx_ref, v_ref, az_ref, mqk_ref, mv_ref, moz_ref, gate_ref,
                 *, tm, tiles_ctx, tiles_total):
    j = pl.program_id(1)
    seg_first = jnp.logical_or(j == 0, j == tiles_ctx)
    seg_last = jnp.logical_or(j == tiles_ctx - 1, j == tiles_total - 1)

    xe = jnp.concatenate([xp_ref[0], x_ref[0], xn_ref[0]], axis=0)
    ms = jnp.mean(xe * xe, axis=-1, keepdims=True)
    y = xe * lax.rsqrt(ms + EPS) * g_ref[...]
    shift = mod_ref[0, 0:1, :]
    scale = mod_ref[0, 1:2, :]
    xn_ext = (y * (1.0 + scale) + shift).astype(jnp.bfloat16)
    xn = xn_ext[HALO:HALO + tm]

    def proj(rows, c0, c1):
        return jnp.dot(rows, w_ref[:, c0:c1], preferred_element_type=jnp.float32)

    cos = cos_ref[...]
    sin = sin_ref[...]
    for grp in range(2 * ATT_KV_HEADS):
        yq = proj(xn, grp * KV_GROUP_COLS, (grp + 1) * KV_GROUP_COLS)
        first, second = yq[:, :LANES], yq[:, LANES:]
        dst = q_ref if grp < ATT_KV_HEADS else kx_ref
        c0 = (grp % ATT_KV_HEADS) * KV_GROUP_COLS
        dst[0, :, c0:c0 + LANES] = (first * cos - second * sin).astype(jnp.bfloat16)
        dst[0, :, c0 + LANES:c0 + 2 * LANES] = (second * cos + first * sin).astype(jnp.bfloat16)

    v_ref[0] = proj(xn, C_V, C_AZ).astype(jnp.bfloat16)
    az_ref[0] = proj(xn, C_AZ, C_MQK)

    ye = proj(xn_ext, C_MQK, C_MV)
    row = lax.broadcasted_iota(jnp.int32, (tm + 2 * HALO, 1), 0)
    keep = jnp.logical_and(jnp.logical_or(row >= HALO, jnp.logical_not(seg_first)),
                           jnp.logical_or(row < HALO + tm, jnp.logical_not(seg_last)))
    ye = jnp.where(keep, ye, 0.0)
    acc = jnp.broadcast_to(cb_ref[...], (tm, 2 * M_WIDTH))
    for t in range(CONV_K):
        lo = HALO - CONV_K // 2 + t
        acc = acc + cw_ref[t:t + 1, :] * ye[lo:lo + tm]
    mqk_ref[0] = (_silu(acc) * ks_ref[...]).astype(jnp.bfloat16)

    mv_ref[0] = proj(xn, C_MV, C_MOZ).astype(jnp.bfloat16)
    moz_ref[0] = proj(xn, C_MOZ, C_G)
    gate_ref[0] = proj(xn, C_G, N_PROJ) + gb_ref[...]


def _projection(h, mod_l, norm_g, w_p, cos, sin, conv_w, conv_b, kscale, gate_b, *, ctx_len):
    bsz, s_len, d = h.shape
    tm = ROW_TILE
    tiles_total = s_len // tm
    tiles_ctx = ctx_len // tm
    hb = tm // HALO
    n_halo = s_len // HALO
    ctx_row = mod_l.shape[0] - 1

    def row_map(b, j):
        return (b, j, 0)

    kernel = functools.partial(_proj_kernel, tm=tm, tiles_ctx=tiles_ctx, tiles_total=tiles_total)
    out_shapes = (
        jax.ShapeDtypeStruct((bsz, s_len, ATT_WIDTH), jnp.bfloat16),
        jax.ShapeDtypeStruct((bsz, s_len, ATT_WIDTH), jnp.bfloat16),
        jax.ShapeDtypeStruct((bsz, s_len, KV_WIDTH), jnp.bfloat16),
        jax.ShapeDtypeStruct((bsz, s_len, ATT_WIDTH), jnp.float32),
        jax.ShapeDtypeStruct((bsz, s_len, 2 * M_WIDTH), jnp.bfloat16),
        jax.ShapeDtypeStruct((bsz, s_len, M_WIDTH), jnp.bfloat16),
        jax.ShapeDtypeStruct((bsz, s_len, 2 * M_WIDTH), jnp.float32),
        jax.ShapeDtypeStruct((bsz, s_len, LANES), jnp.float32),
    )
    return pl.pallas_call(
        kernel,
        grid=(bsz, tiles_total),
        in_specs=[
            pl.BlockSpec((1, HALO, d), lambda b, j: (b, jnp.maximum(j * hb - 1, 0), 0)),
            pl.BlockSpec((1, tm, d), row_map),
            pl.BlockSpec((1, HALO, d), lambda b, j: (b, jnp.minimum((j + 1) * hb, n_halo - 1), 0)),
            pl.BlockSpec((1, 3, d), lambda b, j: (jnp.where(j < tiles_ctx, ctx_row, b), 0, 0)),
            pl.BlockSpec((1, d), lambda b, j: (0, 0)),
            pl.BlockSpec((d, N_PROJ), lambda b, j: (0, 0), pipeline_mode=pl.Buffered(1)),
            pl.BlockSpec((tm, LANES), lambda b, j: (j, 0)),
            pl.BlockSpec((tm, LANES), lambda b, j: (j, 0)),
            pl.BlockSpec((CONV_K, 2 * M_WIDTH), lambda b, j: (0, 0)),
            pl.BlockSpec((1, 2 * M_WIDTH), lambda b, j: (0, 0)),
            pl.BlockSpec((1, 2 * M_WIDTH), lambda b, j: (0, 0)),
            pl.BlockSpec((1, LANES), lambda b, j: (0, 0)),
        ],
        out_specs=[
            pl.BlockSpec((1, tm, ATT_WIDTH), row_map),
            pl.BlockSpec((1, tm, ATT_WIDTH), row_map),
            pl.BlockSpec((1, tm, KV_WIDTH), row_map),
            pl.BlockSpec((1, tm, ATT_WIDTH), row_map),
            pl.BlockSpec((1, tm, 2 * M_WIDTH), row_map),
            pl.BlockSpec((1, tm, M_WIDTH), row_map),
            pl.BlockSpec((1, tm, 2 * M_WIDTH), row_map),
            pl.BlockSpec((1, tm, LANES), row_map),
        ],
        out_shape=out_shapes,
        compiler_params=pltpu.CompilerParams(
            dimension_semantics=("arbitrary", "arbitrary"), vmem_limit_bytes=VMEM_LIMIT),
        name="norm_mod_projection",
    )(h, h, h, mod_l, norm_g, w_p, cos, sin, conv_w, conv_b, kscale, gate_b)


def _attn_kernel(sink_ref, q_ref, kp_ref, kc_ref, kn_ref, kctx_ref, vp_ref, vc_ref, vn_ref, vctx_ref,
                 az_ref, o_ref, *, blk0, ctx_blocks, total_blocks):
    qb = pl.program_id(1) + blk0
    is_ctx = qb < ctx_blocks
    n = qb - ctx_blocks
    lat_blocks = total_blocks - ctx_blocks
    ctx_len = ctx_blocks * BLOCK
    win = 3 * BLOCK
    neg = -jnp.inf

    i = lax.broadcasted_iota(jnp.int32, (BLOCK, win), 0)
    jx = lax.broadcasted_iota(jnp.int32, (BLOCK, win), 1)
    lo = jnp.where(n == 0, BLOCK, 0)
    hi = jnp.where(n == lat_blocks - 1, 2 * BLOCK, win)
    valid = (jx >= i) & (jx <= i + 2 * WINDOW) & (jx >= lo) & (jx < hi) & jnp.logical_not(is_ctx)
    bias = jnp.concatenate([jnp.where(valid, 0.0, neg), jnp.zeros((BLOCK, ctx_len), jnp.float32)], axis=1)
    bias = jnp.concatenate([bias] * ATT_GROUP, axis=0)

    k_all = jnp.concatenate([kp_ref[0], kc_ref[0], kn_ref[0], kctx_ref[0]], axis=0)
    v_all = jnp.concatenate([vp_ref[0], vc_ref[0], vn_ref[0], vctx_ref[0]], axis=0).astype(jnp.float32)
    v_rot = pltpu.roll(v_all, ATT_HEAD_DIM, axis=1)
    lane_v = lax.broadcasted_iota(jnp.int32, (1, LANES), 1)

    lane_q = lax.broadcasted_iota(jnp.int32, (1, KV_GROUP_COLS), 1)
    head_of_qlane = (lane_q % LANES) // (LANES // ATT_GROUP)
    head_of_olane = lane_q // ATT_HEAD_DIM
    row_head = lax.broadcasted_iota(jnp.int32, (ATT_GROUP * BLOCK, 1), 0) // BLOCK
    scale = ATT_HEAD_DIM ** -0.5

    for kvh in range(ATT_KV_HEADS):
        c0 = kvh * KV_GROUP_COLS
        qg = q_ref[0, :, c0:c0 + KV_GROUP_COLS]
        qm = jnp.concatenate(
            [jnp.where(head_of_qlane == g, qg, jnp.zeros_like(qg)) for g in range(ATT_GROUP)], axis=0)
        kx = k_all[:, c0:c0 + KV_GROUP_COLS]
        s = lax.dot_general(qm, kx, (((1,), (1,)), ((), ())),
                            preferred_element_type=jnp.float32) * scale + bias
        sink = jnp.zeros((ATT_GROUP * BLOCK, 1), jnp.float32)
        for g in range(ATT_GROUP):
            sink = jnp.where(row_head == g, sink_ref[kvh * ATT_GROUP + g], sink)
        m = jnp.maximum(jnp.max(s, axis=-1, keepdims=True), sink)
        e = jnp.exp(s - m)
        denom = jnp.sum(e, axis=-1, keepdims=True) + jnp.exp(sink - m)
        p = (e / denom).astype(jnp.bfloat16)
        own_half = (lane_v < ATT_HEAD_DIM) if kvh == 0 else (lane_v >= ATT_HEAD_DIM)
        v_two = jnp.where(own_half, v_all, v_rot).astype(jnp.bfloat16)
        v_exp = jnp.concatenate([v_two, v_two], axis=1)
        o = jnp.dot(p, v_exp, preferred_element_type=jnp.float32)
        out = jnp.zeros((BLOCK, KV_GROUP_COLS), jnp.float32)
        for g in range(ATT_GROUP):
            out = out + jnp.where(head_of_olane == g, o[g * BLOCK:(g + 1) * BLOCK], 0.0)
        az = az_ref[0, :, c0:c0 + KV_GROUP_COLS]
        o_ref[0, :, c0:c0 + KV_GROUP_COLS] = (out * _silu(az)).astype(jnp.bfloat16)


def _attention(sink, q, kx, v, az, *, ctx_len, skip_ctx):
    bsz, s_len, _ = q.shape
    ctx_blocks = ctx_len // BLOCK
    total_blocks = s_len // BLOCK
    blk0 = ctx_blocks if skip_ctx else 0
    lo, hi = ctx_blocks, total_blocks - 1

    def at(off):
        return lambda b, j: (b, jnp.clip(j + blk0 + off, lo, hi), 0)

    def cur(b, j):
        return (b, j + blk0, 0)

    def ctx(b, j):
        return (b, 0, 0)

    kernel = functools.partial(_attn_kernel, blk0=blk0, ctx_blocks=ctx_blocks, total_blocks=total_blocks)
    return pl.pallas_call(
        kernel,
        grid=(bsz, total_blocks - blk0),
        in_specs=[
            pl.BlockSpec(memory_space=pltpu.SMEM),
            pl.BlockSpec((1, BLOCK, ATT_WIDTH), cur),
            pl.BlockSpec((1, BLOCK, ATT_WIDTH), at(-1)),
            pl.BlockSpec((1, BLOCK, ATT_WIDTH), at(0)),
            pl.BlockSpec((1, BLOCK, ATT_WIDTH), at(1)),
            pl.BlockSpec((1, ctx_len, ATT_WIDTH), ctx),
            pl.BlockSpec((1, BLOCK, KV_WIDTH), at(-1)),
            pl.BlockSpec((1, BLOCK, KV_WIDTH), at(0)),
            pl.BlockSpec((1, BLOCK, KV_WIDTH), at(1)),
            pl.BlockSpec((1, ctx_len, KV_WIDTH), ctx),
            pl.BlockSpec((1, BLOCK, ATT_WIDTH), cur),
        ],
        out_specs=pl.BlockSpec((1, BLOCK, ATT_WIDTH), cur),
        out_shape=jax.ShapeDtypeStruct((bsz, s_len, ATT_WIDTH), jnp.bfloat16),
        compiler_params=pltpu.CompilerParams(
            dimension_semantics=("arbitrary", "arbitrary"), vmem_limit_bytes=VMEM_LIMIT),
        name="window_ctx_attention",
    )(sink, q, kx, kx, kx, kx, v, v, v, v, az)


STATE_ROWS = M_HEAD_DIM + SUBLANES


def _split3(x):
    hi = x.astype(jnp.bfloat16)
    r1 = x - hi.astype(jnp.float32)
    mid = r1.astype(jnp.bfloat16)
    lo = (r1 - mid.astype(jnp.float32)).astype(jnp.bfloat16)
    return hi, mid, lo


def _mlstm_kernel(qkf_ref, vf_ref, gf_ref, qkb_ref, vb_ref, gb_ref, hf_ref, hb_ref, c_ref, m_ref):
    nst = N_DIR * M_HEADS

    @pl.when(pl.program_id(1) == 0)
    def _():
        c_ref[...] = jnp.zeros_like(c_ref)
        m_ref[...] = jnp.full_like(m_ref, -jnp.inf)

    rows = lax.broadcasted_iota(jnp.int32, (CHUNK, CHUNK), 0)
    cols = lax.broadcasted_iota(jnp.int32, (CHUNK, CHUNK), 1)
    le = rows <= cols
    ge = rows >= cols
    tri_le = jnp.where(le, 1.0, 0.0).astype(jnp.bfloat16)
    tri_ge = jnp.where(ge, 1.0, 0.0).astype(jnp.bfloat16)

    gates = (gf_ref[0], gb_ref[0])
    gates_t = (gates[0].T, gates[1].T)

    i8 = jnp.concatenate([gates_t[0][0:M_HEADS], gates_t[1][M_HEADS:nst]], axis=0)
    lf8 = _log_sigmoid(jnp.concatenate([gates_t[0][nst:nst + M_HEADS],
                                        gates_t[1][nst + M_HEADS:2 * nst]], axis=0))
    pre8 = sum(jnp.dot(p, tri_le, preferred_element_type=jnp.float32) for p in _split3(lf8))
    tot8 = pre8[:, CHUNK - 1:CHUNK]
    is_fwd = lax.broadcasted_iota(jnp.int32, (nst, 1), 0) < M_HEADS
    b8 = jnp.where(is_fwd, pre8, tot8 - pre8 + lf8)
    r8 = i8 - b8
    g8 = tot8 + r8
    m_prev = m_ref[...]
    m_new = jnp.maximum(tot8 + m_prev, jnp.max(g8, axis=1, keepdims=True))
    a8 = jnp.exp(tot8 + m_prev - m_new)
    w8 = jnp.exp(g8 - m_new)
    inter8 = b8 + m_prev
    m_ref[...] = m_new

    ones_row = jnp.where(lax.broadcasted_iota(jnp.int32, (SUBLANES, CHUNK), 0) == 0, 1.0, 0.0)

    for d, (qk_ref, v_ref, h_ref) in enumerate(((qkf_ref, vf_ref, hf_ref), (qkb_ref, vb_ref, hb_ref))):
        valid_t = le if d == 0 else ge
        lf_c = _log_sigmoid(gates[d])
        pre_c = sum(jnp.dot(tri_ge, p, preferred_element_type=jnp.float32) for p in _split3(lf_c))
        b_c = pre_c if d == 0 else pre_c[CHUNK - 1:CHUNK, :] - pre_c + lf_c
        r_c = pltpu.roll(gates[d], nst, axis=1) - b_c

        heads = range(M_HEADS)
        rs = [d * M_HEADS + hd for hd in heads]
        qs = [qk_ref[0, :, hd * M_HEAD_DIM:(hd + 1) * M_HEAD_DIM] for hd in heads]
        ks = [qk_ref[0, :, M_WIDTH + hd * M_HEAD_DIM:M_WIDTH + (hd + 1) * M_HEAD_DIM] for hd in heads]
        vts = [jnp.concatenate([v_ref[0, :, hd * M_HEAD_DIM:(hd + 1) * M_HEAD_DIM].astype(jnp.float32).T,
                                ones_row], axis=0) for hd in heads]
        cs = [c_ref[r] for r in rs]

        st = [lax.dot_general(k, q, (((1,), (1,)), ((), ())), preferred_element_type=jnp.float32)
              for k, q in zip(ks, qs)]
        dt = [jnp.where(valid_t, r_c[:, nst + r:nst + r + 1] + b8[r:r + 1, :], -jnp.inf) for r in rs]
        m_t = [jnp.maximum(inter8[r:r + 1, :], jnp.max(x, axis=0, keepdims=True)) for r, x in zip(rs, dt)]
        w_i = [jnp.exp(inter8[r:r + 1, :] - m) for r, m in zip(rs, m_t)]
        pt = [s_ * jnp.exp(x - m) for s_, x, m in zip(st, dt, m_t)]
        sp = [jnp.sum(p, axis=0, keepdims=True) for p in pt]
        cq = [lax.dot_general(c.astype(jnp.bfloat16), q, (((1,), (1,)), ((), ())),
                              preferred_element_type=jnp.float32) for c, q in zip(cs, qs)]
        hn = [jnp.dot(vt[:M_HEAD_DIM].astype(jnp.bfloat16), p.astype(jnp.bfloat16),
                      preferred_element_type=jnp.float32) for vt, p in zip(vts, pt)]
        for hd in heads:
            num = hn[hd] + w_i[hd] * cq[hd][:M_HEAD_DIM]
            den = sp[hd] + w_i[hd] * cq[hd][M_HEAD_DIM:M_HEAD_DIM + 1]
            ht = num / jnp.maximum(jnp.abs(den), jnp.exp(-m_t[hd]))
            h_ref[0, :, hd * M_HEAD_DIM:(hd + 1) * M_HEAD_DIM] = ht.T
        for hd, r in zip(heads, rs):
            vw = (vts[hd] * w8[r:r + 1, :]).astype(jnp.bfloat16)
            c_ref[r] = a8[r:r + 1, :] * cs[hd] + jnp.dot(vw, ks[hd], preferred_element_type=jnp.float32)


def _mlstm(mqk, mv, gates, *, ctx_len):
    bsz, s_len, _ = mqk.shape
    nc = s_len // CHUNK
    ncc = ctx_len // CHUNK

    def fwd(b, j):
        return (b, j, 0)

    def bwd(b, j):
        return (b, jnp.where(j < ncc, ncc - 1 - j, nc - 1 + ncc - j), 0)

    nstate = N_DIR * M_HEADS
    return pl.pallas_call(
        _mlstm_kernel,
        grid=(bsz, nc),
        in_specs=[
            pl.BlockSpec((1, CHUNK, 2 * M_WIDTH), fwd),
            pl.BlockSpec((1, CHUNK, M_WIDTH), fwd),
            pl.BlockSpec((1, CHUNK, LANES), fwd),
            pl.BlockSpec((1, CHUNK, 2 * M_WIDTH), bwd),
            pl.BlockSpec((1, CHUNK, M_WIDTH), bwd),
            pl.BlockSpec((1, CHUNK, LANES), bwd),
        ],
        out_specs=[
            pl.BlockSpec((1, CHUNK, M_WIDTH), fwd),
            pl.BlockSpec((1, CHUNK, M_WIDTH), bwd),
        ],
        out_shape=(jax.ShapeDtypeStruct((bsz, s_len, M_WIDTH), jnp.float32),
                   jax.ShapeDtypeStruct((bsz, s_len, M_WIDTH), jnp.float32)),
        scratch_shapes=[
            pltpu.VMEM((nstate, STATE_ROWS, M_HEAD_DIM), jnp.float32),
            pltpu.VMEM((nstate, LANES), jnp.float32),
        ],
        compiler_params=pltpu.CompilerParams(
            dimension_semantics=("arbitrary", "arbitrary"), vmem_limit_bytes=VMEM_LIMIT),
        name="mlstm_bidirectional_scan",
    )(mqk, mv, gates, mqk, mv, gates)


def _out_kernel(x_ref, att_ref, hf_ref, hb_ref, moz_ref, hg_ref, w_ref, mod_ref, *rest, final):
    if final:
        fg_ref, o_ref = rest
    else:
        (o_ref,) = rest
    hs = _sigmoid(moz_ref[0, :, :M_WIDTH]) * (hf_ref[0] + hb_ref[0])
    parts = []
    for hd in range(M_HEADS):
        hh = hs[:, hd * M_HEAD_DIM:(hd + 1) * M_HEAD_DIM]
        ms = jnp.mean(hh * hh, axis=-1, keepdims=True)
        parts.append(hh * lax.rsqrt(ms + EPS))
    mx = jnp.concatenate(parts, axis=1) * hg_ref[...] * _silu(moz_ref[0, :, M_WIDTH:])
    out = (jnp.dot(att_ref[0], w_ref[:ATT_WIDTH, :], preferred_element_type=jnp.float32)
           + jnp.dot(mx.astype(jnp.bfloat16), w_ref[ATT_WIDTH:, :], preferred_element_type=jnp.float32))
    xnew = x_ref[0] + mod_ref[0, 2:3, :] * out
    if final:
        ms = jnp.mean(xnew * xnew, axis=-1, keepdims=True)
        o_ref[0] = xnew * lax.rsqrt(ms + EPS) * fg_ref[...]
    else:
        o_ref[0] = xnew


def _output(h, att, hf, hb, moz, head_g, w_out, mod_l, final_g, *, ctx_len, final):
    bsz, s_len, d = h.shape
    tm = ROW_TILE
    tiles_total = s_len // tm
    tiles_ctx = ctx_len // tm
    t0 = tiles_ctx if final else 0
    ctx_row = mod_l.shape[0] - 1

    def row_map(b, j):
        return (b, j + t0, 0)

    in_specs = [
        pl.BlockSpec((1, tm, d), row_map),
        pl.BlockSpec((1, tm, ATT_WIDTH), row_map),
        pl.BlockSpec((1, tm, M_WIDTH), row_map),
        pl.BlockSpec((1, tm, M_WIDTH), row_map),
        pl.BlockSpec((1, tm, 2 * M_WIDTH), row_map),
        pl.BlockSpec((1, M_WIDTH), lambda b, j: (0, 0)),
        pl.BlockSpec((d, d), lambda b, j: (0, 0), pipeline_mode=pl.Buffered(1)),
        pl.BlockSpec((1, 3, d), lambda b, j: (jnp.where(j + t0 < tiles_ctx, ctx_row, b), 0, 0)),
    ]
    args = [h, att, hf, hb, moz, head_g, w_out, mod_l]
    if final:
        in_specs.append(pl.BlockSpec((1, d), lambda b, j: (0, 0)))
        args.append(final_g)
        out_rows = s_len - ctx_len
    else:
        out_rows = s_len
    return pl.pallas_call(
        functools.partial(_out_kernel, final=final),
        grid=(bsz, tiles_total - t0),
        in_specs=in_specs,
        out_specs=pl.BlockSpec((1, tm, d), lambda b, j: (b, j, 0)),
        out_shape=jax.ShapeDtypeStruct((bsz, out_rows, d), jnp.float32),
        compiler_params=pltpu.CompilerParams(
            dimension_semantics=("arbitrary", "arbitrary"), vmem_limit_bytes=VMEM_LIMIT),
        name="gate_norm_out_projection",
    )(*args)


def kernel(x, c, ctx, c_ctx, w_ada, b_ada, norm_g, w_in, conv_w, conv_b, gate_b, sink, head_g, w_out, final_g):
    bsz, t_len, d = x.shape
    ctx_len = ctx.shape[1]
    depth = w_in.shape[0]
    assert d == D_MODEL and t_len % ROW_TILE == 0 and ctx_len % ROW_TILE == 0 and t_len % GRID_W == 0

    h = jnp.concatenate([ctx, x], axis=1)
    mod_rows = -(-(bsz + 1) // SUBLANES) * SUBLANES
    cc = jnp.zeros((mod_rows, d), jnp.float32).at[:bsz].set(c).at[mod_rows - 1].set(c_ctx)
    mod = _modulation(cc, w_ada, b_ada).reshape(depth, mod_rows, 3, d)

    cols = _proj_columns()
    gate_cols = w_in[:, :, w_in.shape[2] - N_GATES:]
    w_p = jnp.concatenate(
        [jnp.take(w_in, cols, axis=2), gate_cols,
         jnp.zeros((depth, d, LANES - N_GATES), w_in.dtype)], axis=2).astype(jnp.bfloat16)
    w_o = w_out.astype(jnp.bfloat16)
    cos, sin = _rope_tables(t_len, ctx_len)
    kscale = jnp.concatenate([jnp.ones((1, M_WIDTH), jnp.float32),
                              jnp.full((1, M_WIDTH), M_HEAD_DIM ** -0.5, jnp.float32)], axis=1)
    gate_bias = jnp.pad(gate_b, ((0, 0), (0, LANES - N_GATES)))

    for l in range(depth):
        last = l == depth - 1
        q, kx, v, az, mqk, mv, moz, gates = _projection(
            h, mod[l], norm_g[l][None], w_p[l], cos, sin, conv_w[l], conv_b[l][None], kscale,
            gate_bias[l][None], ctx_len=ctx_len)
        att = _attention(sink[l], q, kx, v, az, ctx_len=ctx_len, skip_ctx=last)
        hf, hb = _mlstm(mqk, mv, gates, ctx_len=ctx_len)
        h = _output(h, att, hf, hb, moz, head_g[l][None], w_o[l], mod[l], final_g[None],
                    ctx_len=ctx_len, final=last)
    return h
```

```python
import functools

import numpy as np
import jax
import jax.numpy as jnp
from jax import lax
from jax.experimental import pallas as pl
from jax.experimental.pallas import tpu as pltpu

D_MODEL = 1024
GRID_W = 64
ATT_HEADS = 8
ATT_KV_HEADS = 2
ATT_GROUP = ATT_HEADS // ATT_KV_HEADS
ATT_HEAD_DIM = 64
ATT_WIDTH = ATT_HEADS * ATT_HEAD_DIM
KV_WIDTH = ATT_KV_HEADS * ATT_HEAD_DIM
WINDOW = 128
BLOCK = 128
ROPE_BASE = 10000.0
M_HEADS = 4
M_HEAD_DIM = 128
M_WIDTH = M_HEADS * M_HEAD_DIM
CONV_K = 5
CHUNK = 128
N_DIR = 2
N_GATES = 2 * N_DIR * M_HEADS
EPS = 1e-6

LANES = 128
SUBLANES = 8
HALO = SUBLANES
ROW_TILE = 256
VMEM_LIMIT = 48 * 1024 * 1024

C_Q = 0
C_KX = C_Q + ATT_WIDTH
C_V = C_KX + ATT_WIDTH
C_AZ = C_V + KV_WIDTH
C_MQK = C_AZ + ATT_WIDTH
C_MV = C_MQK + 2 * M_WIDTH
C_MOZ = C_MV + M_WIDTH
C_G = C_MOZ + 2 * M_WIDTH
N_PROJ = C_G + LANES
KV_GROUP_COLS = ATT_GROUP * ATT_HEAD_DIM
LOG2_E = 1.4426950408889634
Q_PRESCALE = ATT_HEAD_DIM ** -0.5 * LOG2_E


def _sigmoid(x):
    return 1.0 / (1.0 + jnp.exp(-x))


def _silu(x):
    return x * _sigmoid(x)


def _log_sigmoid(x):
    return jnp.minimum(x, 0.0) - jnp.log(1.0 + jnp.exp(-jnp.abs(x)))


def _rearrange_w_in(w_in):
    depth, d, n_in = w_in.shape
    quarter = ATT_HEAD_DIM // 4
    k0 = ATT_WIDTH
    v0 = k0 + KV_WIDTH
    wq = w_in[:, :, :k0].reshape(depth, d, ATT_KV_HEADS, ATT_GROUP, 2, 2, quarter)
    wq = wq.transpose(0, 1, 2, 5, 3, 4, 6).reshape(depth, d, ATT_WIDTH)
    wk = w_in[:, :, k0:v0].reshape(depth, d, ATT_KV_HEADS, 1, 2, 2, quarter)
    wk = jnp.broadcast_to(wk.transpose(0, 1, 2, 5, 3, 4, 6),
                          (depth, d, ATT_KV_HEADS, 2, ATT_GROUP, 2, quarter)).reshape(depth, d, ATT_WIDTH)
    pad = jnp.zeros((depth, d, LANES - N_GATES), w_in.dtype)
    return jnp.concatenate([wq, wk, w_in[:, :, v0:], pad], axis=2).astype(jnp.bfloat16)


def _rope_tables(t_len, ctx_len):
    rows = t_len // GRID_W
    row = jnp.repeat(jnp.arange(rows), GRID_W).astype(jnp.float32)
    col = jnp.tile(jnp.arange(GRID_W), rows).astype(jnp.float32)
    half = ATT_HEAD_DIM // 2
    inv = ROPE_BASE ** (-jnp.arange(0, half, 2, dtype=jnp.float32) / half)
    ang = jnp.concatenate([row[:, None] * inv, col[:, None] * inv], axis=-1)
    ang = jnp.tile(ang, (1, LANES // half))
    cos = jnp.concatenate([jnp.ones((ctx_len, LANES), jnp.float32), jnp.cos(ang)], axis=0)
    sin = jnp.concatenate([jnp.zeros((ctx_len, LANES), jnp.float32), jnp.sin(ang)], axis=0)
    return cos, sin


def _mod_kernel(c_ref, w_ref, b_ref, o_ref):
    sc = _silu(c_ref[...])
    o_ref[0] = jnp.dot(sc, w_ref[0], preferred_element_type=jnp.float32,
                       precision=lax.Precision.HIGHEST) + b_ref[0]


def _modulation(cc, w_ada, b_ada):
    depth, d, d3 = w_ada.shape
    rows = cc.shape[0]
    nt = d3 // d
    return pl.pallas_call(
        _mod_kernel,
        grid=(depth, nt),
        in_specs=[
            pl.BlockSpec((rows, d), lambda l, n: (0, 0)),
            pl.BlockSpec((1, d, d), lambda l, n: (l, 0, n)),
            pl.BlockSpec((1, 1, d), lambda l, n: (l, 0, n)),
        ],
        out_specs=pl.BlockSpec((1, rows, d), lambda l, n: (l, 0, n)),
        out_shape=jax.ShapeDtypeStruct((depth, rows, d3), jnp.float32),
        compiler_params=pltpu.CompilerParams(dimension_semantics=("arbitrary", "arbitrary")),
        name="adaln_modulation",
    )(cc, w_ada, b_ada.reshape(depth, 1, d3))


def _proj_kernel(xp_ref, x_ref, xn_ref, mod_ref, g_ref, w_ref, cos_ref, sin_ref, cw_ref, cb_ref,
                 ks_ref, gb_ref,
                 q_ref, kx_ref, v_ref, az_ref, mqk_ref, mv_ref, moz_ref, gate_ref, ye_ref,
                 *, tm, tiles_ctx, tiles_total):
    j = pl.program_id(1)
    seg_first = jnp.logical_or(j == 0, j == tiles_ctx)
    seg_last = jnp.logical_or(j == tiles_ctx - 1, j == tiles_total - 1)

    xe = jnp.concatenate([xp_ref[0], x_ref[0], xn_ref[0]], axis=0)
    ms = jnp.mean(xe * xe, axis=-1, keepdims=True)
    shift = mod_ref[0, 0:1, :]
    gain = g_ref[...] * (1.0 + mod_ref[0, 1:2, :])
    xn_ext = (xe * lax.rsqrt(ms + EPS) * gain + shift).astype(jnp.bfloat16)
    xn = xn_ext[HALO:HALO + tm]

    def proj(rows, c0, c1):
        return jnp.dot(rows, w_ref[:, c0:c1], preferred_element_type=jnp.float32)

    for grp in range(2 * ATT_KV_HEADS):
        is_q = grp < ATT_KV_HEADS
        cos = cos_ref[...] * Q_PRESCALE if is_q else cos_ref[...]
        sin = sin_ref[...] * Q_PRESCALE if is_q else sin_ref[...]
        yq = proj(xn, grp * KV_GROUP_COLS, (grp + 1) * KV_GROUP_COLS)
        first, second = yq[:, :LANES], yq[:, LANES:]
        dst = q_ref if is_q else kx_ref
        c0 = (grp % ATT_KV_HEADS) * KV_GROUP_COLS
        dst[0, :, c0:c0 + LANES] = (first * cos - second * sin).astype(jnp.bfloat16)
        dst[0, :, c0 + LANES:c0 + 2 * LANES] = (second * cos + first * sin).astype(jnp.bfloat16)

    v_ref[0] = proj(xn, C_V, C_AZ).astype(jnp.bfloat16)
    az_ref[0] = proj(xn, C_AZ, C_MQK).astype(jnp.bfloat16)

    ye = proj(xn_ext, C_MQK, C_MV)
    row = lax.broadcasted_iota(jnp.int32, (tm + 2 * HALO, 1), 0)
    keep = jnp.logical_and(jnp.logical_or(row >= HALO, jnp.logical_not(seg_first)),
                           jnp.logical_or(row < HALO + tm, jnp.logical_not(seg_last)))
    nb = tm // SUBLANES
    ye = jnp.where(keep, ye, 0.0).reshape(nb + 2, SUBLANES, 2 * M_WIDTH)
    sub = lax.broadcasted_iota(jnp.int32, (1, SUBLANES, 1), 1)
    acc = jnp.broadcast_to(cb_ref[...].reshape(1, 1, 2 * M_WIDTH), (nb, SUBLANES, 2 * M_WIDTH))
    for t in range(CONV_K):
        delta = t - CONV_K // 2
        w_t = cw_ref[t:t + 1, :].reshape(1, 1, 2 * M_WIDTH)
        if delta == 0:
            tap = ye[1:nb + 1]
        else:
            rot = pltpu.roll(ye, (-delta) % SUBLANES, axis=1)
            if delta > 0:
                tap = jnp.where(sub < SUBLANES - delta, rot[1:nb + 1], rot[2:nb + 2])
            else:
                tap = jnp.where(sub >= -delta, rot[1:nb + 1], rot[0:nb])
        acc = acc + w_t * tap
    acc = acc.reshape(tm, 2 * M_WIDTH)
    mqk_ref[0] = (_silu(acc) * ks_ref[...]).astype(jnp.bfloat16)

    mv_ref[0] = proj(xn, C_MV, C_MOZ).astype(jnp.bfloat16)
    moz_ref[0] = proj(xn, C_MOZ, C_G).astype(jnp.bfloat16)
    gate_ref[0] = proj(xn, C_G, N_PROJ) + gb_ref[...]


def _projection(h, mod_l, norm_g, w_p, cos, sin, conv_w, conv_b, kscale, gate_b, *, ctx_len):
    bsz, s_len, d = h.shape
    tm = ROW_TILE
    tiles_total = s_len // tm
    tiles_ctx = ctx_len // tm
    hb = tm // HALO
    n_halo = s_len // HALO
    ctx_row = mod_l.shape[0] - 1

    def row_map(b, j):
        return (b, j, 0)

    kernel = functools.partial(_proj_kernel, tm=tm, tiles_ctx=tiles_ctx, tiles_total=tiles_total)
    out_shapes = (
        jax.ShapeDtypeStruct((bsz, s_len, ATT_WIDTH), jnp.bfloat16),
        jax.ShapeDtypeStruct((bsz, s_len, ATT_WIDTH), jnp.bfloat16),
        jax.ShapeDtypeStruct((bsz, s_len, KV_WIDTH), jnp.bfloat16),
        jax.ShapeDtypeStruct((bsz, s_len, ATT_WIDTH), jnp.bfloat16),
        jax.ShapeDtypeStruct((bsz, s_len, 2 * M_WIDTH), jnp.bfloat16),
        jax.ShapeDtypeStruct((bsz, s_len, M_WIDTH), jnp.bfloat16),
        jax.ShapeDtypeStruct((bsz, s_len, 2 * M_WIDTH), jnp.bfloat16),
        jax.ShapeDtypeStruct((bsz, s_len, LANES), jnp.float32),
    )
    return pl.pallas_call(
        kernel,
        grid=(bsz, tiles_total),
        in_specs=[
            pl.BlockSpec((1, HALO, d), lambda b, j: (b, jnp.maximum(j * hb - 1, 0), 0)),
            pl.BlockSpec((1, tm, d), row_map),
            pl.BlockSpec((1, HALO, d), lambda b, j: (b, jnp.minimum((j + 1) * hb, n_halo - 1), 0)),
            pl.BlockSpec((1, 3, d), lambda b, j: (jnp.where(j < tiles_ctx, ctx_row, b), 0, 0)),
            pl.BlockSpec((1, d), lambda b, j: (0, 0)),
            pl.BlockSpec((d, N_PROJ), lambda b, j: (0, 0), pipeline_mode=pl.Buffered(1)),
            pl.BlockSpec((tm, LANES), lambda b, j: (j, 0)),
            pl.BlockSpec((tm, LANES), lambda b, j: (j, 0)),
            pl.BlockSpec((CONV_K, 2 * M_WIDTH), lambda b, j: (0, 0)),
            pl.BlockSpec((1, 2 * M_WIDTH), lambda b, j: (0, 0)),
            pl.BlockSpec((1, 2 * M_WIDTH), lambda b, j: (0, 0)),
            pl.BlockSpec((1, LANES), lambda b, j: (0, 0)),
        ],
        out_specs=[
            pl.BlockSpec((1, tm, ATT_WIDTH), row_map),
            pl.BlockSpec((1, tm, ATT_WIDTH), row_map),
            pl.BlockSpec((1, tm, KV_WIDTH), row_map),
            pl.BlockSpec((1, tm, ATT_WIDTH), row_map),
            pl.BlockSpec((1, tm, 2 * M_WIDTH), row_map),
            pl.BlockSpec((1, tm, M_WIDTH), row_map),
            pl.BlockSpec((1, tm, 2 * M_WIDTH), row_map),
            pl.BlockSpec((1, tm, LANES), row_map),
        ],
        out_shape=out_shapes,
        scratch_shapes=[pltpu.VMEM((tm + 2 * HALO, 2 * M_WIDTH), jnp.float32)],
        compiler_params=pltpu.CompilerParams(
            dimension_semantics=("arbitrary", "arbitrary"), vmem_limit_bytes=VMEM_LIMIT),
        name="norm_mod_projection",
    )(h, h, h, mod_l, norm_g, w_p, cos, sin, conv_w, conv_b, kscale, gate_b)


def _attn_kernel(sink_ref, q_ref, kp_ref, kc_ref, kn_ref, kctx_ref, vp_ref, vc_ref, vn_ref, vctx_ref,
                 az_ref, o_ref, *, blk0, ctx_blocks, total_blocks):
    qb = pl.program_id(1) + blk0
    is_ctx = qb < ctx_blocks
    n = qb - ctx_blocks
    lat_blocks = total_blocks - ctx_blocks
    ctx_len = ctx_blocks * BLOCK
    win = 3 * BLOCK
    neg = -jnp.inf

    i = lax.broadcasted_iota(jnp.int32, (BLOCK, win), 0)
    jx = lax.broadcasted_iota(jnp.int32, (BLOCK, win), 1)
    lo = jnp.where(n == 0, BLOCK, 0)
    hi = jnp.where(n == lat_blocks - 1, 2 * BLOCK, win)
    valid = (jx >= i) & (jx <= i + 2 * WINDOW) & (jx >= lo) & (jx < hi) & jnp.logical_not(is_ctx)
    bias = jnp.concatenate([jnp.where(valid, 0.0, neg)] * ATT_GROUP, axis=0)

    k_all = jnp.concatenate([kp_ref[0], kc_ref[0], kn_ref[0], kctx_ref[0]], axis=0)
    v_all = jnp.concatenate([vp_ref[0], vc_ref[0], vn_ref[0], vctx_ref[0]], axis=0).astype(jnp.float32)
    v_rot = pltpu.roll(v_all, ATT_HEAD_DIM, axis=1)
    lane_v = lax.broadcasted_iota(jnp.int32, (1, LANES), 1)
    low_half = lane_v < ATT_HEAD_DIM

    lane_q = lax.broadcasted_iota(jnp.int32, (1, KV_GROUP_COLS), 1)
    head_of_qlane = (lane_q % LANES) // (LANES // ATT_GROUP)
    row_head = lax.broadcasted_iota(jnp.int32, (ATT_GROUP * BLOCK, 1), 0) // BLOCK

    for kvh in range(ATT_KV_HEADS):
        c0 = kvh * KV_GROUP_COLS
        qg = q_ref[0, :, c0:c0 + KV_GROUP_COLS]
        qm = jnp.concatenate(
            [jnp.where(head_of_qlane == g, qg, jnp.zeros_like(qg)) for g in range(ATT_GROUP)], axis=0)
        kx = k_all[:, c0:c0 + KV_GROUP_COLS]
        s = lax.dot_general(qm, kx, (((1,), (1,)), ((), ())), preferred_element_type=jnp.float32)
        s_win = s[:, :win] + bias
        s_ctx = s[:, win:]
        sink = jnp.zeros((ATT_GROUP * BLOCK, 1), jnp.float32)
        for g in range(ATT_GROUP):
            sink = jnp.where(row_head == g, sink_ref[kvh * ATT_GROUP + g] * LOG2_E, sink)
        m = jnp.maximum(jnp.maximum(jnp.max(s_win, axis=-1, keepdims=True),
                                    jnp.max(s_ctx, axis=-1, keepdims=True)), sink)
        e_win = jnp.exp2(s_win - m)
        e_ctx = jnp.exp2(s_ctx - m)
        denom = (jnp.sum(e_win, axis=-1, keepdims=True) + jnp.sum(e_ctx, axis=-1, keepdims=True)
                 + jnp.exp2(sink - m))
        e = jnp.concatenate([e_win, e_ctx], axis=1).astype(jnp.bfloat16)
        own_half = low_half if kvh == 0 else jnp.logical_not(low_half)
        v_two = jnp.where(own_half, v_all, v_rot).astype(jnp.bfloat16)
        o = jnp.dot(e, v_two, preferred_element_type=jnp.float32) * (1.0 / denom)
        az = az_ref[0, :, c0:c0 + KV_GROUP_COLS].astype(jnp.float32)
        for pair in range(ATT_GROUP // 2):
            both = jnp.where(low_half, o[2 * pair * BLOCK:(2 * pair + 1) * BLOCK],
                             o[(2 * pair + 1) * BLOCK:(2 * pair + 2) * BLOCK])
            lo = c0 + pair * LANES
            o_ref[0, :, lo:lo + LANES] = (
                both * _silu(az[:, pair * LANES:(pair + 1) * LANES])).astype(jnp.bfloat16)


def _attention(sink, q, kx, v, az, *, ctx_len, skip_ctx):
    bsz, s_len, _ = q.shape
    ctx_blocks = ctx_len // BLOCK
    total_blocks = s_len // BLOCK
    blk0 = ctx_blocks if skip_ctx else 0
    lo, hi = ctx_blocks, total_blocks - 1

    def at(off):
        return lambda b, j: (b, jnp.clip(j + blk0 + off, lo, hi), 0)

    def cur(b, j):
        return (b, j + blk0, 0)

    def ctx(b, j):
        return (b, 0, 0)

    kernel = functools.partial(_attn_kernel, blk0=blk0, ctx_blocks=ctx_blocks, total_blocks=total_blocks)
    return pl.pallas_call(
        kernel,
        grid=(bsz, total_blocks - blk0),
        in_specs=[
            pl.BlockSpec(memory_space=pltpu.SMEM),
            pl.BlockSpec((1, BLOCK, ATT_WIDTH), cur),
            pl.BlockSpec((1, BLOCK, ATT_WIDTH), at(-1)),
            pl.BlockSpec((1, BLOCK, ATT_WIDTH), at(0)),
            pl.BlockSpec((1, BLOCK, ATT_WIDTH), at(1)),
            pl.BlockSpec((1, ctx_len, ATT_WIDTH), ctx),
            pl.BlockSpec((1, BLOCK, KV_WIDTH), at(-1)),
            pl.BlockSpec((1, BLOCK, KV_WIDTH), at(0)),
            pl.BlockSpec((1, BLOCK, KV_WIDTH), at(1)),
            pl.BlockSpec((1, ctx_len, KV_WIDTH), ctx),
            pl.BlockSpec((1, BLOCK, ATT_WIDTH), cur),
        ],
        out_specs=pl.BlockSpec((1, BLOCK, ATT_WIDTH), lambda b, j: (b, j, 0)),
        out_shape=jax.ShapeDtypeStruct((bsz, s_len - blk0 * BLOCK, ATT_WIDTH), jnp.bfloat16),
        compiler_params=pltpu.CompilerParams(
            dimension_semantics=("arbitrary", "arbitrary"), vmem_limit_bytes=VMEM_LIMIT),
        name="window_ctx_attention",
    )(sink, q, kx, kx, kx, kx, v, v, v, v, az)


STATE_ROWS = M_HEAD_DIM + SUBLANES


def _split3(x):
    hi = x.astype(jnp.bfloat16)
    r1 = x - hi.astype(jnp.float32)
    mid = r1.astype(jnp.bfloat16)
    lo = (r1 - mid.astype(jnp.float32)).astype(jnp.bfloat16)
    return hi, mid, lo


def _mlstm_kernel(qkf_ref, vf_ref, gf_ref, qkb_ref, vb_ref, gb_ref, hf_ref, hb_ref, c_ref, m_ref):
    nst = N_DIR * M_HEADS

    @pl.when(pl.program_id(1) == 0)
    def _():
        c_ref[...] = jnp.zeros_like(c_ref)
        m_ref[...] = jnp.full_like(m_ref, -jnp.inf)

    rows = lax.broadcasted_iota(jnp.int32, (CHUNK, CHUNK), 0)
    cols = lax.broadcasted_iota(jnp.int32, (CHUNK, CHUNK), 1)
    le = rows <= cols
    ge = rows >= cols
    tri_le = jnp.where(le, 1.0, 0.0).astype(jnp.bfloat16)
    tri_ge = jnp.where(ge, 1.0, 0.0).astype(jnp.bfloat16)

    gates = (gf_ref[0], gb_ref[0])
    gates_t = (gates[0].T, gates[1].T)

    i8 = jnp.concatenate([gates_t[0][0:M_HEADS], gates_t[1][M_HEADS:nst]], axis=0)
    lf8 = _log_sigmoid(jnp.concatenate([gates_t[0][nst:nst + M_HEADS],
                                        gates_t[1][nst + M_HEADS:2 * nst]], axis=0))
    pre8 = sum(jnp.dot(p, tri_le, preferred_element_type=jnp.float32) for p in _split3(lf8))
    tot8 = pre8[:, CHUNK - 1:CHUNK]
    is_fwd = lax.broadcasted_iota(jnp.int32, (nst, 1), 0) < M_HEADS
    b8 = jnp.where(is_fwd, pre8, tot8 - pre8 + lf8)
    r8 = i8 - b8
    g8 = tot8 + r8
    m_prev = m_ref[...]
    m_new = jnp.maximum(tot8 + m_prev, jnp.max(g8, axis=1, keepdims=True))
    a8 = jnp.exp(tot8 + m_prev - m_new)
    w8 = jnp.exp(g8 - m_new)
    inter8 = b8 + m_prev
    m_ref[...] = m_new

    ones_row = jnp.where(lax.broadcasted_iota(jnp.int32, (SUBLANES, CHUNK), 0) == 0, 1.0, 0.0)

    for d, (qk_ref, v_ref, h_ref) in enumerate(((qkf_ref, vf_ref, hf_ref), (qkb_ref, vb_ref, hb_ref))):
        valid_t = le if d == 0 else ge
        lf_c = _log_sigmoid(gates[d])
        pre_c = sum(jnp.dot(tri_ge, p, preferred_element_type=jnp.float32) for p in _split3(lf_c))
        b_c = pre_c if d == 0 else pre_c[CHUNK - 1:CHUNK, :] - pre_c + lf_c
        r_c = pltpu.roll(gates[d], nst, axis=1) - b_c

        heads = range(M_HEADS)
        rs = [d * M_HEADS + hd for hd in heads]
        qs = [qk_ref[0, :, hd * M_HEAD_DIM:(hd + 1) * M_HEAD_DIM] for hd in heads]
        ks = [qk_ref[0, :, M_WIDTH + hd * M_HEAD_DIM:M_WIDTH + (hd + 1) * M_HEAD_DIM] for hd in heads]
        vts = [jnp.concatenate([v_ref[0, :, hd * M_HEAD_DIM:(hd + 1) * M_HEAD_DIM].astype(jnp.float32).T,
                                ones_row], axis=0) for hd in heads]
        cs = [c_ref[r] for r in rs]

        st = [lax.dot_general(k, q, (((1,), (1,)), ((), ())), preferred_element_type=jnp.float32)
              for k, q in zip(ks, qs)]
        dt = [jnp.where(valid_t, r_c[:, nst + r:nst + r + 1] + b8[r:r + 1, :], -jnp.inf) for r in rs]
        m_t = [jnp.maximum(inter8[r:r + 1, :], jnp.max(x, axis=0, keepdims=True)) for r, x in zip(rs, dt)]
        w_i = [jnp.exp(inter8[r:r + 1, :] - m) for r, m in zip(rs, m_t)]
        pt = [s_ * jnp.exp(x - m) for s_, x, m in zip(st, dt, m_t)]
        sp = [jnp.sum(p, axis=0, keepdims=True) for p in pt]
        cq = [lax.dot_general(c.astype(jnp.bfloat16), q, (((1,), (1,)), ((), ())),
                              preferred_element_type=jnp.float32) for c, q in zip(cs, qs)]
        hn = [jnp.dot(vt[:M_HEAD_DIM].astype(jnp.bfloat16), p.astype(jnp.bfloat16),
                      preferred_element_type=jnp.float32) for vt, p in zip(vts, pt)]
        for hd in heads:
            num = hn[hd] + w_i[hd] * cq[hd][:M_HEAD_DIM]
            den = sp[hd] + w_i[hd] * cq[hd][M_HEAD_DIM:M_HEAD_DIM + 1]
            ht = num / jnp.maximum(jnp.abs(den), jnp.exp(-m_t[hd]))
            h_ref[0, :, hd * M_HEAD_DIM:(hd + 1) * M_HEAD_DIM] = ht.T.astype(h_ref.dtype)
        for hd, r in zip(heads, rs):
            vw = (vts[hd] * w8[r:r + 1, :]).astype(jnp.bfloat16)
            c_ref[r] = a8[r:r + 1, :] * cs[hd] + jnp.dot(vw, ks[hd], preferred_element_type=jnp.float32)


def _mlstm(mqk, mv, gates, *, ctx_len):
    bsz, s_len, _ = mqk.shape
    nc = s_len // CHUNK
    ncc = ctx_len // CHUNK

    def fwd(b, j):
        return (b, j, 0)

    def bwd(b, j):
        return (b, jnp.where(j < ncc, ncc - 1 - j, nc - 1 + ncc - j), 0)

    nstate = N_DIR * M_HEADS
    return pl.pallas_call(
        _mlstm_kernel,
        grid=(bsz, nc),
        in_specs=[
            pl.BlockSpec((1, CHUNK, 2 * M_WIDTH), fwd),
            pl.BlockSpec((1, CHUNK, M_WIDTH), fwd),
            pl.BlockSpec((1, CHUNK, LANES), fwd),
            pl.BlockSpec((1, CHUNK, 2 * M_WIDTH), bwd),
            pl.BlockSpec((1, CHUNK, M_WIDTH), bwd),
            pl.BlockSpec((1, CHUNK, LANES), bwd),
        ],
        out_specs=[
            pl.BlockSpec((1, CHUNK, M_WIDTH), fwd),
            pl.BlockSpec((1, CHUNK, M_WIDTH), bwd),
        ],
        out_shape=(jax.ShapeDtypeStruct((bsz, s_len, M_WIDTH), jnp.bfloat16),
                   jax.ShapeDtypeStruct((bsz, s_len, M_WIDTH), jnp.bfloat16)),
        scratch_shapes=[
            pltpu.VMEM((nstate, STATE_ROWS, M_HEAD_DIM), jnp.float32),
            pltpu.VMEM((nstate, LANES), jnp.float32),
        ],
        compiler_params=pltpu.CompilerParams(
            dimension_semantics=("arbitrary", "arbitrary"), vmem_limit_bytes=VMEM_LIMIT),
        name="mlstm_bidirectional_scan",
    )(mqk, mv, gates, mqk, mv, gates)


def _out_kernel(x_ref, att_ref, hf_ref, hb_ref, moz_ref, hg_ref, w_ref, mod_ref, *rest, final):
    if final:
        fg_ref, o_ref = rest
    else:
        (o_ref,) = rest
    mo = moz_ref[0, :, :M_WIDTH].astype(jnp.float32)
    mz = moz_ref[0, :, M_WIDTH:].astype(jnp.float32)
    hs = _sigmoid(mo) * (hf_ref[0].astype(jnp.float32) + hb_ref[0].astype(jnp.float32))
    parts = []
    for hd in range(M_HEADS):
        hh = hs[:, hd * M_HEAD_DIM:(hd + 1) * M_HEAD_DIM]
        ms = jnp.mean(hh * hh, axis=-1, keepdims=True)
        parts.append(hh * lax.rsqrt(ms + EPS))
    mx = jnp.concatenate(parts, axis=1) * hg_ref[...] * _silu(mz)
    out = (jnp.dot(att_ref[0], w_ref[:ATT_WIDTH, :], preferred_element_type=jnp.float32)
           + jnp.dot(mx.astype(jnp.bfloat16), w_ref[ATT_WIDTH:, :], preferred_element_type=jnp.float32))
    xnew = x_ref[0] + mod_ref[0, 2:3, :] * out
    if final:
        ms = jnp.mean(xnew * xnew, axis=-1, keepdims=True)
        o_ref[0] = xnew * lax.rsqrt(ms + EPS) * fg_ref[...]
    else:
        o_ref[0] = xnew


def _output(h, att, hf, hb, moz, head_g, w_out, mod_l, final_g, *, ctx_len, final):
    bsz, s_len, d = h.shape
    tm = ROW_TILE
    tiles_total = s_len // tm
    tiles_ctx = ctx_len // tm
    t0 = tiles_ctx if final else 0
    ctx_row = mod_l.shape[0] - 1

    def row_map(b, j):
        return (b, j + t0, 0)

    in_specs = [
        pl.BlockSpec((1, tm, d), row_map),
        pl.BlockSpec((1, tm, ATT_WIDTH), lambda b, j: (b, j, 0)),
        pl.BlockSpec((1, tm, M_WIDTH), row_map),
        pl.BlockSpec((1, tm, M_WIDTH), row_map),
        pl.BlockSpec((1, tm, 2 * M_WIDTH), row_map),
        pl.BlockSpec((1, M_WIDTH), lambda b, j: (0, 0)),
        pl.BlockSpec((d, d), lambda b, j: (0, 0), pipeline_mode=pl.Buffered(1)),
        pl.BlockSpec((1, 3, d), lambda b, j: (jnp.where(j + t0 < tiles_ctx, ctx_row, b), 0, 0)),
    ]
    args = [h, att, hf, hb, moz, head_g, w_out, mod_l]
    if final:
        in_specs.append(pl.BlockSpec((1, d), lambda b, j: (0, 0)))
        args.append(final_g)
        out_rows = s_len - ctx_len
    else:
        out_rows = s_len
    return pl.pallas_call(
        functools.partial(_out_kernel, final=final),
        grid=(bsz, tiles_total - t0),
        in_specs=in_specs,
        out_specs=pl.BlockSpec((1, tm, d), lambda b, j: (b, j, 0)),
        out_shape=jax.ShapeDtypeStruct((bsz, out_rows, d), jnp.float32),
        compiler_params=pltpu.CompilerParams(
            dimension_semantics=("arbitrary", "arbitrary"), vmem_limit_bytes=VMEM_LIMIT),
        name="gate_norm_out_projection",
    )(*args)


def kernel(x, c, ctx, c_ctx, w_ada, b_ada, norm_g, w_in, conv_w, conv_b, gate_b, sink, head_g, w_out, final_g):
    bsz, t_len, d = x.shape
    ctx_len = ctx.shape[1]
    depth = w_in.shape[0]
    assert d == D_MODEL and t_len % ROW_TILE == 0 and ctx_len % ROW_TILE == 0 and t_len % GRID_W == 0

    h = jnp.concatenate([ctx, x], axis=1)
    mod_rows = -(-(bsz + 1) // SUBLANES) * SUBLANES
    cc = jnp.zeros((mod_rows, d), jnp.float32).at[:bsz].set(c).at[mod_rows - 1].set(c_ctx)
    mod = _modulation(cc, w_ada, b_ada).reshape(depth, mod_rows, 3, d)

    w_p = _rearrange_w_in(w_in)
    w_o = w_out.astype(jnp.bfloat16)
    cos, sin = _rope_tables(t_len, ctx_len)
    kscale = jnp.concatenate([jnp.ones((1, M_WIDTH), jnp.float32),
                              jnp.full((1, M_WIDTH), M_HEAD_DIM ** -0.5, jnp.float32)], axis=1)
    gate_bias = jnp.pad(gate_b, ((0, 0), (0, LANES - N_GATES)))

    for l in range(depth):
        last = l == depth - 1
        q, kx, v, az, mqk, mv, moz, gates = _projection(
            h, mod[l], norm_g[l][None], w_p[l], cos, sin, conv_w[l], conv_b[l][None], kscale,
            gate_bias[l][None], ctx_len=ctx_len)
        att = _attention(sink[l], q, kx, v, az, ctx_len=ctx_len, skip_ctx=last)
        hf, hb = _mlstm(mqk, mv, gates, ctx_len=ctx_len)
        h = _output(h, att, hf, hb, moz, head_g[l][None], w_o[l], mod[l], final_g[None],
                    ctx_len=ctx_len, final=last)
    return h
```

```python
import functools
import math

import jax
import jax.numpy as jnp
from jax import lax
from jax.experimental import pallas as pl
from jax.experimental.pallas import tpu as pltpu

D_MODEL = 1024
GRID_W = 64
ATT_HEADS = 8
ATT_KV_HEADS = 2
ATT_GROUP = ATT_HEADS // ATT_KV_HEADS
ATT_HEAD_DIM = 64
ATT_WIDTH = ATT_HEADS * ATT_HEAD_DIM
KV_WIDTH = ATT_KV_HEADS * ATT_HEAD_DIM
WINDOW = 128
BLOCK = 128
ROPE_BASE = 10000.0
M_HEADS = 4
M_HEAD_DIM = 128
M_WIDTH = M_HEADS * M_HEAD_DIM
CONV_K = 5
CHUNK = 128
N_DIR = 2
N_GATES = 2 * N_DIR * M_HEADS
EPS = 1e-6

LANES = 128
SUBLANES = 8
HALO = SUBLANES
ROW_TILE = 256
BATCH_PER_STEP = 8
PROJ_BATCH = 4
PROJ_STACK = 1
VMEM_LIMIT = 56 * 1024 * 1024

C_Q = 0
C_KX = C_Q + ATT_WIDTH
C_V = C_KX + ATT_WIDTH
C_AZ = C_V + KV_WIDTH
C_MQK = C_AZ + ATT_WIDTH
C_MV = C_MQK + 2 * M_WIDTH
C_MOZ = C_MV + M_WIDTH
C_G = C_MOZ + 2 * M_WIDTH
N_PROJ = C_G + LANES
KV_GROUP_COLS = ATT_GROUP * ATT_HEAD_DIM
LOG2_E = 1.4426950408889634
Q_PRESCALE = ATT_HEAD_DIM ** -0.5 * LOG2_E


def _sigmoid(x):
    return 1.0 / (1.0 + jnp.exp(-x))


def _silu(x):
    return x * _sigmoid(x)


def _log_sigmoid(x):
    return jnp.minimum(x, 0.0) - jnp.log(1.0 + jnp.exp(-jnp.abs(x)))


def _rearrange_w_in(w_in):
    depth, d, n_in = w_in.shape
    quarter = ATT_HEAD_DIM // 4
    k0 = ATT_WIDTH
    v0 = k0 + KV_WIDTH
    wq = w_in[:, :, :k0].reshape(depth, d, ATT_KV_HEADS, ATT_GROUP, 2, 2, quarter)
    wq = wq.transpose(0, 1, 2, 5, 3, 4, 6).reshape(depth, d, ATT_WIDTH)
    wk = w_in[:, :, k0:v0].reshape(depth, d, ATT_KV_HEADS, 1, 2, 2, quarter)
    wk = jnp.broadcast_to(wk.transpose(0, 1, 2, 5, 3, 4, 6),
                          (depth, d, ATT_KV_HEADS, 2, ATT_GROUP, 2, quarter)).reshape(depth, d, ATT_WIDTH)
    pad = jnp.zeros((depth, d, LANES - N_GATES), w_in.dtype)
    return jnp.concatenate([wq, wk, w_in[:, :, v0:], pad], axis=2).astype(jnp.bfloat16)


def _rope_tables(t_len, ctx_len):
    rows = t_len // GRID_W
    row = jnp.repeat(jnp.arange(rows), GRID_W).astype(jnp.float32)
    col = jnp.tile(jnp.arange(GRID_W), rows).astype(jnp.float32)
    half = ATT_HEAD_DIM // 2
    inv = ROPE_BASE ** (-jnp.arange(0, half, 2, dtype=jnp.float32) / half)
    ang = jnp.concatenate([row[:, None] * inv, col[:, None] * inv], axis=-1)
    ang = jnp.tile(ang, (1, LANES // half))
    cos = jnp.concatenate([jnp.ones((ctx_len, LANES), jnp.float32), jnp.cos(ang)], axis=0)
    sin = jnp.concatenate([jnp.zeros((ctx_len, LANES), jnp.float32), jnp.sin(ang)], axis=0)
    return cos, sin


def _mod_kernel(c_ref, w_ref, b_ref, o_ref):
    sc = _silu(c_ref[...])
    o_ref[0] = jnp.dot(sc, w_ref[0], preferred_element_type=jnp.float32,
                       precision=lax.Precision.HIGHEST) + b_ref[0]


def _modulation(cc, w_ada, b_ada):
    depth, d, d3 = w_ada.shape
    rows = cc.shape[0]
    nt = d3 // d
    return pl.pallas_call(
        _mod_kernel,
        grid=(depth, nt),
        in_specs=[
            pl.BlockSpec((rows, d), lambda l, n: (0, 0)),
            pl.BlockSpec((1, d, d), lambda l, n: (l, 0, n)),
            pl.BlockSpec((1, 1, d), lambda l, n: (l, 0, n)),
        ],
        out_specs=pl.BlockSpec((1, rows, d), lambda l, n: (l, 0, n)),
        out_shape=jax.ShapeDtypeStruct((depth, rows, d3), jnp.float32),
        compiler_params=pltpu.CompilerParams(dimension_semantics=("arbitrary", "arbitrary")),
        name="adaln_modulation",
    )(cc, w_ada, b_ada.reshape(depth, 1, d3))


def _proj_kernel(*refs, tm, tiles_ctx, tiles_total):
    n_elem = refs[1].shape[0]
    for e0 in range(0, n_elem, PROJ_STACK):
        _proj_group(e0, min(PROJ_STACK, n_elem - e0), *refs, tm=tm, tiles_ctx=tiles_ctx, tiles_total=tiles_total)


def _proj_group(e0, ne, xp_ref, x_ref, xn_ref, mod_ref, modc_ref, g_ref, w_ref, cos_ref, sin_ref, cw_ref, cb_ref,
                ks_ref, gb_ref,
                q_ref, kx_ref, v_ref, az_ref, mqk_ref, mv_ref, moz_ref, gate_ref,
                *, tm, tiles_ctx, tiles_total):
    te = tm + 2 * HALO
    j = pl.program_id(1)
    is_ctx = j < tiles_ctx
    seg_first = jnp.logical_or(j == 0, j == tiles_ctx)
    seg_last = jnp.logical_or(j == tiles_ctx - 1, j == tiles_total - 1)

    ext = []
    for e in range(e0, e0 + ne):
        xe = jnp.concatenate([xp_ref[e], x_ref[e], xn_ref[e]], axis=0)
        ms = jnp.mean(xe * xe, axis=-1, keepdims=True)
        shift = jnp.where(is_ctx, modc_ref[0, 0:1, :], mod_ref[e, 0:1, :])
        scale = jnp.where(is_ctx, modc_ref[0, 1:2, :], mod_ref[e, 1:2, :])
        gain = g_ref[...] * (1.0 + scale)
        ext.append((xe * lax.rsqrt(ms + EPS) * gain + shift).astype(jnp.bfloat16))
    xn_ext = jnp.concatenate(ext, axis=0)
    xn = jnp.concatenate([x_[HALO:HALO + tm] for x_ in ext], axis=0)

    def proj(rows, c0, c1):
        return jnp.dot(rows, w_ref[:, c0:c1], preferred_element_type=jnp.float32)

    row = lax.broadcasted_iota(jnp.int32, (te, 1), 0)
    keep = jnp.logical_and(jnp.logical_or(row >= HALO, jnp.logical_not(seg_first)),
                           jnp.logical_or(row < HALO + tm, jnp.logical_not(seg_last)))
    nb = tm // SUBLANES
    sub = lax.broadcasted_iota(jnp.int32, (1, SUBLANES, 1), 1)

    def conv_chunk(c0, width):
        y_all = proj(xn_ext, C_MQK + c0, C_MQK + c0 + width)
        for i, e in enumerate(range(e0, e0 + ne)):
            ye = jnp.where(keep, y_all[i * te:(i + 1) * te], 0.0).reshape(nb + 2, SUBLANES, width)
            acc = jnp.broadcast_to(cb_ref[:, c0:c0 + width].reshape(1, 1, width), (nb, SUBLANES, width))
            for t in range(CONV_K):
                delta = t - CONV_K // 2
                w_t = cw_ref[t:t + 1, c0:c0 + width].reshape(1, 1, width)
                if delta == 0:
                    tap = ye[1:nb + 1]
                else:
                    rot = pltpu.roll(ye, (-delta) % SUBLANES, axis=1)
                    if delta > 0:
                        tap = jnp.where(sub < SUBLANES - delta, rot[1:nb + 1], rot[2:nb + 2])
                    else:
                        tap = jnp.where(sub >= -delta, rot[1:nb + 1], rot[0:nb])
                acc = acc + w_t * tap
            acc = acc.reshape(tm, width)
            mqk_ref[e, :, c0:c0 + width] = (_silu(acc) * ks_ref[:, c0:c0 + width]).astype(jnp.bfloat16)

    def rope_group(grp):
        is_q = grp < ATT_KV_HEADS
        cos = cos_ref[...] * Q_PRESCALE if is_q else cos_ref[...]
        sin = sin_ref[...] * Q_PRESCALE if is_q else sin_ref[...]
        yq = proj(xn, grp * KV_GROUP_COLS, (grp + 1) * KV_GROUP_COLS)
        dst = q_ref if is_q else kx_ref
        c0 = (grp % ATT_KV_HEADS) * KV_GROUP_COLS
        for i, e in enumerate(range(e0, e0 + ne)):
            first = yq[i * tm:(i + 1) * tm, :LANES]
            second = yq[i * tm:(i + 1) * tm, LANES:]
            dst[e, :, c0:c0 + LANES] = (first * cos - second * sin).astype(jnp.bfloat16)
            dst[e, :, c0 + LANES:c0 + 2 * LANES] = (second * cos + first * sin).astype(jnp.bfloat16)

    def plain(dst_ref, c0, width, off=0, bias_ref=None):
        y = proj(xn, c0, c0 + width)
        if bias_ref is not None:
            y = y + bias_ref[...]
        for i, e in enumerate(range(e0, e0 + ne)):
            dst_ref[e, :, off:off + width] = y[i * tm:(i + 1) * tm].astype(dst_ref.dtype)

    others = [functools.partial(rope_group, g) for g in range(2 * ATT_KV_HEADS)]
    others += [functools.partial(plain, az_ref, C_AZ, ATT_WIDTH),
               functools.partial(plain, mv_ref, C_MV, M_WIDTH),
               functools.partial(plain, moz_ref, C_MOZ, M_WIDTH),
               functools.partial(plain, moz_ref, C_MOZ + M_WIDTH, M_WIDTH, M_WIDTH)]
    per_chunk = 2
    cw = 2 * M_WIDTH * per_chunk // len(others)
    for i in range(len(others) // per_chunk):
        conv_chunk(i * cw, cw)
        for other in others[i * per_chunk:(i + 1) * per_chunk]:
            other()
    plain(v_ref, C_V, KV_WIDTH)
    plain(gate_ref, C_G, LANES, bias_ref=gb_ref)


def _projection(h, mod_l, norm_g, w_p, cos, sin, conv_w, conv_b, kscale, gate_b, *, ctx_len):
    bsz, s_len, d = h.shape
    tm = ROW_TILE
    ne = math.gcd(bsz, PROJ_BATCH)
    tiles_total = s_len // tm
    tiles_ctx = ctx_len // tm
    hb = tm // HALO
    n_halo = s_len // HALO
    ctx_blk = mod_l.shape[0] - 1

    def row_map(b, j):
        return (b, j, 0)

    kernel = functools.partial(_proj_kernel, tm=tm, tiles_ctx=tiles_ctx, tiles_total=tiles_total)
    out_shapes = (
        jax.ShapeDtypeStruct((bsz, s_len, ATT_WIDTH), jnp.bfloat16),
        jax.ShapeDtypeStruct((bsz, s_len, ATT_WIDTH), jnp.bfloat16),
        jax.ShapeDtypeStruct((bsz, s_len, KV_WIDTH), jnp.bfloat16),
        jax.ShapeDtypeStruct((bsz, s_len, ATT_WIDTH), jnp.bfloat16),
        jax.ShapeDtypeStruct((bsz, s_len, 2 * M_WIDTH), jnp.bfloat16),
        jax.ShapeDtypeStruct((bsz, s_len, M_WIDTH), jnp.bfloat16),
        jax.ShapeDtypeStruct((bsz, s_len, 2 * M_WIDTH), jnp.bfloat16),
        jax.ShapeDtypeStruct((bsz, s_len, LANES), jnp.float32),
    )
    return pl.pallas_call(
        kernel,
        grid=(bsz // ne, tiles_total),
        in_specs=[
            pl.BlockSpec((ne, HALO, d), lambda b, j: (b, jnp.maximum(j * hb - 1, 0), 0)),
            pl.BlockSpec((ne, tm, d), row_map),
            pl.BlockSpec((ne, HALO, d), lambda b, j: (b, jnp.minimum((j + 1) * hb, n_halo - 1), 0)),
            pl.BlockSpec((ne, 3, d), lambda b, j: (b, 0, 0)),
            pl.BlockSpec((1, 3, d), lambda b, j: (ctx_blk, 0, 0)),
            pl.BlockSpec((1, d), lambda b, j: (0, 0)),
            pl.BlockSpec((d, N_PROJ), lambda b, j: (0, 0), pipeline_mode=pl.Buffered(1)),
            pl.BlockSpec((tm, LANES), lambda b, j: (j, 0)),
            pl.BlockSpec((tm, LANES), lambda b, j: (j, 0)),
            pl.BlockSpec((CONV_K, 2 * M_WIDTH), lambda b, j: (0, 0)),
            pl.BlockSpec((1, 2 * M_WIDTH), lambda b, j: (0, 0)),
            pl.BlockSpec((1, 2 * M_WIDTH), lambda b, j: (0, 0)),
            pl.BlockSpec((1, LANES), lambda b, j: (0, 0)),
        ],
        out_specs=[
            pl.BlockSpec((ne, tm, ATT_WIDTH), row_map),
            pl.BlockSpec((ne, tm, ATT_WIDTH), row_map),
            pl.BlockSpec((ne, tm, KV_WIDTH), row_map),
            pl.BlockSpec((ne, tm, ATT_WIDTH), row_map),
            pl.BlockSpec((ne, tm, 2 * M_WIDTH), row_map),
            pl.BlockSpec((ne, tm, M_WIDTH), row_map),
            pl.BlockSpec((ne, tm, 2 * M_WIDTH), row_map),
            pl.BlockSpec((ne, tm, LANES), row_map),
        ],
        out_shape=out_shapes,
        compiler_params=pltpu.CompilerParams(
            dimension_semantics=("arbitrary", "arbitrary"), vmem_limit_bytes=VMEM_LIMIT),
        name="norm_mod_projection",
    )(h, h, h, mod_l, mod_l, norm_g, w_p, cos, sin, conv_w, conv_b, kscale, gate_b)


def _attn_kernel(sink_ref, q_ref, kp_ref, kc_ref, kn_ref, kctx_ref, vp_ref, vc_ref, vn_ref, vctx_ref,
                 az_ref, o_ref, *, blk0, ctx_blocks, total_blocks):
    qb = pl.program_id(1) + blk0
    is_ctx = qb < ctx_blocks
    n = qb - ctx_blocks
    lat_blocks = total_blocks - ctx_blocks
    ctx_len = ctx_blocks * BLOCK
    win = 3 * BLOCK
    neg = -jnp.inf

    i = lax.broadcasted_iota(jnp.int32, (BLOCK, win), 0)
    jx = lax.broadcasted_iota(jnp.int32, (BLOCK, win), 1)
    lo = jnp.where(n == 0, BLOCK, 0)
    hi = jnp.where(n == lat_blocks - 1, 2 * BLOCK, win)
    valid = (jx >= i) & (jx <= i + 2 * WINDOW) & (jx >= lo) & (jx < hi) & jnp.logical_not(is_ctx)
    bias = jnp.concatenate([jnp.where(valid, 0.0, neg)] * ATT_GROUP, axis=0)

    lane_v = lax.broadcasted_iota(jnp.int32, (1, LANES), 1)
    low_half = lane_v < ATT_HEAD_DIM
    lane_q = lax.broadcasted_iota(jnp.int32, (1, KV_GROUP_COLS), 1)
    head_of_qlane = (lane_q % LANES) // (LANES // ATT_GROUP)
    row_head = lax.broadcasted_iota(jnp.int32, (ATT_GROUP * BLOCK, 1), 0) // BLOCK
    n_keys = win + ctx_len
    ones = jnp.ones((n_keys, LANES), jnp.bfloat16)

    for e_i in range(q_ref.shape[0]):
        k_all = jnp.concatenate([kp_ref[e_i], kc_ref[e_i], kn_ref[e_i], kctx_ref[e_i]], axis=0)
        v_all = jnp.concatenate([vp_ref[e_i], vc_ref[e_i], vn_ref[e_i], vctx_ref[e_i]],
                                axis=0).astype(jnp.float32)
        v_rot = pltpu.roll(v_all, ATT_HEAD_DIM, axis=1)
        for kvh in range(ATT_KV_HEADS):
            c0 = kvh * KV_GROUP_COLS
            qg = q_ref[e_i, :, c0:c0 + KV_GROUP_COLS]
            qm = jnp.concatenate(
                [jnp.where(head_of_qlane == g, qg, jnp.zeros_like(qg)) for g in range(ATT_GROUP)], axis=0)
            kx = k_all[:, c0:c0 + KV_GROUP_COLS]
            s = lax.dot_general(qm, kx, (((1,), (1,)), ((), ())), preferred_element_type=jnp.float32)
            s_win = s[:, :win] + bias
            s_ctx = s[:, win:]
            sink = jnp.zeros((ATT_GROUP * BLOCK, 1), jnp.float32)
            for g in range(ATT_GROUP):
                sink = jnp.where(row_head == g, sink_ref[kvh * ATT_GROUP + g] * LOG2_E, sink)
            m = jnp.maximum(jnp.maximum(jnp.max(s_win, axis=-1, keepdims=True),
                                        jnp.max(s_ctx, axis=-1, keepdims=True)), sink)
            e = jnp.concatenate([jnp.exp2(s_win - m), jnp.exp2(s_ctx - m)], axis=1).astype(jnp.bfloat16)
            own_half = low_half if kvh == 0 else jnp.logical_not(low_half)
            v_two = jnp.where(own_half, v_all, v_rot).astype(jnp.bfloat16)
            res = jnp.dot(e, jnp.concatenate([v_two, ones], axis=1),
                          preferred_element_type=jnp.float32)
            o = res[:, :LANES] / (res[:, LANES:] + jnp.exp2(sink - m))
            az = az_ref[e_i, :, c0:c0 + KV_GROUP_COLS].astype(jnp.float32)
            for pair in range(ATT_GROUP // 2):
                both = jnp.where(low_half, o[2 * pair * BLOCK:(2 * pair + 1) * BLOCK],
                                 o[(2 * pair + 1) * BLOCK:(2 * pair + 2) * BLOCK])
                lo = c0 + pair * LANES
                o_ref[e_i, :, lo:lo + LANES] = (
                    both * _silu(az[:, pair * LANES:(pair + 1) * LANES])).astype(jnp.bfloat16)


def _attention(sink, q, kx, v, az, *, ctx_len, skip_ctx):
    bsz, s_len, _ = q.shape
    ctx_blocks = ctx_len // BLOCK
    total_blocks = s_len // BLOCK
    blk0 = ctx_blocks if skip_ctx else 0
    lo, hi = ctx_blocks, total_blocks - 1
    nb = math.gcd(bsz, BATCH_PER_STEP)

    def at(off):
        return lambda b, j: (b, jnp.clip(j + blk0 + off, lo, hi), 0)

    def cur(b, j):
        return (b, j + blk0, 0)

    def ctx(b, j):
        return (b, 0, 0)

    kernel = functools.partial(_attn_kernel, blk0=blk0, ctx_blocks=ctx_blocks, total_blocks=total_blocks)
    return pl.pallas_call(
        kernel,
        grid=(bsz // nb, total_blocks - blk0),
        in_specs=[
            pl.BlockSpec(memory_space=pltpu.SMEM),
            pl.BlockSpec((nb, BLOCK, ATT_WIDTH), cur),
            pl.BlockSpec((nb, BLOCK, ATT_WIDTH), at(-1)),
            pl.BlockSpec((nb, BLOCK, ATT_WIDTH), at(0)),
            pl.BlockSpec((nb, BLOCK, ATT_WIDTH), at(1)),
            pl.BlockSpec((nb, ctx_len, ATT_WIDTH), ctx),
            pl.BlockSpec((nb, BLOCK, KV_WIDTH), at(-1)),
            pl.BlockSpec((nb, BLOCK, KV_WIDTH), at(0)),
            pl.BlockSpec((nb, BLOCK, KV_WIDTH), at(1)),
            pl.BlockSpec((nb, ctx_len, KV_WIDTH), ctx),
            pl.BlockSpec((nb, BLOCK, ATT_WIDTH), cur),
        ],
        out_specs=pl.BlockSpec((nb, BLOCK, ATT_WIDTH), lambda b, j: (b, j, 0)),
        out_shape=jax.ShapeDtypeStruct((bsz, s_len - blk0 * BLOCK, ATT_WIDTH), jnp.bfloat16),
        compiler_params=pltpu.CompilerParams(
            dimension_semantics=("arbitrary", "arbitrary"), vmem_limit_bytes=VMEM_LIMIT),
        name="window_ctx_attention",
    )(sink, q, kx, kx, kx, kx, v, v, v, v, az)


STATE_ROWS = M_HEAD_DIM + SUBLANES


def _split2(x):
    hi = x.astype(jnp.bfloat16)
    lo = (x - hi.astype(jnp.float32)).astype(jnp.bfloat16)
    return hi, lo


def _mlstm_kernel(qkf_ref, vf_ref, gf_ref, qkb_ref, vb_ref, gb_ref, hf_ref, hb_ref, c_ref, m_ref):
    @pl.when(pl.program_id(1) == 0)
    def _():
        c_ref[...] = jnp.zeros_like(c_ref)
        m_ref[...] = jnp.full_like(m_ref, -jnp.inf)

    rows = lax.broadcasted_iota(jnp.int32, (CHUNK, CHUNK), 0)
    cols = lax.broadcasted_iota(jnp.int32, (CHUNK, CHUNK), 1)
    le = rows <= cols
    ge = rows >= cols
    tri_le = jnp.where(le, 1.0, 0.0).astype(jnp.bfloat16)
    tri_ge = jnp.where(ge, 1.0, 0.0).astype(jnp.bfloat16)

    for e in range(qkf_ref.shape[0]):
        _mlstm_element(e, qkf_ref, vf_ref, gf_ref, qkb_ref, vb_ref, gb_ref, hf_ref, hb_ref, c_ref, m_ref,
                       le, ge, tri_le, tri_ge)


def _mlstm_element(e, qkf_ref, vf_ref, gf_ref, qkb_ref, vb_ref, gb_ref, hf_ref, hb_ref, c_ref, m_ref,
                   le, ge, tri_le, tri_ge):
    nst = N_DIR * M_HEADS
    gates = (gf_ref[e], gb_ref[e])
    gates_t = (gates[0].T, gates[1].T)

    i8 = jnp.concatenate([gates_t[0][0:M_HEADS], gates_t[1][M_HEADS:nst]], axis=0)
    lf8 = _log_sigmoid(jnp.concatenate([gates_t[0][nst:nst + M_HEADS],
                                        gates_t[1][nst + M_HEADS:2 * nst]], axis=0))
    pre8 = sum(jnp.dot(p, tri_le, preferred_element_type=jnp.float32) for p in _split2(lf8))
    tot8 = pre8[:, CHUNK - 1:CHUNK]
    is_fwd = lax.broadcasted_iota(jnp.int32, (nst, 1), 0) < M_HEADS
    b8 = jnp.where(is_fwd, pre8, tot8 - pre8 + lf8)
    r8 = i8 - b8
    g8 = tot8 + r8
    m_prev = m_ref[e]
    m_new = jnp.maximum(tot8 + m_prev, jnp.max(g8, axis=1, keepdims=True))
    a8 = jnp.exp(tot8 + m_prev - m_new)
    w8 = jnp.exp(g8 - m_new)
    inter8 = b8 + m_prev
    m_ref[e] = m_new

    ones_row = jnp.where(lax.broadcasted_iota(jnp.int32, (SUBLANES, CHUNK), 0) == 0, 1.0, 0.0)

    for d, (qk_ref, v_ref, h_ref) in enumerate(((qkf_ref, vf_ref, hf_ref), (qkb_ref, vb_ref, hb_ref))):
        valid_t = le if d == 0 else ge
        lf_c = _log_sigmoid(gates[d])
        pre_c = sum(jnp.dot(tri_ge, p, preferred_element_type=jnp.float32) for p in _split2(lf_c))
        b_c = pre_c if d == 0 else pre_c[CHUNK - 1:CHUNK, :] - pre_c + lf_c
        r_c = pltpu.roll(gates[d], nst, axis=1) - b_c

        heads = range(M_HEADS)
        rs = [d * M_HEADS + hd for hd in heads]
        qs = [qk_ref[e, :, hd * M_HEAD_DIM:(hd + 1) * M_HEAD_DIM] for hd in heads]
        ks = [qk_ref[e, :, M_WIDTH + hd * M_HEAD_DIM:M_WIDTH + (hd + 1) * M_HEAD_DIM] for hd in heads]
        vts = [jnp.concatenate([v_ref[e, :, hd * M_HEAD_DIM:(hd + 1) * M_HEAD_DIM].astype(jnp.float32).T,
                                ones_row], axis=0) for hd in heads]
        cs = [c_ref[e, r] for r in rs]

        st = [lax.dot_general(k, q, (((1,), (1,)), ((), ())), preferred_element_type=jnp.float32)
              for k, q in zip(ks, qs)]
        dt = [jnp.where(valid_t, r_c[:, nst + r:nst + r + 1] + b8[r:r + 1, :], -jnp.inf) for r in rs]
        m_t = [jnp.maximum(inter8[r:r + 1, :], jnp.max(x, axis=0, keepdims=True)) for r, x in zip(rs, dt)]
        w_i = [jnp.exp(inter8[r:r + 1, :] - m) for r, m in zip(rs, m_t)]
        pt = [s_ * jnp.exp(x - m) for s_, x, m in zip(st, dt, m_t)]
        sp = [jnp.sum(p, axis=0, keepdims=True) for p in pt]
        cq = [lax.dot_general(c.astype(jnp.bfloat16), q, (((1,), (1,)), ((), ())),
                              preferred_element_type=jnp.float32) for c, q in zip(cs, qs)]
        hn = [jnp.dot(vt[:M_HEAD_DIM].astype(jnp.bfloat16), p.astype(jnp.bfloat16),
                      preferred_element_type=jnp.float32) for vt, p in zip(vts, pt)]
        for hd in heads:
            num = hn[hd] + w_i[hd] * cq[hd][:M_HEAD_DIM]
            den = sp[hd] + w_i[hd] * cq[hd][M_HEAD_DIM:M_HEAD_DIM + 1]
            ht = num / jnp.maximum(jnp.abs(den), jnp.exp(-m_t[hd]))
            h_ref[e, :, hd * M_HEAD_DIM:(hd + 1) * M_HEAD_DIM] = ht.T.astype(h_ref.dtype)
        for hd, r in zip(heads, rs):
            vw = (vts[hd] * w8[r:r + 1, :]).astype(jnp.bfloat16)
            c_ref[e, r] = a8[r:r + 1, :] * cs[hd] + jnp.dot(vw, ks[hd], preferred_element_type=jnp.float32)


def _mlstm(mqk, mv, gates, *, ctx_len):
    bsz, s_len, _ = mqk.shape
    nc = s_len // CHUNK
    ncc = ctx_len // CHUNK

    def fwd(b, j):
        return (b, j, 0)

    def bwd(b, j):
        return (b, jnp.where(j < ncc, ncc - 1 - j, nc - 1 + ncc - j), 0)

    nstate = N_DIR * M_HEADS
    nb = math.gcd(bsz, BATCH_PER_STEP)
    return pl.pallas_call(
        _mlstm_kernel,
        grid=(bsz // nb, nc),
        in_specs=[
            pl.BlockSpec((nb, CHUNK, 2 * M_WIDTH), fwd),
            pl.BlockSpec((nb, CHUNK, M_WIDTH), fwd),
            pl.BlockSpec((nb, CHUNK, LANES), fwd),
            pl.BlockSpec((nb, CHUNK, 2 * M_WIDTH), bwd),
            pl.BlockSpec((nb, CHUNK, M_WIDTH), bwd),
            pl.BlockSpec((nb, CHUNK, LANES), bwd),
        ],
        out_specs=[
            pl.BlockSpec((nb, CHUNK, M_WIDTH), fwd),
            pl.BlockSpec((nb, CHUNK, M_WIDTH), bwd),
        ],
        out_shape=(jax.ShapeDtypeStruct((bsz, s_len, M_WIDTH), jnp.bfloat16),
                   jax.ShapeDtypeStruct((bsz, s_len, M_WIDTH), jnp.bfloat16)),
        scratch_shapes=[
            pltpu.VMEM((nb, nstate, STATE_ROWS, M_HEAD_DIM), jnp.float32),
            pltpu.VMEM((nb, nstate, LANES), jnp.float32),
        ],
        compiler_params=pltpu.CompilerParams(
            dimension_semantics=("arbitrary", "arbitrary"), vmem_limit_bytes=VMEM_LIMIT),
        name="mlstm_bidirectional_scan",
    )(mqk, mv, gates, mqk, mv, gates)


def _out_kernel(x_ref, att_ref, hf_ref, hb_ref, moz_ref, hg_ref, w_ref, mod_ref, modc_ref, *rest,
                final, tiles_ctx, t0):
    if final:
        fg_ref, o_ref = rest
    else:
        (o_ref,) = rest
    ne, tm, _ = x_ref.shape
    is_ctx = pl.program_id(1) + t0 < tiles_ctx
    mxs = []
    for e in range(ne):
        mo = moz_ref[e, :, :M_WIDTH].astype(jnp.float32)
        mz = moz_ref[e, :, M_WIDTH:].astype(jnp.float32)
        hs = _sigmoid(mo) * (hf_ref[e].astype(jnp.float32) + hb_ref[e].astype(jnp.float32))
        parts = []
        for hd in range(M_HEADS):
            hh = hs[:, hd * M_HEAD_DIM:(hd + 1) * M_HEAD_DIM]
            ms = jnp.mean(hh * hh, axis=-1, keepdims=True)
            parts.append(hh * lax.rsqrt(ms + EPS))
        mxs.append((jnp.concatenate(parts, axis=1) * hg_ref[...] * _silu(mz)).astype(jnp.bfloat16))
    att = jnp.concatenate([att_ref[e] for e in range(ne)], axis=0)
    out = (jnp.dot(att, w_ref[:ATT_WIDTH, :], preferred_element_type=jnp.float32)
           + jnp.dot(jnp.concatenate(mxs, axis=0), w_ref[ATT_WIDTH:, :], preferred_element_type=jnp.float32))
    for e in range(ne):
        gate = jnp.where(is_ctx, modc_ref[0, 2:3, :], mod_ref[e, 2:3, :])
        xnew = x_ref[e] + gate * out[e * tm:(e + 1) * tm]
        if final:
            ms = jnp.mean(xnew * xnew, axis=-1, keepdims=True)
            o_ref[e] = xnew * lax.rsqrt(ms + EPS) * fg_ref[...]
        else:
            o_ref[e] = xnew


def _output(h, att, hf, hb, moz, head_g, w_out, mod_l, final_g, *, ctx_len, final):
    bsz, s_len, d = h.shape
    tm = ROW_TILE
    ne = math.gcd(bsz, PROJ_BATCH)
    tiles_total = s_len // tm
    tiles_ctx = ctx_len // tm
    t0 = tiles_ctx if final else 0
    ctx_blk = mod_l.shape[0] - 1

    def row_map(b, j):
        return (b, j + t0, 0)

    in_specs = [
        pl.BlockSpec((ne, tm, d), row_map),
        pl.BlockSpec((ne, tm, ATT_WIDTH), lambda b, j: (b, j, 0)),
        pl.BlockSpec((ne, tm, M_WIDTH), row_map),
        pl.BlockSpec((ne, tm, M_WIDTH), row_map),
        pl.BlockSpec((ne, tm, 2 * M_WIDTH), row_map),
        pl.BlockSpec((1, M_WIDTH), lambda b, j: (0, 0)),
        pl.BlockSpec((d, d), lambda b, j: (0, 0), pipeline_mode=pl.Buffered(1)),
        pl.BlockSpec((ne, 3, d), lambda b, j: (b, 0, 0)),
        pl.BlockSpec((1, 3, d), lambda b, j: (ctx_blk, 0, 0)),
    ]
    args = [h, att, hf, hb, moz, head_g, w_out, mod_l, mod_l]
    if final:
        in_specs.append(pl.BlockSpec((1, d), lambda b, j: (0, 0)))
        args.append(final_g)
        out_rows = s_len - ctx_len
    else:
        out_rows = s_len
    return pl.pallas_call(
        functools.partial(_out_kernel, final=final, tiles_ctx=tiles_ctx, t0=t0),
        grid=(bsz // ne, tiles_total - t0),
        in_specs=in_specs,
        out_specs=pl.BlockSpec((ne, tm, d), lambda b, j: (b, j, 0)),
        out_shape=jax.ShapeDtypeStruct((bsz, out_rows, d), jnp.float32),
        compiler_params=pltpu.CompilerParams(
            dimension_semantics=("arbitrary", "arbitrary"), vmem_limit_bytes=VMEM_LIMIT),
        name="gate_norm_out_projection",
    )(*args)


def kernel(x, c, ctx, c_ctx, w_ada, b_ada, norm_g, w_in, conv_w, conv_b, gate_b, sink, head_g, w_out, final_g):
    bsz, t_len, d = x.shape
    ctx_len = ctx.shape[1]
    depth = w_in.shape[0]
    assert d == D_MODEL and t_len % ROW_TILE == 0 and ctx_len % ROW_TILE == 0 and t_len % GRID_W == 0

    h = jnp.concatenate([ctx, x], axis=1)
    mod_rows = -(-(bsz + 1) // SUBLANES) * SUBLANES
    cc = jnp.zeros((mod_rows, d), jnp.float32).at[:bsz].set(c).at[mod_rows - 1].set(c_ctx)
    mod = _modulation(cc, w_ada, b_ada).reshape(depth, mod_rows, 3, d)

    w_p = _rearrange_w_in(w_in)
    w_o = w_out.astype(jnp.bfloat16)
    cos, sin = _rope_tables(t_len, ctx_len)
    kscale = jnp.concatenate([jnp.ones((1, M_WIDTH), jnp.float32),
                              jnp.full((1, M_WIDTH), M_HEAD_DIM ** -0.5, jnp.float32)], axis=1)
    gate_bias = jnp.pad(gate_b, ((0, 0), (0, LANES - N_GATES)))

    for l in range(depth):
        last = l == depth - 1
        q, kx, v, az, mqk, mv, moz, gates = _projection(
            h, mod[l], norm_g[l][None], w_p[l], cos, sin, conv_w[l], conv_b[l][None], kscale,
            gate_bias[l][None], ctx_len=ctx_len)
        att = _attention(sink[l], q, kx, v, az, ctx_len=ctx_len, skip_ctx=last)
        hf, hb = _mlstm(mqk, mv, gates, ctx_len=ctx_len)
        h = _output(h, att, hf, hb, moz, head_g[l][None], w_o[l], mod[l], final_g[None],
                    ctx_len=ctx_len, final=last)
    return h
```

```python
import functools
import math

import jax
import jax.numpy as jnp
from jax import lax
from jax.experimental import pallas as pl
from jax.experimental.pallas import tpu as pltpu

D_MODEL = 1024
GRID_W = 64
ATT_HEADS = 8
ATT_KV_HEADS = 2
ATT_GROUP = ATT_HEADS // ATT_KV_HEADS
ATT_HEAD_DIM = 64
ATT_WIDTH = ATT_HEADS * ATT_HEAD_DIM
KV_WIDTH = ATT_KV_HEADS * ATT_HEAD_DIM
WINDOW = 128
BLOCK = 128
ROPE_BASE = 10000.0
M_HEADS = 4
M_HEAD_DIM = 128
M_WIDTH = M_HEADS * M_HEAD_DIM
CONV_K = 5
CHUNK = 128
N_DIR = 2
N_GATES = 2 * N_DIR * M_HEADS
EPS = 1e-6

LANES = 128
SUBLANES = 8
HALO = SUBLANES
ROW_TILE = 256
BATCH_PER_STEP = 8
PROJ_BATCH = 4
PROJ_STACK = 1
VMEM_LIMIT = 56 * 1024 * 1024

W_IN_MAIN_START = ATT_WIDTH + 2 * KV_WIDTH
C_AZ = 0
C_MQK = C_AZ + ATT_WIDTH
C_MV = C_MQK + 2 * M_WIDTH
C_MOZ = C_MV + M_WIDTH
N_MAIN = C_MOZ + 2 * M_WIDTH
KV_GROUP_COLS = ATT_GROUP * ATT_HEAD_DIM
LOG2_E = 1.4426950408889634
Q_PRESCALE = ATT_HEAD_DIM ** -0.5 * LOG2_E


def _sigmoid(x):
    return 1.0 / (1.0 + jnp.exp(-x))


def _silu(x):
    return x * _sigmoid(x)


def _log_sigmoid(x):
    return jnp.minimum(x, 0.0) - jnp.log(1.0 + jnp.exp(-jnp.abs(x)))


def _split_w_in(w_in):
    depth, d, n_in = w_in.shape
    quarter = ATT_HEAD_DIM // 4
    k0 = ATT_WIDTH
    v0 = k0 + KV_WIDTH
    wq = w_in[:, :, :k0].reshape(depth, d, ATT_KV_HEADS, ATT_GROUP, 2, 2, quarter)
    wq = wq.transpose(0, 1, 2, 5, 3, 4, 6).reshape(depth, d, ATT_WIDTH)
    wk = w_in[:, :, k0:v0].reshape(depth, d, ATT_KV_HEADS, 1, 2, 2, quarter)
    wk = jnp.broadcast_to(wk.transpose(0, 1, 2, 5, 3, 4, 6),
                          (depth, d, ATT_KV_HEADS, 2, ATT_GROUP, 2, quarter)).reshape(depth, d, ATT_WIDTH)
    n_main_end = W_IN_MAIN_START + N_MAIN
    w_qk = jnp.concatenate([wq, wk], axis=2).astype(jnp.bfloat16)
    w_main = w_in[:, :, W_IN_MAIN_START:n_main_end].astype(jnp.bfloat16)
    pad = jnp.zeros((depth, d, LANES - N_GATES), w_in.dtype)
    w_vg = jnp.concatenate([w_in[:, :, v0:W_IN_MAIN_START], w_in[:, :, n_main_end:], pad],
                           axis=2).astype(jnp.bfloat16)
    return w_qk, w_main, w_vg


def _rope_tables(t_len, ctx_len):
    rows = t_len // GRID_W
    row = jnp.repeat(jnp.arange(rows), GRID_W).astype(jnp.float32)
    col = jnp.tile(jnp.arange(GRID_W), rows).astype(jnp.float32)
    half = ATT_HEAD_DIM // 2
    inv = ROPE_BASE ** (-jnp.arange(0, half, 2, dtype=jnp.float32) / half)
    ang = jnp.concatenate([row[:, None] * inv, col[:, None] * inv], axis=-1)
    ang = jnp.tile(ang, (1, LANES // half))
    cos = jnp.concatenate([jnp.ones((ctx_len, LANES), jnp.float32), jnp.cos(ang)], axis=0)
    sin = jnp.concatenate([jnp.zeros((ctx_len, LANES), jnp.float32), jnp.sin(ang)], axis=0)
    return cos, sin


def _mod_kernel(c_ref, w_ref, b_ref, o_ref):
    sc = _silu(c_ref[...])
    o_ref[0] = jnp.dot(sc, w_ref[0], preferred_element_type=jnp.float32,
                       precision=lax.Precision.HIGHEST) + b_ref[0]


def _modulation(cc, w_ada, b_ada):
    depth, d, d3 = w_ada.shape
    rows = cc.shape[0]
    nt = d3 // d
    return pl.pallas_call(
        _mod_kernel,
        grid=(depth, nt),
        in_specs=[
            pl.BlockSpec((rows, d), lambda l, n: (0, 0)),
            pl.BlockSpec((1, d, d), lambda l, n: (l, 0, n)),
            pl.BlockSpec((1, 1, d), lambda l, n: (l, 0, n)),
        ],
        out_specs=pl.BlockSpec((1, rows, d), lambda l, n: (l, 0, n)),
        out_shape=jax.ShapeDtypeStruct((depth, rows, d3), jnp.float32),
        compiler_params=pltpu.CompilerParams(dimension_semantics=("arbitrary", "arbitrary")),
        name="adaln_modulation",
    )(cc, w_ada, b_ada.reshape(depth, 1, d3))


def _proj_kernel(*refs, tm, tiles_ctx, tiles_total):
    n_elem = refs[1].shape[0]
    for e0 in range(0, n_elem, PROJ_STACK):
        _proj_group(e0, min(PROJ_STACK, n_elem - e0), *refs, tm=tm, tiles_ctx=tiles_ctx, tiles_total=tiles_total)


def _proj_group(e0, ne, xp_ref, x_ref, xn_ref, mod_ref, modc_ref, g_ref, wqk_ref, wm_ref, wvg_ref,
                cos_ref, sin_ref, cw_ref, cb_ref,
                ks_ref, gb_ref,
                q_ref, kx_ref, v_ref, az_ref, mqk_ref, mv_ref, moz_ref, gate_ref,
                *, tm, tiles_ctx, tiles_total):
    te = tm + 2 * HALO
    j = pl.program_id(1)
    is_ctx = j < tiles_ctx
    seg_first = jnp.logical_or(j == 0, j == tiles_ctx)
    seg_last = jnp.logical_or(j == tiles_ctx - 1, j == tiles_total - 1)

    ext = []
    for e in range(e0, e0 + ne):
        xe = jnp.concatenate([xp_ref[e], x_ref[e], xn_ref[e]], axis=0)
        ms = jnp.mean(xe * xe, axis=-1, keepdims=True)
        shift = jnp.where(is_ctx, modc_ref[0, 0:1, :], mod_ref[e, 0:1, :])
        scale = jnp.where(is_ctx, modc_ref[0, 1:2, :], mod_ref[e, 1:2, :])
        gain = g_ref[...] * (1.0 + scale)
        ext.append((xe * lax.rsqrt(ms + EPS) * gain + shift).astype(jnp.bfloat16))
    xn_ext = jnp.concatenate(ext, axis=0)
    xn = jnp.concatenate([x_[HALO:HALO + tm] for x_ in ext], axis=0)

    def proj(rows, w_ref, c0, c1):
        return jnp.dot(rows, w_ref[:, c0:c1], preferred_element_type=jnp.float32)

    row = lax.broadcasted_iota(jnp.int32, (te, 1), 0)
    keep = jnp.logical_and(jnp.logical_or(row >= HALO, jnp.logical_not(seg_first)),
                           jnp.logical_or(row < HALO + tm, jnp.logical_not(seg_last)))
    nb = tm // SUBLANES
    sub = lax.broadcasted_iota(jnp.int32, (1, SUBLANES, 1), 1)

    def conv_chunk(c0, width):
        y_all = proj(xn_ext, wm_ref, C_MQK + c0, C_MQK + c0 + width)
        for i, e in enumerate(range(e0, e0 + ne)):
            ye = jnp.where(keep, y_all[i * te:(i + 1) * te], 0.0).reshape(nb + 2, SUBLANES, width)
            acc = jnp.broadcast_to(cb_ref[:, c0:c0 + width].reshape(1, 1, width), (nb, SUBLANES, width))
            for t in range(CONV_K):
                delta = t - CONV_K // 2
                w_t = cw_ref[t:t + 1, c0:c0 + width].reshape(1, 1, width)
                if delta == 0:
                    tap = ye[1:nb + 1]
                else:
                    rot = pltpu.roll(ye, (-delta) % SUBLANES, axis=1)
                    if delta > 0:
                        tap = jnp.where(sub < SUBLANES - delta, rot[1:nb + 1], rot[2:nb + 2])
                    else:
                        tap = jnp.where(sub >= -delta, rot[1:nb + 1], rot[0:nb])
                acc = acc + w_t * tap
            acc = acc.reshape(tm, width)
            mqk_ref[e, :, c0:c0 + width] = (_silu(acc) * ks_ref[:, c0:c0 + width]).astype(jnp.bfloat16)

    def rope_group(grp):
        is_q = grp < ATT_KV_HEADS
        cos = cos_ref[...] * Q_PRESCALE if is_q else cos_ref[...]
        sin = sin_ref[...] * Q_PRESCALE if is_q else sin_ref[...]
        yq = proj(xn, wqk_ref, grp * KV_GROUP_COLS, (grp + 1) * KV_GROUP_COLS)
        dst = q_ref if is_q else kx_ref
        c0 = (grp % ATT_KV_HEADS) * KV_GROUP_COLS
        for i, e in enumerate(range(e0, e0 + ne)):
            first = yq[i * tm:(i + 1) * tm, :LANES]
            second = yq[i * tm:(i + 1) * tm, LANES:]
            dst[e, :, c0:c0 + LANES] = (first * cos - second * sin).astype(jnp.bfloat16)
            dst[e, :, c0 + LANES:c0 + 2 * LANES] = (second * cos + first * sin).astype(jnp.bfloat16)

    def plain(dst_ref, c0, width, off=0):
        y = proj(xn, wm_ref, c0, c0 + width)
        for i, e in enumerate(range(e0, e0 + ne)):
            dst_ref[e, :, off:off + width] = y[i * tm:(i + 1) * tm].astype(dst_ref.dtype)

    others = [functools.partial(rope_group, g) for g in range(2 * ATT_KV_HEADS)]
    others += [functools.partial(plain, az_ref, C_AZ, ATT_WIDTH),
               functools.partial(plain, mv_ref, C_MV, M_WIDTH),
               functools.partial(plain, moz_ref, C_MOZ, M_WIDTH),
               functools.partial(plain, moz_ref, C_MOZ + M_WIDTH, M_WIDTH, M_WIDTH)]
    per_chunk = 2
    cw = 2 * M_WIDTH * per_chunk // len(others)
    for i in range(len(others) // per_chunk):
        conv_chunk(i * cw, cw)
        for other in others[i * per_chunk:(i + 1) * per_chunk]:
            other()
    y = proj(xn, wvg_ref, 0, KV_WIDTH + LANES)
    for i, e in enumerate(range(e0, e0 + ne)):
        v_ref[e] = y[i * tm:(i + 1) * tm, :KV_WIDTH].astype(jnp.bfloat16)
        gate_ref[e] = y[i * tm:(i + 1) * tm, KV_WIDTH:] + gb_ref[...]


def _projection(h, mod_l, norm_g, w_qk, w_main, w_vg, cos, sin, conv_w, conv_b, kscale, gate_b, *, ctx_len):
    bsz, s_len, d = h.shape
    tm = ROW_TILE
    ne = math.gcd(bsz, PROJ_BATCH)
    tiles_total = s_len // tm
    tiles_ctx = ctx_len // tm
    hb = tm // HALO
    n_halo = s_len // HALO
    ctx_blk = mod_l.shape[0] - 1

    def row_map(b, j):
        return (b, j, 0)

    kernel = functools.partial(_proj_kernel, tm=tm, tiles_ctx=tiles_ctx, tiles_total=tiles_total)
    out_shapes = (
        jax.ShapeDtypeStruct((bsz, s_len, ATT_WIDTH), jnp.bfloat16),
        jax.ShapeDtypeStruct((bsz, s_len, ATT_WIDTH), jnp.bfloat16),
        jax.ShapeDtypeStruct((bsz, s_len, KV_WIDTH), jnp.bfloat16),
        jax.ShapeDtypeStruct((bsz, s_len, ATT_WIDTH), jnp.bfloat16),
        jax.ShapeDtypeStruct((bsz, s_len, 2 * M_WIDTH), jnp.bfloat16),
        jax.ShapeDtypeStruct((bsz, s_len, M_WIDTH), jnp.bfloat16),
        jax.ShapeDtypeStruct((bsz, s_len, 2 * M_WIDTH), jnp.bfloat16),
        jax.ShapeDtypeStruct((bsz, s_len, LANES), jnp.float32),
    )
    return pl.pallas_call(
        kernel,
        grid=(bsz // ne, tiles_total),
        in_specs=[
            pl.BlockSpec((ne, HALO, d), lambda b, j: (b, jnp.maximum(j * hb - 1, 0), 0)),
            pl.BlockSpec((ne, tm, d), row_map),
            pl.BlockSpec((ne, HALO, d), lambda b, j: (b, jnp.minimum((j + 1) * hb, n_halo - 1), 0)),
            pl.BlockSpec((ne, 3, d), lambda b, j: (b, 0, 0)),
            pl.BlockSpec((1, 3, d), lambda b, j: (ctx_blk, 0, 0)),
            pl.BlockSpec((1, d), lambda b, j: (0, 0)),
            pl.BlockSpec((d, 2 * ATT_WIDTH), lambda b, j: (0, 0), pipeline_mode=pl.Buffered(1)),
            pl.BlockSpec((d, N_MAIN), lambda b, j: (0, 0), pipeline_mode=pl.Buffered(1)),
            pl.BlockSpec((d, KV_WIDTH + LANES), lambda b, j: (0, 0), pipeline_mode=pl.Buffered(1)),
            pl.BlockSpec((tm, LANES), lambda b, j: (j, 0)),
            pl.BlockSpec((tm, LANES), lambda b, j: (j, 0)),
            pl.BlockSpec((CONV_K, 2 * M_WIDTH), lambda b, j: (0, 0)),
            pl.BlockSpec((1, 2 * M_WIDTH), lambda b, j: (0, 0)),
            pl.BlockSpec((1, 2 * M_WIDTH), lambda b, j: (0, 0)),
            pl.BlockSpec((1, LANES), lambda b, j: (0, 0)),
        ],
        out_specs=[
            pl.BlockSpec((ne, tm, ATT_WIDTH), row_map),
            pl.BlockSpec((ne, tm, ATT_WIDTH), row_map),
            pl.BlockSpec((ne, tm, KV_WIDTH), row_map),
            pl.BlockSpec((ne, tm, ATT_WIDTH), row_map),
            pl.BlockSpec((ne, tm, 2 * M_WIDTH), row_map),
            pl.BlockSpec((ne, tm, M_WIDTH), row_map),
            pl.BlockSpec((ne, tm, 2 * M_WIDTH), row_map),
            pl.BlockSpec((ne, tm, LANES), row_map),
        ],
        out_shape=out_shapes,
        compiler_params=pltpu.CompilerParams(
            dimension_semantics=("arbitrary", "arbitrary"), vmem_limit_bytes=VMEM_LIMIT),
        name="norm_mod_projection",
    )(h, h, h, mod_l, mod_l, norm_g, w_qk, w_main, w_vg, cos, sin, conv_w, conv_b, kscale, gate_b)


def _attn_kernel(sink_ref, q_ref, kp_ref, kc_ref, kn_ref, kctx_ref, vp_ref, vc_ref, vn_ref, vctx_ref,
                 az_ref, o_ref, *, blk0, ctx_blocks, total_blocks):
    qb = pl.program_id(1) + blk0
    is_ctx = qb < ctx_blocks
    n = qb - ctx_blocks
    lat_blocks = total_blocks - ctx_blocks
    ctx_len = ctx_blocks * BLOCK
    win = 3 * BLOCK
    neg = -jnp.inf

    i = lax.broadcasted_iota(jnp.int32, (BLOCK, win), 0)
    jx = lax.broadcasted_iota(jnp.int32, (BLOCK, win), 1)
    lo = jnp.where(n == 0, BLOCK, 0)
    hi = jnp.where(n == lat_blocks - 1, 2 * BLOCK, win)
    valid = (jx >= i) & (jx <= i + 2 * WINDOW) & (jx >= lo) & (jx < hi) & jnp.logical_not(is_ctx)
    bias = jnp.concatenate([jnp.where(valid, 0.0, neg)] * ATT_GROUP, axis=0)

    lane_v = lax.broadcasted_iota(jnp.int32, (1, LANES), 1)
    low_half = lane_v < ATT_HEAD_DIM
    lane_q = lax.broadcasted_iota(jnp.int32, (1, KV_GROUP_COLS), 1)
    head_of_qlane = (lane_q % LANES) // (LANES // ATT_GROUP)
    row_head = lax.broadcasted_iota(jnp.int32, (ATT_GROUP * BLOCK, 1), 0) // BLOCK
    n_keys = win + ctx_len
    ones = jnp.ones((n_keys, LANES), jnp.bfloat16)

    for e_i in range(q_ref.shape[0]):
        k_all = jnp.concatenate([kp_ref[e_i], kc_ref[e_i], kn_ref[e_i], kctx_ref[e_i]], axis=0)
        v_all = jnp.concatenate([vp_ref[e_i], vc_ref[e_i], vn_ref[e_i], vctx_ref[e_i]],
                                axis=0).astype(jnp.float32)
        v_rot = pltpu.roll(v_all, ATT_HEAD_DIM, axis=1)
        for kvh in range(ATT_KV_HEADS):
            c0 = kvh * KV_GROUP_COLS
            qg = q_ref[e_i, :, c0:c0 + KV_GROUP_COLS]
            qm = jnp.concatenate(
                [jnp.where(head_of_qlane == g, qg, jnp.zeros_like(qg)) for g in range(ATT_GROUP)], axis=0)
            kx = k_all[:, c0:c0 + KV_GROUP_COLS]
            s = lax.dot_general(qm, kx, (((1,), (1,)), ((), ())), preferred_element_type=jnp.float32)
            s_win = s[:, :win] + bias
            s_ctx = s[:, win:]
            sink = jnp.zeros((ATT_GROUP * BLOCK, 1), jnp.float32)
            for g in range(ATT_GROUP):
                sink = jnp.where(row_head == g, sink_ref[kvh * ATT_GROUP + g] * LOG2_E, sink)
            m = jnp.maximum(jnp.maximum(jnp.max(s_win, axis=-1, keepdims=True),
                                        jnp.max(s_ctx, axis=-1, keepdims=True)), sink)
            e = jnp.concatenate([jnp.exp2(s_win - m), jnp.exp2(s_ctx - m)], axis=1).astype(jnp.bfloat16)
            own_half = low_half if kvh == 0 else jnp.logical_not(low_half)
            v_two = jnp.where(own_half, v_all, v_rot).astype(jnp.bfloat16)
            res = jnp.dot(e, jnp.concatenate([v_two, ones], axis=1),
                          preferred_element_type=jnp.float32)
            o = res[:, :LANES] / (res[:, LANES:] + jnp.exp2(sink - m))
            az = az_ref[e_i, :, c0:c0 + KV_GROUP_COLS].astype(jnp.float32)
            for pair in range(ATT_GROUP // 2):
                both = jnp.where(low_half, o[2 * pair * BLOCK:(2 * pair + 1) * BLOCK],
                                 o[(2 * pair + 1) * BLOCK:(2 * pair + 2) * BLOCK])
                lo = c0 + pair * LANES
                o_ref[e_i, :, lo:lo + LANES] = (
                    both * _silu(az[:, pair * LANES:(pair + 1) * LANES])).astype(jnp.bfloat16)


def _attention(sink, q, kx, v, az, *, ctx_len, skip_ctx):
    bsz, s_len, _ = q.shape
    ctx_blocks = ctx_len // BLOCK
    total_blocks = s_len // BLOCK
    blk0 = ctx_blocks if skip_ctx else 0
    lo, hi = ctx_blocks, total_blocks - 1
    nb = math.gcd(bsz, BATCH_PER_STEP)

    def at(off):
        return lambda b, j: (b, jnp.clip(j + blk0 + off, lo, hi), 0)

    def cur(b, j):
        return (b, j + blk0, 0)

    def ctx(b, j):
        return (b, 0, 0)

    kernel = functools.partial(_attn_kernel, blk0=blk0, ctx_blocks=ctx_blocks, total_blocks=total_blocks)
    return pl.pallas_call(
        kernel,
        grid=(bsz // nb, total_blocks - blk0),
        in_specs=[
            pl.BlockSpec(memory_space=pltpu.SMEM),
            pl.BlockSpec((nb, BLOCK, ATT_WIDTH), cur),
            pl.BlockSpec((nb, BLOCK, ATT_WIDTH), at(-1)),
            pl.BlockSpec((nb, BLOCK, ATT_WIDTH), at(0)),
            pl.BlockSpec((nb, BLOCK, ATT_WIDTH), at(1)),
            pl.BlockSpec((nb, ctx_len, ATT_WIDTH), ctx),
            pl.BlockSpec((nb, BLOCK, KV_WIDTH), at(-1)),
            pl.BlockSpec((nb, BLOCK, KV_WIDTH), at(0)),
            pl.BlockSpec((nb, BLOCK, KV_WIDTH), at(1)),
            pl.BlockSpec((nb, ctx_len, KV_WIDTH), ctx),
            pl.BlockSpec((nb, BLOCK, ATT_WIDTH), cur),
        ],
        out_specs=pl.BlockSpec((nb, BLOCK, ATT_WIDTH), lambda b, j: (b, j, 0)),
        out_shape=jax.ShapeDtypeStruct((bsz, s_len - blk0 * BLOCK, ATT_WIDTH), jnp.bfloat16),
        compiler_params=pltpu.CompilerParams(
            dimension_semantics=("arbitrary", "arbitrary"), vmem_limit_bytes=VMEM_LIMIT),
        name="window_ctx_attention",
    )(sink, q, kx, kx, kx, kx, v, v, v, v, az)


STATE_ROWS = M_HEAD_DIM + SUBLANES


def _split2(x):
    hi = x.astype(jnp.bfloat16)
    lo = (x - hi.astype(jnp.float32)).astype(jnp.bfloat16)
    return hi, lo


def _mlstm_kernel(qkf_ref, vf_ref, gf_ref, qkb_ref, vb_ref, gb_ref, hf_ref, hb_ref, c_ref, m_ref):
    @pl.when(pl.program_id(1) == 0)
    def _():
        c_ref[...] = jnp.zeros_like(c_ref)
        m_ref[...] = jnp.full_like(m_ref, -jnp.inf)

    rows = lax.broadcasted_iota(jnp.int32, (CHUNK, CHUNK), 0)
    cols = lax.broadcasted_iota(jnp.int32, (CHUNK, CHUNK), 1)
    le = rows <= cols
    ge = rows >= cols
    tri_le = jnp.where(le, 1.0, 0.0).astype(jnp.bfloat16)
    tri_ge = jnp.where(ge, 1.0, 0.0).astype(jnp.bfloat16)

    for e in range(qkf_ref.shape[0]):
        _mlstm_element(e, qkf_ref, vf_ref, gf_ref, qkb_ref, vb_ref, gb_ref, hf_ref, hb_ref, c_ref, m_ref,
                       le, ge, tri_le, tri_ge)


def _mlstm_element(e, qkf_ref, vf_ref, gf_ref, qkb_ref, vb_ref, gb_ref, hf_ref, hb_ref, c_ref, m_ref,
                   le, ge, tri_le, tri_ge):
    nst = N_DIR * M_HEADS
    gates = (gf_ref[e], gb_ref[e])
    gates_t = (gates[0].T, gates[1].T)

    i8 = jnp.concatenate([gates_t[0][0:M_HEADS], gates_t[1][M_HEADS:nst]], axis=0)
    lf8 = _log_sigmoid(jnp.concatenate([gates_t[0][nst:nst + M_HEADS],
                                        gates_t[1][nst + M_HEADS:2 * nst]], axis=0))
    pre8 = sum(jnp.dot(p, tri_le, preferred_element_type=jnp.float32) for p in _split2(lf8))
    tot8 = pre8[:, CHUNK - 1:CHUNK]
    is_fwd = lax.broadcasted_iota(jnp.int32, (nst, 1), 0) < M_HEADS
    b8 = jnp.where(is_fwd, pre8, tot8 - pre8 + lf8)
    r8 = i8 - b8
    g8 = tot8 + r8
    m_prev = m_ref[e]
    m_new = jnp.maximum(tot8 + m_prev, jnp.max(g8, axis=1, keepdims=True))
    a8 = jnp.exp(tot8 + m_prev - m_new)
    w8 = jnp.exp(g8 - m_new)
    inter8 = b8 + m_prev
    m_ref[e] = m_new

    ones_row = jnp.where(lax.broadcasted_iota(jnp.int32, (SUBLANES, CHUNK), 0) == 0, 1.0, 0.0)

    lf_c2 = _log_sigmoid(jnp.concatenate(gates, axis=1))
    pre_c2 = sum(jnp.dot(tri_ge, p, preferred_element_type=jnp.float32) for p in _split2(lf_c2))

    for d, (qk_ref, v_ref, h_ref) in enumerate(((qkf_ref, vf_ref, hf_ref), (qkb_ref, vb_ref, hb_ref))):
        valid_t = le if d == 0 else ge
        lf_c = lf_c2[:, d * LANES:(d + 1) * LANES]
        pre_c = pre_c2[:, d * LANES:(d + 1) * LANES]
        b_c = pre_c if d == 0 else pre_c[CHUNK - 1:CHUNK, :] - pre_c + lf_c
        r_c = pltpu.roll(gates[d], nst, axis=1) - b_c

        heads = range(M_HEADS)
        rs = [d * M_HEADS + hd for hd in heads]
        qs = [qk_ref[e, :, hd * M_HEAD_DIM:(hd + 1) * M_HEAD_DIM] for hd in heads]
        ks = [qk_ref[e, :, M_WIDTH + hd * M_HEAD_DIM:M_WIDTH + (hd + 1) * M_HEAD_DIM] for hd in heads]
        vts = [jnp.concatenate([v_ref[e, :, hd * M_HEAD_DIM:(hd + 1) * M_HEAD_DIM].astype(jnp.float32).T,
                                ones_row], axis=0) for hd in heads]
        cs = [c_ref[e, r] for r in rs]

        qts = [q.astype(jnp.float32).T for q in qs]
        st = [jnp.dot(k, qt.astype(jnp.bfloat16), preferred_element_type=jnp.float32)
              for k, qt in zip(ks, qts)]
        dt = [jnp.where(valid_t, r_c[:, nst + r:nst + r + 1] + b8[r:r + 1, :], -jnp.inf) for r in rs]
        m_t = [jnp.maximum(inter8[r:r + 1, :], jnp.max(x, axis=0, keepdims=True)) for r, x in zip(rs, dt)]
        w_i = [jnp.exp(inter8[r:r + 1, :] - m) for r, m in zip(rs, m_t)]
        pt = [s_ * jnp.exp(x - m) for s_, x, m in zip(st, dt, m_t)]
        both = [jnp.dot(jnp.concatenate([vt, c], axis=1).astype(jnp.bfloat16),
                        jnp.concatenate([p, qt * w], axis=0).astype(jnp.bfloat16),
                        preferred_element_type=jnp.float32)
                for vt, c, p, qt, w in zip(vts, cs, pt, qts, w_i)]
        for hd in heads:
            den = both[hd][M_HEAD_DIM:M_HEAD_DIM + 1]
            ht = both[hd][:M_HEAD_DIM] / jnp.maximum(jnp.abs(den), jnp.exp(-m_t[hd]))
            h_ref[e, :, hd * M_HEAD_DIM:(hd + 1) * M_HEAD_DIM] = ht.T.astype(h_ref.dtype)
        for hd, r in zip(heads, rs):
            vw = (vts[hd] * w8[r:r + 1, :]).astype(jnp.bfloat16)
            c_ref[e, r] = a8[r:r + 1, :] * cs[hd] + jnp.dot(vw, ks[hd], preferred_element_type=jnp.float32)


def _mlstm(mqk, mv, gates, *, ctx_len):
    bsz, s_len, _ = mqk.shape
    nc = s_len // CHUNK
    ncc = ctx_len // CHUNK

    def fwd(b, j):
        return (b, j, 0)

    def bwd(b, j):
        return (b, jnp.where(j < ncc, ncc - 1 - j, nc - 1 + ncc - j), 0)

    nstate = N_DIR * M_HEADS
    nb = math.gcd(bsz, BATCH_PER_STEP)
    return pl.pallas_call(
        _mlstm_kernel,
        grid=(bsz // nb, nc),
        in_specs=[
            pl.BlockSpec((nb, CHUNK, 2 * M_WIDTH), fwd),
            pl.BlockSpec((nb, CHUNK, M_WIDTH), fwd),
            pl.BlockSpec((nb, CHUNK, LANES), fwd),
            pl.BlockSpec((nb, CHUNK, 2 * M_WIDTH), bwd),
            pl.BlockSpec((nb, CHUNK, M_WIDTH), bwd),
            pl.BlockSpec((nb, CHUNK, LANES), bwd),
        ],
        out_specs=[
            pl.BlockSpec((nb, CHUNK, M_WIDTH), fwd),
            pl.BlockSpec((nb, CHUNK, M_WIDTH), bwd),
        ],
        out_shape=(jax.ShapeDtypeStruct((bsz, s_len, M_WIDTH), jnp.bfloat16),
                   jax.ShapeDtypeStruct((bsz, s_len, M_WIDTH), jnp.bfloat16)),
        scratch_shapes=[
            pltpu.VMEM((nb, nstate, STATE_ROWS, M_HEAD_DIM), jnp.float32),
            pltpu.VMEM((nb, nstate, LANES), jnp.float32),
        ],
        compiler_params=pltpu.CompilerParams(
            dimension_semantics=("arbitrary", "arbitrary"), vmem_limit_bytes=VMEM_LIMIT),
        name="mlstm_bidirectional_scan",
    )(mqk, mv, gates, mqk, mv, gates)


def _out_kernel(x_ref, att_ref, hf_ref, hb_ref, moz_ref, hg_ref, w_ref, mod_ref, modc_ref, *rest,
                final, tiles_ctx, t0):
    if final:
        fg_ref, o_ref = rest
    else:
        (o_ref,) = rest
    ne, tm, _ = x_ref.shape
    is_ctx = pl.program_id(1) + t0 < tiles_ctx
    mxs = []
    for e in range(ne):
        mo = moz_ref[e, :, :M_WIDTH].astype(jnp.float32)
        mz = moz_ref[e, :, M_WIDTH:].astype(jnp.float32)
        hs = _sigmoid(mo) * (hf_ref[e].astype(jnp.float32) + hb_ref[e].astype(jnp.float32))
        parts = []
        for hd in range(M_HEADS):
            hh = hs[:, hd * M_HEAD_DIM:(hd + 1) * M_HEAD_DIM]
            ms = jnp.mean(hh * hh, axis=-1, keepdims=True)
            parts.append(hh * lax.rsqrt(ms + EPS))
        mxs.append((jnp.concatenate(parts, axis=1) * hg_ref[...] * _silu(mz)).astype(jnp.bfloat16))
    att = jnp.concatenate([att_ref[e] for e in range(ne)], axis=0)
    out = (jnp.dot(att, w_ref[:ATT_WIDTH, :], preferred_element_type=jnp.float32)
           + jnp.dot(jnp.concatenate(mxs, axis=0), w_ref[ATT_WIDTH:, :], preferred_element_type=jnp.float32))
    for e in range(ne):
        gate = jnp.where(is_ctx, modc_ref[0, 2:3, :], mod_ref[e, 2:3, :])
        xnew = x_ref[e] + gate * out[e * tm:(e + 1) * tm]
        if final:
            ms = jnp.mean(xnew * xnew, axis=-1, keepdims=True)
            o_ref[e] = xnew * lax.rsqrt(ms + EPS) * fg_ref[...]
        else:
            o_ref[e] = xnew


def _output(h, att, hf, hb, moz, head_g, w_out, mod_l, final_g, *, ctx_len, final):
    bsz, s_len, d = h.shape
    tm = ROW_TILE
    ne = math.gcd(bsz, PROJ_BATCH)
    tiles_total = s_len // tm
    tiles_ctx = ctx_len // tm
    t0 = tiles_ctx if final else 0
    ctx_blk = mod_l.shape[0] - 1

    def row_map(b, j):
        return (b, j + t0, 0)

    in_specs = [
        pl.BlockSpec((ne, tm, d), row_map),
        pl.BlockSpec((ne, tm, ATT_WIDTH), lambda b, j: (b, j, 0)),
        pl.BlockSpec((ne, tm, M_WIDTH), row_map),
        pl.BlockSpec((ne, tm, M_WIDTH), row_map),
        pl.BlockSpec((ne, tm, 2 * M_WIDTH), row_map),
        pl.BlockSpec((1, M_WIDTH), lambda b, j: (0, 0)),
        pl.BlockSpec((d, d), lambda b, j: (0, 0), pipeline_mode=pl.Buffered(1)),
        pl.BlockSpec((ne, 3, d), lambda b, j: (b, 0, 0)),
        pl.BlockSpec((1, 3, d), lambda b, j: (ctx_blk, 0, 0)),
    ]
    args = [h, att, hf, hb, moz, head_g, w_out, mod_l, mod_l]
    if final:
        in_specs.append(pl.BlockSpec((1, d), lambda b, j: (0, 0)))
        args.append(final_g)
        out_rows = s_len - ctx_len
    else:
        out_rows = s_len
    return pl.pallas_call(
        functools.partial(_out_kernel, final=final, tiles_ctx=tiles_ctx, t0=t0),
        grid=(bsz // ne, tiles_total - t0),
        in_specs=in_specs,
        out_specs=pl.BlockSpec((ne, tm, d), lambda b, j: (b, j, 0)),
        out_shape=jax.ShapeDtypeStruct((bsz, out_rows, d), jnp.float32),
        compiler_params=pltpu.CompilerParams(
            dimension_semantics=("arbitrary", "arbitrary"), vmem_limit_bytes=VMEM_LIMIT),
        name="gate_norm_out_projection",
    )(*args)


def kernel(x, c, ctx, c_ctx, w_ada, b_ada, norm_g, w_in, conv_w, conv_b, gate_b, sink, head_g, w_out, final_g):
    bsz, t_len, d = x.shape
    ctx_len = ctx.shape[1]
    depth = w_in.shape[0]
    assert d == D_MODEL and t_len % ROW_TILE == 0 and ctx_len % ROW_TILE == 0 and t_len % GRID_W == 0

    h = jnp.concatenate([ctx, x], axis=1)
    mod_rows = -(-(bsz + 1) // SUBLANES) * SUBLANES
    cc = jnp.zeros((mod_rows, d), jnp.float32).at[:bsz].set(c).at[mod_rows - 1].set(c_ctx)
    mod = _modulation(cc, w_ada, b_ada).reshape(depth, mod_rows, 3, d)

    w_qk, w_main, w_vg = _split_w_in(w_in)
    w_o = w_out.astype(jnp.bfloat16)
    cos, sin = _rope_tables(t_len, ctx_len)
    kscale = jnp.concatenate([jnp.ones((1, M_WIDTH), jnp.float32),
                              jnp.full((1, M_WIDTH), M_HEAD_DIM ** -0.5, jnp.float32)], axis=1)
    gate_bias = jnp.pad(gate_b, ((0, 0), (0, LANES - N_GATES)))

    for l in range(depth):
        last = l == depth - 1
        q, kx, v, az, mqk, mv, moz, gates = _projection(
            h, mod[l], norm_g[l][None], w_qk[l], w_main[l], w_vg[l], cos, sin, conv_w[l], conv_b[l][None], kscale,
            gate_bias[l][None], ctx_len=ctx_len)
        att = _attention(sink[l], q, kx, v, az, ctx_len=ctx_len, skip_ctx=last)
        hf, hb = _mlstm(mqk, mv, gates, ctx_len=ctx_len)
        h = _output(h, att, hf, hb, moz, head_g[l][None], w_o[l], mod[l], final_g[None],
                    ctx_len=ctx_len, final=last)
    return h
```

```python
import functools
import math

import jax
import jax.numpy as jnp
from jax import lax
from jax.experimental import pallas as pl
from jax.experimental.pallas import tpu as pltpu

D_MODEL = 1024
GRID_W = 64
ATT_HEADS = 8
ATT_KV_HEADS = 2
ATT_GROUP = ATT_HEADS // ATT_KV_HEADS
ATT_HEAD_DIM = 64
ATT_WIDTH = ATT_HEADS * ATT_HEAD_DIM
KV_WIDTH = ATT_KV_HEADS * ATT_HEAD_DIM
WINDOW = 128
BLOCK = 128
ROPE_BASE = 10000.0
M_HEADS = 4
M_HEAD_DIM = 128
M_WIDTH = M_HEADS * M_HEAD_DIM
CONV_K = 5
CHUNK = 128
N_DIR = 2
N_GATES = 2 * N_DIR * M_HEADS
EPS = 1e-6

LANES = 128
SUBLANES = 8
HALO = SUBLANES
ROW_TILE = 256
BATCH_PER_STEP = 8
PROJ_BATCH = 4
PROJ_STACK = 1
VMEM_LIMIT = 56 * 1024 * 1024

W_IN_MAIN_START = ATT_WIDTH + 2 * KV_WIDTH
C_AZ = 0
C_MQK = C_AZ + ATT_WIDTH
C_MV = C_MQK + 2 * M_WIDTH
C_MOZ = C_MV + M_WIDTH
N_MAIN = C_MOZ + 2 * M_WIDTH
KV_GROUP_COLS = ATT_GROUP * ATT_HEAD_DIM
LOG2_E = 1.4426950408889634
Q_PRESCALE = ATT_HEAD_DIM ** -0.5 * LOG2_E


def _sigmoid(x):
    return 1.0 / (1.0 + jnp.exp(-x))


def _silu(x):
    return x * _sigmoid(x)


def _log_sigmoid(x):
    return jnp.minimum(x, 0.0) - jnp.log(1.0 + jnp.exp(-jnp.abs(x)))


def _split_w_in(w_in):
    depth, d, n_in = w_in.shape
    quarter = ATT_HEAD_DIM // 4
    k0 = ATT_WIDTH
    v0 = k0 + KV_WIDTH
    wq = w_in[:, :, :k0].reshape(depth, d, ATT_KV_HEADS, ATT_GROUP, 2, 2, quarter)
    wq = wq.transpose(0, 1, 2, 5, 3, 4, 6).reshape(depth, d, ATT_WIDTH)
    wk = w_in[:, :, k0:v0].reshape(depth, d, ATT_KV_HEADS, 1, 2, 2, quarter)
    wk = jnp.broadcast_to(wk.transpose(0, 1, 2, 5, 3, 4, 6),
                          (depth, d, ATT_KV_HEADS, 2, ATT_GROUP, 2, quarter)).reshape(depth, d, ATT_WIDTH)
    n_main_end = W_IN_MAIN_START + N_MAIN
    w_qk = jnp.concatenate([wq, wk], axis=2).astype(jnp.bfloat16)
    w_main = w_in[:, :, W_IN_MAIN_START:n_main_end].astype(jnp.bfloat16)
    pad = jnp.zeros((depth, d, LANES - N_GATES), w_in.dtype)
    w_vg = jnp.concatenate([w_in[:, :, v0:W_IN_MAIN_START], w_in[:, :, n_main_end:], pad],
                           axis=2).astype(jnp.bfloat16)
    return w_qk, w_main, w_vg


def _rope_tables(t_len, ctx_len):
    rows = t_len // GRID_W
    row = jnp.repeat(jnp.arange(rows), GRID_W).astype(jnp.float32)
    col = jnp.tile(jnp.arange(GRID_W), rows).astype(jnp.float32)
    half = ATT_HEAD_DIM // 2
    inv = ROPE_BASE ** (-jnp.arange(0, half, 2, dtype=jnp.float32) / half)
    ang = jnp.concatenate([row[:, None] * inv, col[:, None] * inv], axis=-1)
    ang = jnp.tile(ang, (1, LANES // half))
    cos = jnp.concatenate([jnp.ones((ctx_len, LANES), jnp.float32), jnp.cos(ang)], axis=0)
    sin = jnp.concatenate([jnp.zeros((ctx_len, LANES), jnp.float32), jnp.sin(ang)], axis=0)
    return cos, sin


def _mod_kernel(c_ref, w_ref, b_ref, o_ref):
    sc = _silu(c_ref[...])
    o_ref[0] = jnp.dot(sc, w_ref[0], preferred_element_type=jnp.float32,
                       precision=lax.Precision.HIGHEST) + b_ref[0]


def _modulation(cc, w_ada, b_ada):
    depth, d, d3 = w_ada.shape
    rows = cc.shape[0]
    nt = d3 // d
    return pl.pallas_call(
        _mod_kernel,
        grid=(depth, nt),
        in_specs=[
            pl.BlockSpec((rows, d), lambda l, n: (0, 0)),
            pl.BlockSpec((1, d, d), lambda l, n: (l, 0, n)),
            pl.BlockSpec((1, 1, d), lambda l, n: (l, 0, n)),
        ],
        out_specs=pl.BlockSpec((1, rows, d), lambda l, n: (l, 0, n)),
        out_shape=jax.ShapeDtypeStruct((depth, rows, d3), jnp.float32),
        compiler_params=pltpu.CompilerParams(dimension_semantics=("arbitrary", "arbitrary")),
        name="adaln_modulation",
    )(cc, w_ada, b_ada.reshape(depth, 1, d3))


def _proj_kernel(*refs, tm, tiles_ctx, tiles_total):
    n_elem = refs[4].shape[0]
    for e0 in range(0, n_elem, PROJ_STACK):
        _proj_group(e0, min(PROJ_STACK, n_elem - e0), *refs, tm=tm, tiles_ctx=tiles_ctx, tiles_total=tiles_total)


def _proj_group(e0, ne, cp_ref, c_ref, cn_ref, xp_ref, x_ref, xn_ref, mod_ref, modc_ref, g_ref,
                wqk_ref, wm_ref, wvg_ref,
                cos_ref, sin_ref, cw_ref, cb_ref,
                ks_ref, gb_ref,
                q_ref, kx_ref, v_ref, az_ref, mqk_ref, mv_ref, moz_ref, gate_ref,
                *, tm, tiles_ctx, tiles_total):
    te = tm + 2 * HALO
    j = pl.program_id(1)
    is_ctx = j < tiles_ctx
    seg_first = jnp.logical_or(j == 0, j == tiles_ctx)
    seg_last = jnp.logical_or(j == tiles_ctx - 1, j == tiles_total - 1)

    ext = []
    for e in range(e0, e0 + ne):
        xe = jnp.concatenate([jnp.where(is_ctx, cp_ref[e], xp_ref[e]), jnp.where(is_ctx, c_ref[e], x_ref[e]),
                              jnp.where(is_ctx, cn_ref[e], xn_ref[e])], axis=0)
        ms = jnp.mean(xe * xe, axis=-1, keepdims=True)
        shift = jnp.where(is_ctx, modc_ref[0, 0:1, :], mod_ref[e, 0:1, :])
        scale = jnp.where(is_ctx, modc_ref[0, 1:2, :], mod_ref[e, 1:2, :])
        gain = g_ref[...] * (1.0 + scale)
        ext.append((xe * lax.rsqrt(ms + EPS) * gain + shift).astype(jnp.bfloat16))
    xn_ext = jnp.concatenate(ext, axis=0)
    xn = jnp.concatenate([x_[HALO:HALO + tm] for x_ in ext], axis=0)

    def proj(rows, w_ref, c0, c1):
        return jnp.dot(rows, w_ref[:, c0:c1], preferred_element_type=jnp.float32)

    row = lax.broadcasted_iota(jnp.int32, (te, 1), 0)
    keep = jnp.logical_and(jnp.logical_or(row >= HALO, jnp.logical_not(seg_first)),
                           jnp.logical_or(row < HALO + tm, jnp.logical_not(seg_last)))
    nb = tm // SUBLANES
    sub = lax.broadcasted_iota(jnp.int32, (1, SUBLANES, 1), 1)

    def conv_chunk(c0, width):
        y_all = proj(xn_ext, wm_ref, C_MQK + c0, C_MQK + c0 + width)
        for i, e in enumerate(range(e0, e0 + ne)):
            ye = jnp.where(keep, y_all[i * te:(i + 1) * te], 0.0).reshape(nb + 2, SUBLANES, width)
            acc = jnp.broadcast_to(cb_ref[:, c0:c0 + width].reshape(1, 1, width), (nb, SUBLANES, width))
            for t in range(CONV_K):
                delta = t - CONV_K // 2
                w_t = cw_ref[t:t + 1, c0:c0 + width].reshape(1, 1, width)
                if delta == 0:
                    tap = ye[1:nb + 1]
                else:
                    rot = pltpu.roll(ye, (-delta) % SUBLANES, axis=1)
                    if delta > 0:
                        tap = jnp.where(sub < SUBLANES - delta, rot[1:nb + 1], rot[2:nb + 2])
                    else:
                        tap = jnp.where(sub >= -delta, rot[1:nb + 1], rot[0:nb])
                acc = acc + w_t * tap
            acc = acc.reshape(tm, width)
            mqk_ref[e, :, c0:c0 + width] = (_silu(acc) * ks_ref[:, c0:c0 + width]).astype(jnp.bfloat16)

    def rope_group(grp):
        is_q = grp < ATT_KV_HEADS
        cos = cos_ref[...] * Q_PRESCALE if is_q else cos_ref[...]
        sin = sin_ref[...] * Q_PRESCALE if is_q else sin_ref[...]
        yq = proj(xn, wqk_ref, grp * KV_GROUP_COLS, (grp + 1) * KV_GROUP_COLS)
        dst = q_ref if is_q else kx_ref
        c0 = (grp % ATT_KV_HEADS) * KV_GROUP_COLS
        for i, e in enumerate(range(e0, e0 + ne)):
            first = yq[i * tm:(i + 1) * tm, :LANES]
            second = yq[i * tm:(i + 1) * tm, LANES:]
            dst[e, :, c0:c0 + LANES] = (first * cos - second * sin).astype(jnp.bfloat16)
            dst[e, :, c0 + LANES:c0 + 2 * LANES] = (second * cos + first * sin).astype(jnp.bfloat16)

    def plain(dst_ref, c0, width, off=0):
        y = proj(xn, wm_ref, c0, c0 + width)
        for i, e in enumerate(range(e0, e0 + ne)):
            dst_ref[e, :, off:off + width] = y[i * tm:(i + 1) * tm].astype(dst_ref.dtype)

    others = [functools.partial(rope_group, g) for g in range(2 * ATT_KV_HEADS)]
    others += [functools.partial(plain, az_ref, C_AZ, ATT_WIDTH),
               functools.partial(plain, mv_ref, C_MV, M_WIDTH),
               functools.partial(plain, moz_ref, C_MOZ, M_WIDTH),
               functools.partial(plain, moz_ref, C_MOZ + M_WIDTH, M_WIDTH, M_WIDTH)]
    per_chunk = 2
    cw = 2 * M_WIDTH * per_chunk // len(others)
    for i in range(len(others) // per_chunk):
        conv_chunk(i * cw, cw)
        for other in others[i * per_chunk:(i + 1) * per_chunk]:
            other()
    y = proj(xn, wvg_ref, 0, KV_WIDTH + LANES)
    for i, e in enumerate(range(e0, e0 + ne)):
        v_ref[e] = y[i * tm:(i + 1) * tm, :KV_WIDTH].astype(jnp.bfloat16)
        gate_ref[e] = y[i * tm:(i + 1) * tm, KV_WIDTH:] + gb_ref[...]


def _tile_with_halo_specs(ne, tm, d, n_rows, tile_of):
    hb = tm // HALO
    last = n_rows // HALO - 1
    return (pl.BlockSpec((ne, HALO, d), lambda b, j: (b, jnp.maximum(tile_of(j) * hb - 1, 0), 0)),
            pl.BlockSpec((ne, tm, d), lambda b, j: (b, tile_of(j), 0)),
            pl.BlockSpec((ne, HALO, d), lambda b, j: (b, jnp.minimum((tile_of(j) + 1) * hb, last), 0)))


def _projection(hc, hx, mod_l, norm_g, w_qk, w_main, w_vg, cos, sin, conv_w, conv_b, kscale, gate_b):
    bsz, ctx_len, d = hc.shape
    s_len = ctx_len + hx.shape[1]
    tm = ROW_TILE
    ne = math.gcd(bsz, PROJ_BATCH)
    tiles_total = s_len // tm
    tiles_ctx = ctx_len // tm
    ctx_blk = mod_l.shape[0] - 1

    def row_map(b, j):
        return (b, j, 0)

    kernel = functools.partial(_proj_kernel, tm=tm, tiles_ctx=tiles_ctx, tiles_total=tiles_total)
    out_shapes = (
        jax.ShapeDtypeStruct((bsz, s_len, ATT_WIDTH), jnp.bfloat16),
        jax.ShapeDtypeStruct((bsz, s_len, ATT_WIDTH), jnp.bfloat16),
        jax.ShapeDtypeStruct((bsz, s_len, KV_WIDTH), jnp.bfloat16),
        jax.ShapeDtypeStruct((bsz, s_len, ATT_WIDTH), jnp.bfloat16),
        jax.ShapeDtypeStruct((bsz, s_len, 2 * M_WIDTH), jnp.bfloat16),
        jax.ShapeDtypeStruct((bsz, s_len, M_WIDTH), jnp.bfloat16),
        jax.ShapeDtypeStruct((bsz, s_len, 2 * M_WIDTH), jnp.bfloat16),
        jax.ShapeDtypeStruct((bsz, s_len, LANES), jnp.float32),
    )
    return pl.pallas_call(
        kernel,
        grid=(bsz // ne, tiles_total),
        in_specs=[
            *_tile_with_halo_specs(ne, tm, d, ctx_len, lambda j: jnp.minimum(j, tiles_ctx - 1)),
            *_tile_with_halo_specs(ne, tm, d, s_len - ctx_len, lambda j: jnp.maximum(j - tiles_ctx, 0)),
            pl.BlockSpec((ne, 3, d), lambda b, j: (b, 0, 0)),
            pl.BlockSpec((1, 3, d), lambda b, j: (ctx_blk, 0, 0)),
            pl.BlockSpec((1, d), lambda b, j: (0, 0)),
            pl.BlockSpec((d, 2 * ATT_WIDTH), lambda b, j: (0, 0), pipeline_mode=pl.Buffered(1)),
            pl.BlockSpec((d, N_MAIN), lambda b, j: (0, 0), pipeline_mode=pl.Buffered(1)),
            pl.BlockSpec((d, KV_WIDTH + LANES), lambda b, j: (0, 0), pipeline_mode=pl.Buffered(1)),
            pl.BlockSpec((tm, LANES), lambda b, j: (j, 0)),
            pl.BlockSpec((tm, LANES), lambda b, j: (j, 0)),
            pl.BlockSpec((CONV_K, 2 * M_WIDTH), lambda b, j: (0, 0)),
            pl.BlockSpec((1, 2 * M_WIDTH), lambda b, j: (0, 0)),
            pl.BlockSpec((1, 2 * M_WIDTH), lambda b, j: (0, 0)),
            pl.BlockSpec((1, LANES), lambda b, j: (0, 0)),
        ],
        out_specs=[
            pl.BlockSpec((ne, tm, ATT_WIDTH), row_map),
            pl.BlockSpec((ne, tm, ATT_WIDTH), row_map),
            pl.BlockSpec((ne, tm, KV_WIDTH), row_map),
            pl.BlockSpec((ne, tm, ATT_WIDTH), row_map),
            pl.BlockSpec((ne, tm, 2 * M_WIDTH), row_map),
            pl.BlockSpec((ne, tm, M_WIDTH), row_map),
            pl.BlockSpec((ne, tm, 2 * M_WIDTH), row_map),
            pl.BlockSpec((ne, tm, LANES), row_map),
        ],
        out_shape=out_shapes,
        compiler_params=pltpu.CompilerParams(
            dimension_semantics=("arbitrary", "arbitrary"), vmem_limit_bytes=VMEM_LIMIT),
        name="norm_mod_projection",
    )(hc, hc, hc, hx, hx, hx, mod_l, mod_l, norm_g, w_qk, w_main, w_vg, cos, sin, conv_w, conv_b, kscale, gate_b)


def _attn_kernel(sink_ref, q_ref, kp_ref, kc_ref, kn_ref, kctx_ref, vp_ref, vc_ref, vn_ref, vctx_ref,
                 az_ref, o_ref, *, blk0, ctx_blocks, total_blocks):
    qb = pl.program_id(1) + blk0
    is_ctx = qb < ctx_blocks
    n = qb - ctx_blocks
    lat_blocks = total_blocks - ctx_blocks
    ctx_len = ctx_blocks * BLOCK
    win = 3 * BLOCK
    neg = -jnp.inf

    i = lax.broadcasted_iota(jnp.int32, (BLOCK, win), 0)
    jx = lax.broadcasted_iota(jnp.int32, (BLOCK, win), 1)
    lo = jnp.where(n == 0, BLOCK, 0)
    hi = jnp.where(n == lat_blocks - 1, 2 * BLOCK, win)
    valid = (jx >= i) & (jx <= i + 2 * WINDOW) & (jx >= lo) & (jx < hi) & jnp.logical_not(is_ctx)
    bias = jnp.concatenate([jnp.where(valid, 0.0, neg)] * ATT_GROUP, axis=0)

    lane_v = lax.broadcasted_iota(jnp.int32, (1, LANES), 1)
    low_half = lane_v < ATT_HEAD_DIM
    lane_q = lax.broadcasted_iota(jnp.int32, (1, KV_GROUP_COLS), 1)
    head_of_qlane = (lane_q % LANES) // (LANES // ATT_GROUP)
    row_head = lax.broadcasted_iota(jnp.int32, (ATT_GROUP * BLOCK, 1), 0) // BLOCK
    n_keys = win + ctx_len
    ones = jnp.ones((n_keys, LANES), jnp.bfloat16)

    for e_i in range(q_ref.shape[0]):
        k_all = jnp.concatenate([kp_ref[e_i], kc_ref[e_i], kn_ref[e_i], kctx_ref[e_i]], axis=0)
        v_all = jnp.concatenate([vp_ref[e_i], vc_ref[e_i], vn_ref[e_i], vctx_ref[e_i]],
                                axis=0).astype(jnp.float32)
        v_rot = pltpu.roll(v_all, ATT_HEAD_DIM, axis=1)
        for kvh in range(ATT_KV_HEADS):
            c0 = kvh * KV_GROUP_COLS
            qg = q_ref[e_i, :, c0:c0 + KV_GROUP_COLS]
            qm = jnp.concatenate(
                [jnp.where(head_of_qlane == g, qg, jnp.zeros_like(qg)) for g in range(ATT_GROUP)], axis=0)
            kx = k_all[:, c0:c0 + KV_GROUP_COLS]
            s = lax.dot_general(qm, kx, (((1,), (1,)), ((), ())), preferred_element_type=jnp.float32)
            s_win = s[:, :win] + bias
            s_ctx = s[:, win:]
            sink = jnp.zeros((ATT_GROUP * BLOCK, 1), jnp.float32)
            for g in range(ATT_GROUP):
                sink = jnp.where(row_head == g, sink_ref[kvh * ATT_GROUP + g] * LOG2_E, sink)
            m = jnp.maximum(jnp.maximum(jnp.max(s_win, axis=-1, keepdims=True),
                                        jnp.max(s_ctx, axis=-1, keepdims=True)), sink)
            e = jnp.concatenate([jnp.exp2(s_win - m), jnp.exp2(s_ctx - m)], axis=1).astype(jnp.bfloat16)
            own_half = low_half if kvh == 0 else jnp.logical_not(low_half)
            v_two = jnp.where(own_half, v_all, v_rot).astype(jnp.bfloat16)
            res = jnp.dot(e, jnp.concatenate([v_two, ones], axis=1),
                          preferred_element_type=jnp.float32)
            o = res[:, :LANES] / (res[:, LANES:] + jnp.exp2(sink - m))
            az = az_ref[e_i, :, c0:c0 + KV_GROUP_COLS].astype(jnp.float32)
            for pair in range(ATT_GROUP // 2):
                both = jnp.where(low_half, o[2 * pair * BLOCK:(2 * pair + 1) * BLOCK],
                                 o[(2 * pair + 1) * BLOCK:(2 * pair + 2) * BLOCK])
                lo = c0 + pair * LANES
                o_ref[e_i, :, lo:lo + LANES] = (
                    both * _silu(az[:, pair * LANES:(pair + 1) * LANES])).astype(jnp.bfloat16)


def _attention(sink, q, kx, v, az, *, ctx_len, skip_ctx):
    bsz, s_len, _ = q.shape
    ctx_blocks = ctx_len // BLOCK
    total_blocks = s_len // BLOCK
    blk0 = ctx_blocks if skip_ctx else 0
    lo, hi = ctx_blocks, total_blocks - 1
    nb = math.gcd(bsz, BATCH_PER_STEP)

    def at(off):
        return lambda b, j: (b, jnp.clip(j + blk0 + off, lo, hi), 0)

    def cur(b, j):
        return (b, j + blk0, 0)

    def ctx(b, j):
        return (b, 0, 0)

    kernel = functools.partial(_attn_kernel, blk0=blk0, ctx_blocks=ctx_blocks, total_blocks=total_blocks)
    return pl.pallas_call(
        kernel,
        grid=(bsz // nb, total_blocks - blk0),
        in_specs=[
            pl.BlockSpec(memory_space=pltpu.SMEM),
            pl.BlockSpec((nb, BLOCK, ATT_WIDTH), cur),
            pl.BlockSpec((nb, BLOCK, ATT_WIDTH), at(-1)),
            pl.BlockSpec((nb, BLOCK, ATT_WIDTH), at(0)),
            pl.BlockSpec((nb, BLOCK, ATT_WIDTH), at(1)),
            pl.BlockSpec((nb, ctx_len, ATT_WIDTH), ctx),
            pl.BlockSpec((nb, BLOCK, KV_WIDTH), at(-1)),
            pl.BlockSpec((nb, BLOCK, KV_WIDTH), at(0)),
            pl.BlockSpec((nb, BLOCK, KV_WIDTH), at(1)),
            pl.BlockSpec((nb, ctx_len, KV_WIDTH), ctx),
            pl.BlockSpec((nb, BLOCK, ATT_WIDTH), cur),
        ],
        out_specs=pl.BlockSpec((nb, BLOCK, ATT_WIDTH), lambda b, j: (b, j, 0)),
        out_shape=jax.ShapeDtypeStruct((bsz, s_len - blk0 * BLOCK, ATT_WIDTH), jnp.bfloat16),
        compiler_params=pltpu.CompilerParams(
            dimension_semantics=("arbitrary", "arbitrary"), vmem_limit_bytes=VMEM_LIMIT),
        name="window_ctx_attention",
    )(sink, q, kx, kx, kx, kx, v, v, v, v, az)


STATE_ROWS = M_HEAD_DIM + SUBLANES


def _split2(x):
    hi = x.astype(jnp.bfloat16)
    lo = (x - hi.astype(jnp.float32)).astype(jnp.bfloat16)
    return hi, lo


def _mlstm_kernel(qkf_ref, vf_ref, gf_ref, qkb_ref, vb_ref, gb_ref, hf_ref, hb_ref, c_ref, m_ref):
    @pl.when(pl.program_id(1) == 0)
    def _():
        c_ref[...] = jnp.zeros_like(c_ref)
        m_ref[...] = jnp.full_like(m_ref, -jnp.inf)

    rows = lax.broadcasted_iota(jnp.int32, (CHUNK, CHUNK), 0)
    cols = lax.broadcasted_iota(jnp.int32, (CHUNK, CHUNK), 1)
    le = rows <= cols
    ge = rows >= cols
    tri_le = jnp.where(le, 1.0, 0.0).astype(jnp.bfloat16)
    tri_ge = jnp.where(ge, 1.0, 0.0).astype(jnp.bfloat16)

    for e in range(qkf_ref.shape[0]):
        _mlstm_element(e, qkf_ref, vf_ref, gf_ref, qkb_ref, vb_ref, gb_ref, hf_ref, hb_ref, c_ref, m_ref,
                       le, ge, tri_le, tri_ge)


def _mlstm_element(e, qkf_ref, vf_ref, gf_ref, qkb_ref, vb_ref, gb_ref, hf_ref, hb_ref, c_ref, m_ref,
                   le, ge, tri_le, tri_ge):
    nst = N_DIR * M_HEADS
    gates = (gf_ref[e], gb_ref[e])
    gates_t = (gates[0].T, gates[1].T)

    i8 = jnp.concatenate([gates_t[0][0:M_HEADS], gates_t[1][M_HEADS:nst]], axis=0)
    lf8 = _log_sigmoid(jnp.concatenate([gates_t[0][nst:nst + M_HEADS],
                                        gates_t[1][nst + M_HEADS:2 * nst]], axis=0))
    pre8 = sum(jnp.dot(p, tri_le, preferred_element_type=jnp.float32) for p in _split2(lf8))
    tot8 = pre8[:, CHUNK - 1:CHUNK]
    is_fwd = lax.broadcasted_iota(jnp.int32, (nst, 1), 0) < M_HEADS
    b8 = jnp.where(is_fwd, pre8, tot8 - pre8 + lf8)
    r8 = i8 - b8
    g8 = tot8 + r8
    m_prev = m_ref[e]
    m_new = jnp.maximum(tot8 + m_prev, jnp.max(g8, axis=1, keepdims=True))
    a8 = jnp.exp(tot8 + m_prev - m_new)
    w8 = jnp.exp(g8 - m_new)
    inter8 = b8 + m_prev
    m_ref[e] = m_new

    ones_row = jnp.where(lax.broadcasted_iota(jnp.int32, (SUBLANES, CHUNK), 0) == 0, 1.0, 0.0)

    lf_c2 = _log_sigmoid(jnp.concatenate(gates, axis=1))
    pre_c2 = sum(jnp.dot(tri_ge, p, preferred_element_type=jnp.float32) for p in _split2(lf_c2))

    for d, (qk_ref, v_ref, h_ref) in enumerate(((qkf_ref, vf_ref, hf_ref), (qkb_ref, vb_ref, hb_ref))):
        valid_t = le if d == 0 else ge
        lf_c = lf_c2[:, d * LANES:(d + 1) * LANES]
        pre_c = pre_c2[:, d * LANES:(d + 1) * LANES]
        b_c = pre_c if d == 0 else pre_c[CHUNK - 1:CHUNK, :] - pre_c + lf_c
        r_c = pltpu.roll(gates[d], nst, axis=1) - b_c

        heads = range(M_HEADS)
        rs = [d * M_HEADS + hd for hd in heads]
        qs = [qk_ref[e, :, hd * M_HEAD_DIM:(hd + 1) * M_HEAD_DIM] for hd in heads]
        ks = [qk_ref[e, :, M_WIDTH + hd * M_HEAD_DIM:M_WIDTH + (hd + 1) * M_HEAD_DIM] for hd in heads]
        vts = [jnp.concatenate([v_ref[e, :, hd * M_HEAD_DIM:(hd + 1) * M_HEAD_DIM].astype(jnp.float32).T,
                                ones_row], axis=0) for hd in heads]
        cs = [c_ref[e, r] for r in rs]

        qts = [q.astype(jnp.float32).T for q in qs]
        st = [jnp.dot(k, qt.astype(jnp.bfloat16), preferred_element_type=jnp.float32)
              for k, qt in zip(ks, qts)]
        dt = [jnp.where(valid_t, r_c[:, nst + r:nst + r + 1] + b8[r:r + 1, :], -jnp.inf) for r in rs]
        m_t = [jnp.maximum(inter8[r:r + 1, :], jnp.max(x, axis=0, keepdims=True)) for r, x in zip(rs, dt)]
        w_i = [jnp.exp(inter8[r:r + 1, :] - m) for r, m in zip(rs, m_t)]
        pt = [s_ * jnp.exp(x - m) for s_, x, m in zip(st, dt, m_t)]
        both = [jnp.dot(jnp.concatenate([vt, c], axis=1).astype(jnp.bfloat16),
                        jnp.concatenate([p, qt * w], axis=0).astype(jnp.bfloat16),
                        preferred_element_type=jnp.float32)
                for vt, c, p, qt, w in zip(vts, cs, pt, qts, w_i)]
        for hd in heads:
            den = both[hd][M_HEAD_DIM:M_HEAD_DIM + 1]
            ht = both[hd][:M_HEAD_DIM] / jnp.maximum(jnp.abs(den), jnp.exp(-m_t[hd]))
            h_ref[e, :, hd * M_HEAD_DIM:(hd + 1) * M_HEAD_DIM] = ht.T.astype(h_ref.dtype)
        for hd, r in zip(heads, rs):
            vw = (vts[hd] * w8[r:r + 1, :]).astype(jnp.bfloat16)
            c_ref[e, r] = a8[r:r + 1, :] * cs[hd] + jnp.dot(vw, ks[hd], preferred_element_type=jnp.float32)


def _mlstm(mqk, mv, gates, *, ctx_len):
    bsz, s_len, _ = mqk.shape
    nc = s_len // CHUNK
    ncc = ctx_len // CHUNK

    def fwd(b, j):
        return (b, j, 0)

    def bwd(b, j):
        return (b, jnp.where(j < ncc, ncc - 1 - j, nc - 1 + ncc - j), 0)

    nstate = N_DIR * M_HEADS
    nb = math.gcd(bsz, BATCH_PER_STEP)
    return pl.pallas_call(
        _mlstm_kernel,
        grid=(bsz // nb, nc),
        in_specs=[
            pl.BlockSpec((nb, CHUNK, 2 * M_WIDTH), fwd),
            pl.BlockSpec((nb, CHUNK, M_WIDTH), fwd),
            pl.BlockSpec((nb, CHUNK, LANES), fwd),
            pl.BlockSpec((nb, CHUNK, 2 * M_WIDTH), bwd),
            pl.BlockSpec((nb, CHUNK, M_WIDTH), bwd),
            pl.BlockSpec((nb, CHUNK, LANES), bwd),
        ],
        out_specs=[
            pl.BlockSpec((nb, CHUNK, M_WIDTH), fwd),
            pl.BlockSpec((nb, CHUNK, M_WIDTH), bwd),
        ],
        out_shape=(jax.ShapeDtypeStruct((bsz, s_len, M_WIDTH), jnp.bfloat16),
                   jax.ShapeDtypeStruct((bsz, s_len, M_WIDTH), jnp.bfloat16)),
        scratch_shapes=[
            pltpu.VMEM((nb, nstate, STATE_ROWS, M_HEAD_DIM), jnp.float32),
            pltpu.VMEM((nb, nstate, LANES), jnp.float32),
        ],
        compiler_params=pltpu.CompilerParams(
            dimension_semantics=("arbitrary", "arbitrary"), vmem_limit_bytes=VMEM_LIMIT),
        name="mlstm_bidirectional_scan",
    )(mqk, mv, gates, mqk, mv, gates)


def _out_kernel(x_ref, att_ref, hf_ref, hb_ref, moz_ref, hg_ref, w_ref, mod_ref, *rest, final):
    if final:
        fg_ref, o_ref = rest
    else:
        (o_ref,) = rest
    ne, tm, _ = x_ref.shape
    mxs = []
    for e in range(ne):
        mo = moz_ref[e, :, :M_WIDTH].astype(jnp.float32)
        mz = moz_ref[e, :, M_WIDTH:].astype(jnp.float32)
        hs = _sigmoid(mo) * (hf_ref[e].astype(jnp.float32) + hb_ref[e].astype(jnp.float32))
        parts = []
        for hd in range(M_HEADS):
            hh = hs[:, hd * M_HEAD_DIM:(hd + 1) * M_HEAD_DIM]
            ms = jnp.mean(hh * hh, axis=-1, keepdims=True)
            parts.append(hh * lax.rsqrt(ms + EPS))
        mxs.append((jnp.concatenate(parts, axis=1) * hg_ref[...] * _silu(mz)).astype(jnp.bfloat16))
    att = jnp.concatenate([att_ref[e] for e in range(ne)], axis=0)
    out = (jnp.dot(att, w_ref[:ATT_WIDTH, :], preferred_element_type=jnp.float32)
           + jnp.dot(jnp.concatenate(mxs, axis=0), w_ref[ATT_WIDTH:, :], preferred_element_type=jnp.float32))
    for e in range(ne):
        gate = mod_ref[e if mod_ref.shape[0] > 1 else 0, 2:3, :]
        xnew = x_ref[e] + gate * out[e * tm:(e + 1) * tm]
        if final:
            ms = jnp.mean(xnew * xnew, axis=-1, keepdims=True)
            o_ref[e] = xnew * lax.rsqrt(ms + EPS) * fg_ref[...]
        else:
            o_ref[e] = xnew


def _output(hseg, att, hf, hb, moz, head_g, w_out, mod_l, final_g, *, row0, att_row0, is_ctx, final):
    bsz, seg_len, d = hseg.shape
    tm = ROW_TILE
    ne = math.gcd(bsz, PROJ_BATCH)
    t0 = row0 // tm
    a0 = att_row0 // tm

    def seg_map(b, j):
        return (b, j, 0)

    def row_map(b, j):
        return (b, j + t0, 0)

    if is_ctx:
        mod_spec = pl.BlockSpec((1, 3, d), lambda b, j: (mod_l.shape[0] - 1, 0, 0))
    else:
        mod_spec = pl.BlockSpec((ne, 3, d), lambda b, j: (b, 0, 0))
    in_specs = [
        pl.BlockSpec((ne, tm, d), seg_map),
        pl.BlockSpec((ne, tm, ATT_WIDTH), lambda b, j: (b, j + a0, 0)),
        pl.BlockSpec((ne, tm, M_WIDTH), row_map),
        pl.BlockSpec((ne, tm, M_WIDTH), row_map),
        pl.BlockSpec((ne, tm, 2 * M_WIDTH), row_map),
        pl.BlockSpec((1, M_WIDTH), lambda b, j: (0, 0)),
        pl.BlockSpec((d, d), lambda b, j: (0, 0), pipeline_mode=pl.Buffered(1)),
        mod_spec,
    ]
    args = [hseg, att, hf, hb, moz, head_g, w_out, mod_l]
    if final:
        in_specs.append(pl.BlockSpec((1, d), lambda b, j: (0, 0)))
        args.append(final_g)
    return pl.pallas_call(
        functools.partial(_out_kernel, final=final),
        grid=(bsz // ne, seg_len // tm),
        in_specs=in_specs,
        out_specs=pl.BlockSpec((ne, tm, d), seg_map),
        out_shape=jax.ShapeDtypeStruct(hseg.shape, jnp.float32),
        compiler_params=pltpu.CompilerParams(
            dimension_semantics=("arbitrary", "arbitrary"), vmem_limit_bytes=VMEM_LIMIT),
        name="gate_norm_out_projection",
    )(*args)


def kernel(x, c, ctx, c_ctx, w_ada, b_ada, norm_g, w_in, conv_w, conv_b, gate_b, sink, head_g, w_out, final_g):
    bsz, t_len, d = x.shape
    ctx_len = ctx.shape[1]
    depth = w_in.shape[0]
    assert d == D_MODEL and t_len % ROW_TILE == 0 and ctx_len % ROW_TILE == 0 and t_len % GRID_W == 0

    hc, hx = ctx, x
    mod_rows = -(-(bsz + 1) // SUBLANES) * SUBLANES
    cc = jnp.zeros((mod_rows, d), jnp.float32).at[:bsz].set(c).at[mod_rows - 1].set(c_ctx)
    mod = _modulation(cc, w_ada, b_ada).reshape(depth, mod_rows, 3, d)

    w_qk, w_main, w_vg = _split_w_in(w_in)
    w_o = w_out.astype(jnp.bfloat16)
    cos, sin = _rope_tables(t_len, ctx_len)
    kscale = jnp.concatenate([jnp.ones((1, M_WIDTH), jnp.float32),
                              jnp.full((1, M_WIDTH), M_HEAD_DIM ** -0.5, jnp.float32)], axis=1)
    gate_bias = jnp.pad(gate_b, ((0, 0), (0, LANES - N_GATES)))

    for l in range(depth):
        last = l == depth - 1
        q, kx, v, az, mqk, mv, moz, gates = _projection(
            hc, hx, mod[l], norm_g[l][None], w_qk[l], w_main[l], w_vg[l], cos, sin, conv_w[l], conv_b[l][None], kscale,
            gate_bias[l][None])
        att = _attention(sink[l], q, kx, v, az, ctx_len=ctx_len, skip_ctx=last)
        hf, hb = _mlstm(mqk, mv, gates, ctx_len=ctx_len)
        out_args = (att, hf, hb, moz, head_g[l][None], w_o[l], mod[l], final_g[None])
        if last:
            return _output(hx, *out_args, row0=ctx_len, att_row0=0, is_ctx=False, final=True)
        hc, hx = (_output(hc, *out_args, row0=0, att_row0=0, is_ctx=True, final=False),
                  _output(hx, *out_args, row0=ctx_len, att_row0=ctx_len, is_ctx=False, final=False))
```

```python
import functools
import math

import jax
import jax.numpy as jnp
from jax import lax
from jax.experimental import pallas as pl
from jax.experimental.pallas import tpu as pltpu

D_MODEL = 1024
GRID_W = 64
ATT_HEADS = 8
ATT_KV_HEADS = 2
ATT_GROUP = ATT_HEADS // ATT_KV_HEADS
ATT_HEAD_DIM = 64
ATT_WIDTH = ATT_HEADS * ATT_HEAD_DIM
KV_WIDTH = ATT_KV_HEADS * ATT_HEAD_DIM
WINDOW = 128
BLOCK = 128
ROPE_BASE = 10000.0
M_HEADS = 4
M_HEAD_DIM = 128
M_WIDTH = M_HEADS * M_HEAD_DIM
CONV_K = 5
CHUNK = 128
N_DIR = 2
N_GATES = 2 * N_DIR * M_HEADS
EPS = 1e-6

LANES = 128
SUBLANES = 8
HALO = SUBLANES
ROW_TILE = 256
BATCH_PER_STEP = 8
PROJ_BATCH = 4
PROJ_STACK = 1
VMEM_LIMIT = 56 * 1024 * 1024

W_IN_MAIN_START = ATT_WIDTH + 2 * KV_WIDTH
C_AZ = 0
C_MQK = C_AZ + ATT_WIDTH
C_MV = C_MQK + 2 * M_WIDTH
C_MOZ = C_MV + M_WIDTH
N_MAIN = C_MOZ + 2 * M_WIDTH
KV_GROUP_COLS = ATT_GROUP * ATT_HEAD_DIM
LOG2_E = 1.4426950408889634
Q_PRESCALE = ATT_HEAD_DIM ** -0.5 * LOG2_E


def _sigmoid(x):
    return 1.0 / (1.0 + jnp.exp(-x))


def _silu(x):
    return x * _sigmoid(x)


def _log_sigmoid(x):
    return jnp.minimum(x, 0.0) - jnp.log(1.0 + jnp.exp(-jnp.abs(x)))


def _split_w_in(w_in):
    depth, d, n_in = w_in.shape
    quarter = ATT_HEAD_DIM // 4
    k0 = ATT_WIDTH
    v0 = k0 + KV_WIDTH
    wq = w_in[:, :, :k0].reshape(depth, d, ATT_KV_HEADS, ATT_GROUP, 2, 2, quarter)
    wq = wq.transpose(0, 1, 2, 5, 3, 4, 6).reshape(depth, d, ATT_WIDTH)
    wk = w_in[:, :, k0:v0].reshape(depth, d, ATT_KV_HEADS, 1, 2, 2, quarter)
    wk = jnp.broadcast_to(wk.transpose(0, 1, 2, 5, 3, 4, 6),
                          (depth, d, ATT_KV_HEADS, 2, ATT_GROUP, 2, quarter)).reshape(depth, d, ATT_WIDTH)
    w_qk = jnp.concatenate([wq, wk], axis=2).astype(jnp.bfloat16)
    w_main, w_vg = pl.pallas_call(
        _cast_w_kernel,
        grid=(depth, d // ROW_TILE),
        in_specs=[pl.BlockSpec((1, ROW_TILE, n_in), lambda l, r: (l, r, 0))],
        out_specs=[pl.BlockSpec((1, ROW_TILE, N_MAIN), lambda l, r: (l, r, 0)),
                   pl.BlockSpec((1, ROW_TILE, KV_WIDTH + LANES), lambda l, r: (l, r, 0))],
        out_shape=(jax.ShapeDtypeStruct((depth, d, N_MAIN), jnp.bfloat16),
                   jax.ShapeDtypeStruct((depth, d, KV_WIDTH + LANES), jnp.bfloat16)),
        compiler_params=pltpu.CompilerParams(dimension_semantics=("arbitrary", "arbitrary")),
        name="cast_projection_weights",
    )(w_in)
    return w_qk, w_main, w_vg


def _cast_w_kernel(w_ref, wm_ref, wvg_ref):
    main_end = W_IN_MAIN_START + N_MAIN
    wm_ref[0] = w_ref[0, :, W_IN_MAIN_START:main_end].astype(jnp.bfloat16)
    wvg_ref[0, :, :KV_WIDTH] = w_ref[0, :, W_IN_MAIN_START - KV_WIDTH:W_IN_MAIN_START].astype(jnp.bfloat16)
    wvg_ref[0, :, KV_WIDTH:] = jnp.zeros((w_ref.shape[1], LANES), jnp.bfloat16)
    wvg_ref[0, :, KV_WIDTH:KV_WIDTH + N_GATES] = w_ref[0, :, main_end:main_end + N_GATES].astype(jnp.bfloat16)


def _rope_tables(t_len, ctx_len):
    rows = t_len // GRID_W
    row = jnp.repeat(jnp.arange(rows), GRID_W).astype(jnp.float32)
    col = jnp.tile(jnp.arange(GRID_W), rows).astype(jnp.float32)
    half = ATT_HEAD_DIM // 2
    inv = ROPE_BASE ** (-jnp.arange(0, half, 2, dtype=jnp.float32) / half)
    ang = jnp.concatenate([row[:, None] * inv, col[:, None] * inv], axis=-1)
    ang = jnp.tile(ang, (1, LANES // half))
    cos = jnp.concatenate([jnp.ones((ctx_len, LANES), jnp.float32), jnp.cos(ang)], axis=0)
    sin = jnp.concatenate([jnp.zeros((ctx_len, LANES), jnp.float32), jnp.sin(ang)], axis=0)
    return cos, sin


def _mod_kernel(c_ref, w_ref, b_ref, o_ref):
    sc = _silu(c_ref[...])
    o_ref[0] = jnp.dot(sc, w_ref[0], preferred_element_type=jnp.float32,
                       precision=lax.Precision.HIGHEST) + b_ref[0]


def _modulation(cc, w_ada, b_ada):
    depth, d, d3 = w_ada.shape
    rows = cc.shape[0]
    nt = d3 // d
    return pl.pallas_call(
        _mod_kernel,
        grid=(depth, nt),
        in_specs=[
            pl.BlockSpec((rows, d), lambda l, n: (0, 0)),
            pl.BlockSpec((1, d, d), lambda l, n: (l, 0, n)),
            pl.BlockSpec((1, 1, d), lambda l, n: (l, 0, n)),
        ],
        out_specs=pl.BlockSpec((1, rows, d), lambda l, n: (l, 0, n)),
        out_shape=jax.ShapeDtypeStruct((depth, rows, d3), jnp.float32),
        compiler_params=pltpu.CompilerParams(dimension_semantics=("arbitrary", "arbitrary")),
        name="adaln_modulation",
    )(cc, w_ada, b_ada.reshape(depth, 1, d3))


def _proj_kernel(*refs, tm, tiles_ctx, tiles_total):
    n_elem = refs[4].shape[0]
    for e0 in range(0, n_elem, PROJ_STACK):
        _proj_group(e0, min(PROJ_STACK, n_elem - e0), *refs, tm=tm, tiles_ctx=tiles_ctx, tiles_total=tiles_total)


def _proj_group(e0, ne, cp_ref, c_ref, cn_ref, xp_ref, x_ref, xn_ref, mod_ref, modc_ref, g_ref,
                wqk_ref, wm_ref, wvg_ref,
                cos_ref, sin_ref, cw_ref, cb_ref,
                ks_ref, gb_ref,
                q_ref, kx_ref, v_ref, az_ref, mqk_ref, mv_ref, moz_ref, gate_ref,
                *, tm, tiles_ctx, tiles_total):
    te = tm + 2 * HALO
    j = pl.program_id(1)
    is_ctx = j < tiles_ctx
    seg_first = jnp.logical_or(j == 0, j == tiles_ctx)
    seg_last = jnp.logical_or(j == tiles_ctx - 1, j == tiles_total - 1)

    ext = []
    for e in range(e0, e0 + ne):
        xe = jnp.concatenate([jnp.where(is_ctx, cp_ref[e], xp_ref[e]), jnp.where(is_ctx, c_ref[e], x_ref[e]),
                              jnp.where(is_ctx, cn_ref[e], xn_ref[e])], axis=0)
        ms = jnp.mean(xe * xe, axis=-1, keepdims=True)
        shift = jnp.where(is_ctx, modc_ref[0, 0:1, :], mod_ref[e, 0:1, :])
        scale = jnp.where(is_ctx, modc_ref[0, 1:2, :], mod_ref[e, 1:2, :])
        gain = g_ref[...] * (1.0 + scale)
        ext.append((xe * lax.rsqrt(ms + EPS) * gain + shift).astype(jnp.bfloat16))
    xn_ext = jnp.concatenate(ext, axis=0)
    xn = jnp.concatenate([x_[HALO:HALO + tm] for x_ in ext], axis=0)

    def proj(rows, w_ref, c0, c1):
        return jnp.dot(rows, w_ref[:, c0:c1], preferred_element_type=jnp.float32)

    row = lax.broadcasted_iota(jnp.int32, (te, 1), 0)
    keep = jnp.logical_and(jnp.logical_or(row >= HALO, jnp.logical_not(seg_first)),
                           jnp.logical_or(row < HALO + tm, jnp.logical_not(seg_last)))
    nb = tm // SUBLANES
    sub = lax.broadcasted_iota(jnp.int32, (1, SUBLANES, 1), 1)

    def conv_chunk(c0, width):
        y_all = proj(xn_ext, wm_ref, C_MQK + c0, C_MQK + c0 + width)
        for i, e in enumerate(range(e0, e0 + ne)):
            ye = jnp.where(keep, y_all[i * te:(i + 1) * te], 0.0).reshape(nb + 2, SUBLANES, width)
            acc = jnp.broadcast_to(cb_ref[:, c0:c0 + width].reshape(1, 1, width), (nb, SUBLANES, width))
            for t in range(CONV_K):
                delta = t - CONV_K // 2
                w_t = cw_ref[t:t + 1, c0:c0 + width].reshape(1, 1, width)
                if delta == 0:
                    tap = ye[1:nb + 1]
                else:
                    rot = pltpu.roll(ye, (-delta) % SUBLANES, axis=1)
                    if delta > 0:
                        tap = jnp.where(sub < SUBLANES - delta, rot[1:nb + 1], rot[2:nb + 2])
                    else:
                        tap = jnp.where(sub >= -delta, rot[1:nb + 1], rot[0:nb])
                acc = acc + w_t * tap
            acc = acc.reshape(tm, width)
            mqk_ref[e, :, c0:c0 + width] = (_silu(acc) * ks_ref[:, c0:c0 + width]).astype(jnp.bfloat16)

    def rope_group(grp):
        is_q = grp < ATT_KV_HEADS
        cos = cos_ref[...] * Q_PRESCALE if is_q else cos_ref[...]
        sin = sin_ref[...] * Q_PRESCALE if is_q else sin_ref[...]
        yq = proj(xn, wqk_ref, grp * KV_GROUP_COLS, (grp + 1) * KV_GROUP_COLS)
        dst = q_ref if is_q else kx_ref
        c0 = (grp % ATT_KV_HEADS) * KV_GROUP_COLS
        for i, e in enumerate(range(e0, e0 + ne)):
            first = yq[i * tm:(i + 1) * tm, :LANES]
            second = yq[i * tm:(i + 1) * tm, LANES:]
            dst[e, :, c0:c0 + LANES] = (first * cos - second * sin).astype(jnp.bfloat16)
            dst[e, :, c0 + LANES:c0 + 2 * LANES] = (second * cos + first * sin).astype(jnp.bfloat16)

    def plain(dst_ref, c0, width, off=0):
        y = proj(xn, wm_ref, c0, c0 + width)
        for i, e in enumerate(range(e0, e0 + ne)):
            dst_ref[e, :, off:off + width] = y[i * tm:(i + 1) * tm].astype(dst_ref.dtype)

    others = [functools.partial(rope_group, g) for g in range(2 * ATT_KV_HEADS)]
    others += [functools.partial(plain, az_ref, C_AZ, ATT_WIDTH),
               functools.partial(plain, mv_ref, C_MV, M_WIDTH),
               functools.partial(plain, moz_ref, C_MOZ, M_WIDTH),
               functools.partial(plain, moz_ref, C_MOZ + M_WIDTH, M_WIDTH, M_WIDTH)]
    per_chunk = 2
    cw = 2 * M_WIDTH * per_chunk // len(others)
    for i in range(len(others) // per_chunk):
        conv_chunk(i * cw, cw)
        for other in others[i * per_chunk:(i + 1) * per_chunk]:
            other()
    y = proj(xn, wvg_ref, 0, KV_WIDTH + LANES)
    for i, e in enumerate(range(e0, e0 + ne)):
        v_ref[e] = y[i * tm:(i + 1) * tm, :KV_WIDTH].astype(jnp.bfloat16)
        gate_ref[e] = y[i * tm:(i + 1) * tm, KV_WIDTH:] + gb_ref[...]


def _tile_with_halo_specs(ne, tm, d, n_rows, tile_of):
    hb = tm // HALO
    last = n_rows // HALO - 1
    return (pl.BlockSpec((ne, HALO, d), lambda b, j: (b, jnp.maximum(tile_of(j) * hb - 1, 0), 0)),
            pl.BlockSpec((ne, tm, d), lambda b, j: (b, tile_of(j), 0)),
            pl.BlockSpec((ne, HALO, d), lambda b, j: (b, jnp.minimum((tile_of(j) + 1) * hb, last), 0)))


def _projection(layer, hc, hx, mod, norm_g, w_qk, w_main, w_vg, cos, sin, conv_w, conv_b, kscale, gate_b):
    bsz, ctx_len, d = hc.shape
    s_len = ctx_len + hx.shape[1]
    tm = ROW_TILE
    ne = math.gcd(bsz, PROJ_BATCH)
    tiles_total = s_len // tm
    tiles_ctx = ctx_len // tm
    ctx_blk = mod.shape[1] - 1

    def row_map(b, j):
        return (b, j, 0)

    def per_layer(*block):
        return pl.BlockSpec((None,) + block, lambda b, j: (layer,) + (0,) * len(block))

    kernel = functools.partial(_proj_kernel, tm=tm, tiles_ctx=tiles_ctx, tiles_total=tiles_total)
    out_shapes = (
        jax.ShapeDtypeStruct((bsz, s_len, ATT_WIDTH), jnp.bfloat16),
        jax.ShapeDtypeStruct((bsz, s_len, ATT_WIDTH), jnp.bfloat16),
        jax.ShapeDtypeStruct((bsz, s_len, KV_WIDTH), jnp.bfloat16),
        jax.ShapeDtypeStruct((bsz, s_len, ATT_WIDTH), jnp.bfloat16),
        jax.ShapeDtypeStruct((bsz, s_len, 2 * M_WIDTH), jnp.bfloat16),
        jax.ShapeDtypeStruct((bsz, s_len, M_WIDTH), jnp.bfloat16),
        jax.ShapeDtypeStruct((bsz, s_len, 2 * M_WIDTH), jnp.bfloat16),
        jax.ShapeDtypeStruct((bsz, s_len, LANES), jnp.float32),
    )
    return pl.pallas_call(
        kernel,
        grid=(bsz // ne, tiles_total),
        in_specs=[
            *_tile_with_halo_specs(ne, tm, d, ctx_len, lambda j: jnp.minimum(j, tiles_ctx - 1)),
            *_tile_with_halo_specs(ne, tm, d, s_len - ctx_len, lambda j: jnp.maximum(j - tiles_ctx, 0)),
            pl.BlockSpec((None, ne, 3, d), lambda b, j: (layer, b, 0, 0)),
            pl.BlockSpec((None, 1, 3, d), lambda b, j: (layer, ctx_blk, 0, 0)),
            per_layer(1, d),
            pl.BlockSpec((None, d, 2 * ATT_WIDTH), lambda b, j: (layer, 0, 0), pipeline_mode=pl.Buffered(1)),
            pl.BlockSpec((None, d, N_MAIN), lambda b, j: (layer, 0, 0), pipeline_mode=pl.Buffered(1)),
            pl.BlockSpec((None, d, KV_WIDTH + LANES), lambda b, j: (layer, 0, 0), pipeline_mode=pl.Buffered(1)),
            pl.BlockSpec((tm, LANES), lambda b, j: (j, 0)),
            pl.BlockSpec((tm, LANES), lambda b, j: (j, 0)),
            per_layer(CONV_K, 2 * M_WIDTH),
            per_layer(1, 2 * M_WIDTH),
            pl.BlockSpec((1, 2 * M_WIDTH), lambda b, j: (0, 0)),
            per_layer(1, LANES),
        ],
        out_specs=[
            pl.BlockSpec((ne, tm, ATT_WIDTH), row_map),
            pl.BlockSpec((ne, tm, ATT_WIDTH), row_map),
            pl.BlockSpec((ne, tm, KV_WIDTH), row_map),
            pl.BlockSpec((ne, tm, ATT_WIDTH), row_map),
            pl.BlockSpec((ne, tm, 2 * M_WIDTH), row_map),
            pl.BlockSpec((ne, tm, M_WIDTH), row_map),
            pl.BlockSpec((ne, tm, 2 * M_WIDTH), row_map),
            pl.BlockSpec((ne, tm, LANES), row_map),
        ],
        out_shape=out_shapes,
        compiler_params=pltpu.CompilerParams(
            dimension_semantics=("arbitrary", "arbitrary"), vmem_limit_bytes=VMEM_LIMIT),
        name="norm_mod_projection",
    )(hc, hc, hc, hx, hx, hx, mod, mod, norm_g, w_qk, w_main, w_vg, cos, sin, conv_w, conv_b, kscale, gate_b)


def _attn_kernel(sink_ref, q_ref, kp_ref, kc_ref, kn_ref, kctx_ref, vp_ref, vc_ref, vn_ref, vctx_ref,
                 az_ref, o_ref, *, layer, blk0, ctx_blocks, total_blocks):
    qb = pl.program_id(1) + blk0
    is_ctx = qb < ctx_blocks
    n = qb - ctx_blocks
    lat_blocks = total_blocks - ctx_blocks
    ctx_len = ctx_blocks * BLOCK
    win = 3 * BLOCK
    neg = -jnp.inf

    i = lax.broadcasted_iota(jnp.int32, (BLOCK, win), 0)
    jx = lax.broadcasted_iota(jnp.int32, (BLOCK, win), 1)
    lo = jnp.where(n == 0, BLOCK, 0)
    hi = jnp.where(n == lat_blocks - 1, 2 * BLOCK, win)
    valid = (jx >= i) & (jx <= i + 2 * WINDOW) & (jx >= lo) & (jx < hi) & jnp.logical_not(is_ctx)
    bias = jnp.concatenate([jnp.where(valid, 0.0, neg)] * ATT_GROUP, axis=0)

    lane_v = lax.broadcasted_iota(jnp.int32, (1, LANES), 1)
    low_half = lane_v < ATT_HEAD_DIM
    lane_q = lax.broadcasted_iota(jnp.int32, (1, KV_GROUP_COLS), 1)
    head_of_qlane = (lane_q % LANES) // (LANES // ATT_GROUP)
    row_head = lax.broadcasted_iota(jnp.int32, (ATT_GROUP * BLOCK, 1), 0) // BLOCK
    n_keys = win + ctx_len
    ones = jnp.ones((n_keys, LANES), jnp.bfloat16)

    for e_i in range(q_ref.shape[0]):
        k_all = jnp.concatenate([kp_ref[e_i], kc_ref[e_i], kn_ref[e_i], kctx_ref[e_i]], axis=0)
        v_all = jnp.concatenate([vp_ref[e_i], vc_ref[e_i], vn_ref[e_i], vctx_ref[e_i]],
                                axis=0).astype(jnp.float32)
        v_rot = pltpu.roll(v_all, ATT_HEAD_DIM, axis=1)
        for kvh in range(ATT_KV_HEADS):
            c0 = kvh * KV_GROUP_COLS
            qg = q_ref[e_i, :, c0:c0 + KV_GROUP_COLS]
            qm = jnp.concatenate(
                [jnp.where(head_of_qlane == g, qg, jnp.zeros_like(qg)) for g in range(ATT_GROUP)], axis=0)
            kx = k_all[:, c0:c0 + KV_GROUP_COLS]
            s = lax.dot_general(qm, kx, (((1,), (1,)), ((), ())), preferred_element_type=jnp.float32)
            s_win = s[:, :win] + bias
            s_ctx = s[:, win:]
            sink = jnp.zeros((ATT_GROUP * BLOCK, 1), jnp.float32)
            for g in range(ATT_GROUP):
                sink = jnp.where(row_head == g, sink_ref[layer, kvh * ATT_GROUP + g] * LOG2_E, sink)
            m = jnp.maximum(jnp.maximum(jnp.max(s_win, axis=-1, keepdims=True),
                                        jnp.max(s_ctx, axis=-1, keepdims=True)), sink)
            e = jnp.concatenate([jnp.exp2(s_win - m), jnp.exp2(s_ctx - m)], axis=1).astype(jnp.bfloat16)
            own_half = low_half if kvh == 0 else jnp.logical_not(low_half)
            v_two = jnp.where(own_half, v_all, v_rot).astype(jnp.bfloat16)
            res = jnp.dot(e, jnp.concatenate([v_two, ones], axis=1),
                          preferred_element_type=jnp.float32)
            o = res[:, :LANES] / (res[:, LANES:] + jnp.exp2(sink - m))
            az = az_ref[e_i, :, c0:c0 + KV_GROUP_COLS].astype(jnp.float32)
            for pair in range(ATT_GROUP // 2):
                both = jnp.where(low_half, o[2 * pair * BLOCK:(2 * pair + 1) * BLOCK],
                                 o[(2 * pair + 1) * BLOCK:(2 * pair + 2) * BLOCK])
                lo = c0 + pair * LANES
                o_ref[e_i, :, lo:lo + LANES] = (
                    both * _silu(az[:, pair * LANES:(pair + 1) * LANES])).astype(jnp.bfloat16)


def _attention(layer, sink, q, kx, v, az, *, ctx_len, skip_ctx):
    bsz, s_len, _ = q.shape
    ctx_blocks = ctx_len // BLOCK
    total_blocks = s_len // BLOCK
    blk0 = ctx_blocks if skip_ctx else 0
    lo, hi = ctx_blocks, total_blocks - 1
    nb = math.gcd(bsz, BATCH_PER_STEP)

    def at(off):
        return lambda b, j: (b, jnp.clip(j + blk0 + off, lo, hi), 0)

    def cur(b, j):
        return (b, j + blk0, 0)

    def ctx(b, j):
        return (b, 0, 0)

    kernel = functools.partial(_attn_kernel, layer=layer, blk0=blk0, ctx_blocks=ctx_blocks, total_blocks=total_blocks)
    return pl.pallas_call(
        kernel,
        grid=(bsz // nb, total_blocks - blk0),
        in_specs=[
            pl.BlockSpec(memory_space=pltpu.SMEM),
            pl.BlockSpec((nb, BLOCK, ATT_WIDTH), cur),
            pl.BlockSpec((nb, BLOCK, ATT_WIDTH), at(-1)),
            pl.BlockSpec((nb, BLOCK, ATT_WIDTH), at(0)),
            pl.BlockSpec((nb, BLOCK, ATT_WIDTH), at(1)),
            pl.BlockSpec((nb, ctx_len, ATT_WIDTH), ctx),
            pl.BlockSpec((nb, BLOCK, KV_WIDTH), at(-1)),
            pl.BlockSpec((nb, BLOCK, KV_WIDTH), at(0)),
            pl.BlockSpec((nb, BLOCK, KV_WIDTH), at(1)),
            pl.BlockSpec((nb, ctx_len, KV_WIDTH), ctx),
            pl.BlockSpec((nb, BLOCK, ATT_WIDTH), cur),
        ],
        out_specs=pl.BlockSpec((nb, BLOCK, ATT_WIDTH), lambda b, j: (b, j, 0)),
        out_shape=jax.ShapeDtypeStruct((bsz, s_len - blk0 * BLOCK, ATT_WIDTH), jnp.bfloat16),
        compiler_params=pltpu.CompilerParams(
            dimension_semantics=("arbitrary", "arbitrary"), vmem_limit_bytes=VMEM_LIMIT),
        name="window_ctx_attention",
    )(sink, q, kx, kx, kx, kx, v, v, v, v, az)


STATE_ROWS = M_HEAD_DIM + SUBLANES


def _split2(x):
    hi = x.astype(jnp.bfloat16)
    lo = (x - hi.astype(jnp.float32)).astype(jnp.bfloat16)
    return hi, lo


def _mlstm_kernel(qkf_ref, vf_ref, gf_ref, qkb_ref, vb_ref, gb_ref, hf_ref, hb_ref, c_ref, m_ref):
    @pl.when(pl.program_id(1) == 0)
    def _():
        c_ref[...] = jnp.zeros_like(c_ref)
        m_ref[...] = jnp.full_like(m_ref, -jnp.inf)

    rows = lax.broadcasted_iota(jnp.int32, (CHUNK, CHUNK), 0)
    cols = lax.broadcasted_iota(jnp.int32, (CHUNK, CHUNK), 1)
    le = rows <= cols
    ge = rows >= cols
    tri_le = jnp.where(le, 1.0, 0.0).astype(jnp.bfloat16)
    tri_ge = jnp.where(ge, 1.0, 0.0).astype(jnp.bfloat16)

    for e in range(qkf_ref.shape[0]):
        _mlstm_element(e, qkf_ref, vf_ref, gf_ref, qkb_ref, vb_ref, gb_ref, hf_ref, hb_ref, c_ref, m_ref,
                       le, ge, tri_le, tri_ge)


def _mlstm_element(e, qkf_ref, vf_ref, gf_ref, qkb_ref, vb_ref, gb_ref, hf_ref, hb_ref, c_ref, m_ref,
                   le, ge, tri_le, tri_ge):
    nst = N_DIR * M_HEADS
    gates = (gf_ref[e], gb_ref[e])
    gates_t = (gates[0].T, gates[1].T)

    i8 = jnp.concatenate([gates_t[0][0:M_HEADS], gates_t[1][M_HEADS:nst]], axis=0)
    lf8 = _log_sigmoid(jnp.concatenate([gates_t[0][nst:nst + M_HEADS],
                                        gates_t[1][nst + M_HEADS:2 * nst]], axis=0))
    pre8 = sum(jnp.dot(p, tri_le, preferred_element_type=jnp.float32) for p in _split2(lf8))
    tot8 = pre8[:, CHUNK - 1:CHUNK]
    is_fwd = lax.broadcasted_iota(jnp.int32, (nst, 1), 0) < M_HEADS
    b8 = jnp.where(is_fwd, pre8, tot8 - pre8 + lf8)
    r8 = i8 - b8
    g8 = tot8 + r8
    m_prev = m_ref[e]
    m_new = jnp.maximum(tot8 + m_prev, jnp.max(g8, axis=1, keepdims=True))
    a8 = jnp.exp(tot8 + m_prev - m_new)
    w8 = jnp.exp(g8 - m_new)
    inter8 = b8 + m_prev
    m_ref[e] = m_new

    ones_row = jnp.where(lax.broadcasted_iota(jnp.int32, (SUBLANES, CHUNK), 0) == 0, 1.0, 0.0)

    lf_c2 = _log_sigmoid(jnp.concatenate(gates, axis=1))
    pre_c2 = sum(jnp.dot(tri_ge, p, preferred_element_type=jnp.float32) for p in _split2(lf_c2))

    for d, (qk_ref, v_ref, h_ref) in enumerate(((qkf_ref, vf_ref, hf_ref), (qkb_ref, vb_ref, hb_ref))):
        valid_t = le if d == 0 else ge
        lf_c = lf_c2[:, d * LANES:(d + 1) * LANES]
        pre_c = pre_c2[:, d * LANES:(d + 1) * LANES]
        b_c = pre_c if d == 0 else pre_c[CHUNK - 1:CHUNK, :] - pre_c + lf_c
        r_c = pltpu.roll(gates[d], nst, axis=1) - b_c

        heads = range(M_HEADS)
        rs = [d * M_HEADS + hd for hd in heads]
        qs = [qk_ref[e, :, hd * M_HEAD_DIM:(hd + 1) * M_HEAD_DIM] for hd in heads]
        ks = [qk_ref[e, :, M_WIDTH + hd * M_HEAD_DIM:M_WIDTH + (hd + 1) * M_HEAD_DIM] for hd in heads]
        vts = [jnp.concatenate([v_ref[e, :, hd * M_HEAD_DIM:(hd + 1) * M_HEAD_DIM].astype(jnp.float32).T,
                                ones_row], axis=0) for hd in heads]
        cs = [c_ref[e, r] for r in rs]

        qts = [q.astype(jnp.float32).T for q in qs]
        st = [jnp.dot(k, qt.astype(jnp.bfloat16), preferred_element_type=jnp.float32)
              for k, qt in zip(ks, qts)]
        dt = [jnp.where(valid_t, r_c[:, nst + r:nst + r + 1] + b8[r:r + 1, :], -jnp.inf) for r in rs]
        m_t = [jnp.maximum(inter8[r:r + 1, :], jnp.max(x, axis=0, keepdims=True)) for r, x in zip(rs, dt)]
        w_i = [jnp.exp(inter8[r:r + 1, :] - m) for r, m in zip(rs, m_t)]
        pt = [s_ * jnp.exp(x - m) for s_, x, m in zip(st, dt, m_t)]
        both = [jnp.dot(jnp.concatenate([vt, c], axis=1).astype(jnp.bfloat16),
                        jnp.concatenate([p, qt * w], axis=0).astype(jnp.bfloat16),
                        preferred_element_type=jnp.float32)
                for vt, c, p, qt, w in zip(vts, cs, pt, qts, w_i)]
        for hd in heads:
            den = both[hd][M_HEAD_DIM:M_HEAD_DIM + 1]
            ht = both[hd][:M_HEAD_DIM] / jnp.maximum(jnp.abs(den), jnp.exp(-m_t[hd]))
            h_ref[e, :, hd * M_HEAD_DIM:(hd + 1) * M_HEAD_DIM] = ht.T.astype(h_ref.dtype)
        for hd, r in zip(heads, rs):
            vw = (vts[hd] * w8[r:r + 1, :]).astype(jnp.bfloat16)
            c_ref[e, r] = a8[r:r + 1, :] * cs[hd] + jnp.dot(vw, ks[hd], preferred_element_type=jnp.float32)


def _mlstm(mqk, mv, gates, *, ctx_len):
    bsz, s_len, _ = mqk.shape
    nc = s_len // CHUNK
    ncc = ctx_len // CHUNK

    def fwd(b, j):
        return (b, j, 0)

    def bwd(b, j):
        return (b, jnp.where(j < ncc, ncc - 1 - j, nc - 1 + ncc - j), 0)

    nstate = N_DIR * M_HEADS
    nb = math.gcd(bsz, BATCH_PER_STEP)
    return pl.pallas_call(
        _mlstm_kernel,
        grid=(bsz // nb, nc),
        in_specs=[
            pl.BlockSpec((nb, CHUNK, 2 * M_WIDTH), fwd),
            pl.BlockSpec((nb, CHUNK, M_WIDTH), fwd),
            pl.BlockSpec((nb, CHUNK, LANES), fwd),
            pl.BlockSpec((nb, CHUNK, 2 * M_WIDTH), bwd),
            pl.BlockSpec((nb, CHUNK, M_WIDTH), bwd),
            pl.BlockSpec((nb, CHUNK, LANES), bwd),
        ],
        out_specs=[
            pl.BlockSpec((nb, CHUNK, M_WIDTH), fwd),
            pl.BlockSpec((nb, CHUNK, M_WIDTH), bwd),
        ],
        out_shape=(jax.ShapeDtypeStruct((bsz, s_len, M_WIDTH), jnp.bfloat16),
                   jax.ShapeDtypeStruct((bsz, s_len, M_WIDTH), jnp.bfloat16)),
        scratch_shapes=[
            pltpu.VMEM((nb, nstate, STATE_ROWS, M_HEAD_DIM), jnp.float32),
            pltpu.VMEM((nb, nstate, LANES), jnp.float32),
        ],
        compiler_params=pltpu.CompilerParams(
            dimension_semantics=("arbitrary", "arbitrary"), vmem_limit_bytes=VMEM_LIMIT),
        name="mlstm_bidirectional_scan",
    )(mqk, mv, gates, mqk, mv, gates)


def _out_kernel(x_ref, att_ref, hf_ref, hb_ref, moz_ref, hg_ref, w_ref, mod_ref, *rest, final):
    if final:
        fg_ref, o_ref = rest
    else:
        (o_ref,) = rest
    ne, tm, _ = x_ref.shape
    mxs = []
    for e in range(ne):
        mo = moz_ref[e, :, :M_WIDTH].astype(jnp.float32)
        mz = moz_ref[e, :, M_WIDTH:].astype(jnp.float32)
        hs = _sigmoid(mo) * (hf_ref[e].astype(jnp.float32) + hb_ref[e].astype(jnp.float32))
        parts = []
        for hd in range(M_HEADS):
            hh = hs[:, hd * M_HEAD_DIM:(hd + 1) * M_HEAD_DIM]
            ms = jnp.mean(hh * hh, axis=-1, keepdims=True)
            parts.append(hh * lax.rsqrt(ms + EPS))
        mxs.append((jnp.concatenate(parts, axis=1) * hg_ref[...] * _silu(mz)).astype(jnp.bfloat16))
    att = jnp.concatenate([att_ref[e] for e in range(ne)], axis=0)
    out = (jnp.dot(att, w_ref[:ATT_WIDTH, :], preferred_element_type=jnp.float32)
           + jnp.dot(jnp.concatenate(mxs, axis=0), w_ref[ATT_WIDTH:, :], preferred_element_type=jnp.float32))
    for e in range(ne):
        gate = mod_ref[e if mod_ref.shape[0] > 1 else 0, 2:3, :]
        xnew = x_ref[e] + gate * out[e * tm:(e + 1) * tm]
        if final:
            ms = jnp.mean(xnew * xnew, axis=-1, keepdims=True)
            o_ref[e] = xnew * lax.rsqrt(ms + EPS) * fg_ref[...]
        else:
            o_ref[e] = xnew


def _output(layer, hseg, att, hf, hb, moz, head_g, w_out, mod, final_g, *, row0, att_row0, is_ctx, final):
    bsz, seg_len, d = hseg.shape
    tm = ROW_TILE
    ne = math.gcd(bsz, PROJ_BATCH)
    t0 = row0 // tm
    a0 = att_row0 // tm

    def seg_map(b, j):
        return (b, j, 0)

    def row_map(b, j):
        return (b, j + t0, 0)

    if is_ctx:
        mod_spec = pl.BlockSpec((None, 1, 3, d), lambda b, j: (layer, mod.shape[1] - 1, 0, 0))
    else:
        mod_spec = pl.BlockSpec((None, ne, 3, d), lambda b, j: (layer, b, 0, 0))
    in_specs = [
        pl.BlockSpec((ne, tm, d), seg_map),
        pl.BlockSpec((ne, tm, ATT_WIDTH), lambda b, j: (b, j + a0, 0)),
        pl.BlockSpec((ne, tm, M_WIDTH), row_map),
        pl.BlockSpec((ne, tm, M_WIDTH), row_map),
        pl.BlockSpec((ne, tm, 2 * M_WIDTH), row_map),
        pl.BlockSpec((None, 1, M_WIDTH), lambda b, j: (layer, 0, 0)),
        pl.BlockSpec((None, d, d), lambda b, j: (layer, 0, 0), pipeline_mode=pl.Buffered(1)),
        mod_spec,
    ]
    args = [hseg, att, hf, hb, moz, head_g, w_out, mod]
    if final:
        in_specs.append(pl.BlockSpec((1, d), lambda b, j: (0, 0)))
        args.append(final_g)
    return pl.pallas_call(
        functools.partial(_out_kernel, final=final),
        grid=(bsz // ne, seg_len // tm),
        in_specs=in_specs,
        out_specs=pl.BlockSpec((ne, tm, d), seg_map),
        out_shape=jax.ShapeDtypeStruct(hseg.shape, jnp.float32),
        compiler_params=pltpu.CompilerParams(
            dimension_semantics=("arbitrary", "arbitrary"), vmem_limit_bytes=VMEM_LIMIT),
        name="gate_norm_out_projection",
    )(*args)


def kernel(x, c, ctx, c_ctx, w_ada, b_ada, norm_g, w_in, conv_w, conv_b, gate_b, sink, head_g, w_out, final_g):
    bsz, t_len, d = x.shape
    ctx_len = ctx.shape[1]
    depth = w_in.shape[0]
    assert d == D_MODEL and t_len % ROW_TILE == 0 and ctx_len % ROW_TILE == 0 and t_len % GRID_W == 0

    hc, hx = ctx, x
    mod_rows = -(-(bsz + 1) // SUBLANES) * SUBLANES
    cc = jnp.zeros((mod_rows, d), jnp.float32).at[:bsz].set(c).at[mod_rows - 1].set(c_ctx)
    mod = _modulation(cc, w_ada, b_ada).reshape(depth, mod_rows, 3, d)

    w_qk, w_main, w_vg = _split_w_in(w_in)
    w_o = w_out.astype(jnp.bfloat16)
    cos, sin = _rope_tables(t_len, ctx_len)
    kscale = jnp.concatenate([jnp.ones((1, M_WIDTH), jnp.float32),
                              jnp.full((1, M_WIDTH), M_HEAD_DIM ** -0.5, jnp.float32)], axis=1)
    gate_bias = jnp.pad(gate_b, ((0, 0), (0, LANES - N_GATES)))[:, None, :]
    norm_g3, conv_b3, head_g3 = norm_g[:, None, :], conv_b[:, None, :], head_g[:, None, :]

    for l in range(depth):
        last = l == depth - 1
        q, kx, v, az, mqk, mv, moz, gates = _projection(
            l, hc, hx, mod, norm_g3, w_qk, w_main, w_vg, cos, sin, conv_w, conv_b3, kscale, gate_bias)
        att = _attention(l, sink, q, kx, v, az, ctx_len=ctx_len, skip_ctx=last)
        hf, hb = _mlstm(mqk, mv, gates, ctx_len=ctx_len)
        out_args = (att, hf, hb, moz, head_g3, w_o, mod, final_g[None])
        if last:
            return _output(l, hx, *out_args, row0=ctx_len, att_row0=0, is_ctx=False, final=True)
        hc, hx = (_output(l, hc, *out_args, row0=0, att_row0=0, is_ctx=True, final=False),
                  _output(l, hx, *out_args, row0=ctx_len, att_row0=ctx_len, is_ctx=False, final=False))
```

```python
import functools
import math

import jax
import jax.numpy as jnp
from jax import lax
from jax.experimental import pallas as pl
from jax.experimental.pallas import tpu as pltpu

D_MODEL = 1024
GRID_W = 64
ATT_HEADS = 8
ATT_KV_HEADS = 2
ATT_GROUP = ATT_HEADS // ATT_KV_HEADS
ATT_HEAD_DIM = 64
ATT_WIDTH = ATT_HEADS * ATT_HEAD_DIM
KV_WIDTH = ATT_KV_HEADS * ATT_HEAD_DIM
WINDOW = 128
BLOCK = 128
ROPE_BASE = 10000.0
M_HEADS = 4
M_HEAD_DIM = 128
M_WIDTH = M_HEADS * M_HEAD_DIM
CONV_K = 5
CHUNK = 128
N_DIR = 2
N_GATES = 2 * N_DIR * M_HEADS
EPS = 1e-6

LANES = 128
SUBLANES = 8
HALO = SUBLANES
ROW_TILE = 256
BATCH_PER_STEP = 8
PROJ_BATCH = 4
PROJ_STACK = 1
VMEM_LIMIT = 56 * 1024 * 1024

W_IN_MAIN_START = ATT_WIDTH + 2 * KV_WIDTH
C_AZ = 0
C_MQK = C_AZ + ATT_WIDTH
C_MV = C_MQK + 2 * M_WIDTH
C_MOZ = C_MV + M_WIDTH
N_MAIN = C_MOZ + 2 * M_WIDTH
KV_GROUP_COLS = ATT_GROUP * ATT_HEAD_DIM
LOG2_E = 1.4426950408889634
Q_PRESCALE = ATT_HEAD_DIM ** -0.5 * LOG2_E


def _sigmoid(x):
    return 1.0 / (1.0 + jnp.exp(-x))


def _silu(x):
    return x * _sigmoid(x)


def _log_sigmoid(x):
    return jnp.minimum(x, 0.0) - jnp.log(1.0 + jnp.exp(-jnp.abs(x)))


def _split_w_in(w_in):
    depth, d, n_in = w_in.shape
    w_t = jnp.swapaxes(w_in, 1, 2)
    quarter = ATT_HEAD_DIM // 4
    k0 = ATT_WIDTH
    v0 = k0 + KV_WIDTH
    wq = w_t[:, :k0].reshape(depth, ATT_KV_HEADS, ATT_GROUP, 2, 2, quarter, d)
    wq = wq.transpose(0, 1, 4, 2, 3, 5, 6).reshape(depth, ATT_WIDTH, d)
    wk = w_t[:, k0:v0].reshape(depth, ATT_KV_HEADS, 1, 2, 2, quarter, d)
    wk = jnp.broadcast_to(wk.transpose(0, 1, 4, 2, 3, 5, 6),
                          (depth, ATT_KV_HEADS, 2, ATT_GROUP, 2, quarter, d)).reshape(depth, ATT_WIDTH, d)
    w_qk = jnp.concatenate([wq, wk], axis=1).astype(jnp.bfloat16)
    main_blk0 = W_IN_MAIN_START // ROW_TILE
    v_blk = (W_IN_MAIN_START - KV_WIDTH) // KV_WIDTH
    g_blk = (W_IN_MAIN_START + N_MAIN) // N_GATES
    w_main, w_vg = pl.pallas_call(
        _cast_w_kernel,
        grid=(depth, N_MAIN // ROW_TILE),
        in_specs=[pl.BlockSpec((1, ROW_TILE, d), lambda l, r: (l, r + main_blk0, 0)),
                  pl.BlockSpec((1, KV_WIDTH, d), lambda l, r: (l, v_blk, 0)),
                  pl.BlockSpec((1, N_GATES, d), lambda l, r: (l, g_blk, 0))],
        out_specs=[pl.BlockSpec((1, ROW_TILE, d), lambda l, r: (l, r, 0)),
                   pl.BlockSpec((1, KV_WIDTH + LANES, d), lambda l, r: (l, 0, 0))],
        out_shape=(jax.ShapeDtypeStruct((depth, N_MAIN, d), jnp.bfloat16),
                   jax.ShapeDtypeStruct((depth, KV_WIDTH + LANES, d), jnp.bfloat16)),
        compiler_params=pltpu.CompilerParams(dimension_semantics=("arbitrary", "arbitrary")),
        name="cast_projection_weights",
    )(w_t, w_t, w_t)
    return w_qk, w_main, w_vg


def _cast_w_kernel(wm_in, wv_in, wg_in, wm_ref, wvg_ref):
    wm_ref[0] = wm_in[0].astype(jnp.bfloat16)

    @pl.when(pl.program_id(1) == 0)
    def _():
        wvg_ref[0, :KV_WIDTH] = wv_in[0].astype(jnp.bfloat16)
        wvg_ref[0, KV_WIDTH:] = jnp.zeros((LANES, wv_in.shape[2]), jnp.bfloat16)
        wvg_ref[0, KV_WIDTH:KV_WIDTH + N_GATES] = wg_in[0].astype(jnp.bfloat16)


def _rope_tables(t_len, ctx_len):
    rows = t_len // GRID_W
    row = jnp.repeat(jnp.arange(rows), GRID_W).astype(jnp.float32)
    col = jnp.tile(jnp.arange(GRID_W), rows).astype(jnp.float32)
    half = ATT_HEAD_DIM // 2
    inv = ROPE_BASE ** (-jnp.arange(0, half, 2, dtype=jnp.float32) / half)
    ang = jnp.concatenate([row[:, None] * inv, col[:, None] * inv], axis=-1)
    ang = jnp.tile(ang, (1, LANES // half))
    cos = jnp.concatenate([jnp.ones((ctx_len, LANES), jnp.float32), jnp.cos(ang)], axis=0)
    sin = jnp.concatenate([jnp.zeros((ctx_len, LANES), jnp.float32), jnp.sin(ang)], axis=0)
    return cos, sin


def _mod_kernel(c_ref, w_ref, b_ref, o_ref):
    sc = _silu(c_ref[...])
    o_ref[0] = jnp.dot(sc, w_ref[0], preferred_element_type=jnp.float32,
                       precision=lax.Precision.HIGHEST) + b_ref[0]


def _modulation(cc, w_ada, b_ada):
    depth, d, d3 = w_ada.shape
    rows = cc.shape[0]
    nt = d3 // d
    return pl.pallas_call(
        _mod_kernel,
        grid=(depth, nt),
        in_specs=[
            pl.BlockSpec((rows, d), lambda l, n: (0, 0)),
            pl.BlockSpec((1, d, d), lambda l, n: (l, 0, n)),
            pl.BlockSpec((1, 1, d), lambda l, n: (l, 0, n)),
        ],
        out_specs=pl.BlockSpec((1, rows, d), lambda l, n: (l, 0, n)),
        out_shape=jax.ShapeDtypeStruct((depth, rows, d3), jnp.float32),
        compiler_params=pltpu.CompilerParams(dimension_semantics=("arbitrary", "arbitrary")),
        name="adaln_modulation",
    )(cc, w_ada, b_ada.reshape(depth, 1, d3))


def _proj_kernel(*refs, tm, tiles_ctx, tiles_total):
    n_elem = refs[4].shape[0]
    for e0 in range(0, n_elem, PROJ_STACK):
        _proj_group(e0, min(PROJ_STACK, n_elem - e0), *refs, tm=tm, tiles_ctx=tiles_ctx, tiles_total=tiles_total)


def _proj_group(e0, ne, cp_ref, c_ref, cn_ref, xp_ref, x_ref, xn_ref, mod_ref, modc_ref, g_ref,
                wqk_ref, wm_ref, wvg_ref,
                cos_ref, sin_ref, cw_ref, cb_ref,
                ks_ref, gb_ref,
                q_ref, kx_ref, v_ref, az_ref, mqk_ref, mv_ref, moz_ref, gate_ref,
                *, tm, tiles_ctx, tiles_total):
    te = tm + 2 * HALO
    j = pl.program_id(1)
    is_ctx = j < tiles_ctx
    seg_first = jnp.logical_or(j == 0, j == tiles_ctx)
    seg_last = jnp.logical_or(j == tiles_ctx - 1, j == tiles_total - 1)

    ext = []
    for e in range(e0, e0 + ne):
        xe = jnp.concatenate([jnp.where(is_ctx, cp_ref[e], xp_ref[e]), jnp.where(is_ctx, c_ref[e], x_ref[e]),
                              jnp.where(is_ctx, cn_ref[e], xn_ref[e])], axis=0)
        ms = jnp.mean(xe * xe, axis=-1, keepdims=True)
        shift = jnp.where(is_ctx, modc_ref[0, 0:1, :], mod_ref[e, 0:1, :])
        scale = jnp.where(is_ctx, modc_ref[0, 1:2, :], mod_ref[e, 1:2, :])
        gain = g_ref[...] * (1.0 + scale)
        ext.append((xe * lax.rsqrt(ms + EPS) * gain + shift).astype(jnp.bfloat16))
    xn_ext = jnp.concatenate(ext, axis=0)
    xn = jnp.concatenate([x_[HALO:HALO + tm] for x_ in ext], axis=0)

    def proj(rows, w_ref, c0, c1):
        return lax.dot_general(rows, w_ref[c0:c1, :], (((1,), (1,)), ((), ())),
                               preferred_element_type=jnp.float32)

    row = lax.broadcasted_iota(jnp.int32, (te, 1), 0)
    keep = jnp.logical_and(jnp.logical_or(row >= HALO, jnp.logical_not(seg_first)),
                           jnp.logical_or(row < HALO + tm, jnp.logical_not(seg_last)))
    nb = tm // SUBLANES
    sub = lax.broadcasted_iota(jnp.int32, (1, SUBLANES, 1), 1)

    def conv_chunk(c0, width):
        y_all = proj(xn_ext, wm_ref, C_MQK + c0, C_MQK + c0 + width)
        for i, e in enumerate(range(e0, e0 + ne)):
            ye = jnp.where(keep, y_all[i * te:(i + 1) * te], 0.0).reshape(nb + 2, SUBLANES, width)
            acc = jnp.broadcast_to(cb_ref[:, c0:c0 + width].reshape(1, 1, width), (nb, SUBLANES, width))
            for t in range(CONV_K):
                delta = t - CONV_K // 2
                w_t = cw_ref[t:t + 1, c0:c0 + width].reshape(1, 1, width)
                if delta == 0:
                    tap = ye[1:nb + 1]
                else:
                    rot = pltpu.roll(ye, (-delta) % SUBLANES, axis=1)
                    if delta > 0:
                        tap = jnp.where(sub < SUBLANES - delta, rot[1:nb + 1], rot[2:nb + 2])
                    else:
                        tap = jnp.where(sub >= -delta, rot[1:nb + 1], rot[0:nb])
                acc = acc + w_t * tap
            acc = acc.reshape(tm, width)
            mqk_ref[e, :, c0:c0 + width] = (_silu(acc) * ks_ref[:, c0:c0 + width]).astype(jnp.bfloat16)

    def rope_group(grp):
        is_q = grp < ATT_KV_HEADS
        cos = cos_ref[...] * Q_PRESCALE if is_q else cos_ref[...]
        sin = sin_ref[...] * Q_PRESCALE if is_q else sin_ref[...]
        yq = proj(xn, wqk_ref, grp * KV_GROUP_COLS, (grp + 1) * KV_GROUP_COLS)
        dst = q_ref if is_q else kx_ref
        c0 = (grp % ATT_KV_HEADS) * KV_GROUP_COLS
        for i, e in enumerate(range(e0, e0 + ne)):
            first = yq[i * tm:(i + 1) * tm, :LANES]
            second = yq[i * tm:(i + 1) * tm, LANES:]
            dst[e, :, c0:c0 + LANES] = (first * cos - second * sin).astype(jnp.bfloat16)
            dst[e, :, c0 + LANES:c0 + 2 * LANES] = (second * cos + first * sin).astype(jnp.bfloat16)

    def plain(dst_ref, c0, width, off=0):
        y = proj(xn, wm_ref, c0, c0 + width)
        for i, e in enumerate(range(e0, e0 + ne)):
            dst_ref[e, :, off:off + width] = y[i * tm:(i + 1) * tm].astype(dst_ref.dtype)

    others = [functools.partial(rope_group, g) for g in range(2 * ATT_KV_HEADS)]
    others += [functools.partial(plain, az_ref, C_AZ, ATT_WIDTH),
               functools.partial(plain, mv_ref, C_MV, M_WIDTH),
               functools.partial(plain, moz_ref, C_MOZ, M_WIDTH),
               functools.partial(plain, moz_ref, C_MOZ + M_WIDTH, M_WIDTH, M_WIDTH)]
    per_chunk = 2
    cw = 2 * M_WIDTH * per_chunk // len(others)
    for i in range(len(others) // per_chunk):
        conv_chunk(i * cw, cw)
        for other in others[i * per_chunk:(i + 1) * per_chunk]:
            other()
    y = proj(xn, wvg_ref, 0, KV_WIDTH + LANES)
    for i, e in enumerate(range(e0, e0 + ne)):
        v_ref[e] = y[i * tm:(i + 1) * tm, :KV_WIDTH].astype(jnp.bfloat16)
        gate_ref[e] = y[i * tm:(i + 1) * tm, KV_WIDTH:] + gb_ref[...]


def _tile_with_halo_specs(ne, tm, d, n_rows, tile_of):
    hb = tm // HALO
    last = n_rows // HALO - 1
    return (pl.BlockSpec((ne, HALO, d), lambda b, j: (b, jnp.maximum(tile_of(j) * hb - 1, 0), 0)),
            pl.BlockSpec((ne, tm, d), lambda b, j: (b, tile_of(j), 0)),
            pl.BlockSpec((ne, HALO, d), lambda b, j: (b, jnp.minimum((tile_of(j) + 1) * hb, last), 0)))


def _projection(layer, hc, hx, mod, norm_g, w_qk, w_main, w_vg, cos, sin, conv_w, conv_b, kscale, gate_b):
    bsz, ctx_len, d = hc.shape
    s_len = ctx_len + hx.shape[1]
    tm = ROW_TILE
    ne = math.gcd(bsz, PROJ_BATCH)
    tiles_total = s_len // tm
    tiles_ctx = ctx_len // tm
    ctx_blk = mod.shape[1] - 1

    def row_map(b, j):
        return (b, j, 0)

    def per_layer(*block):
        return pl.BlockSpec((None,) + block, lambda b, j: (layer,) + (0,) * len(block))

    kernel = functools.partial(_proj_kernel, tm=tm, tiles_ctx=tiles_ctx, tiles_total=tiles_total)
    out_shapes = (
        jax.ShapeDtypeStruct((bsz, s_len, ATT_WIDTH), jnp.bfloat16),
        jax.ShapeDtypeStruct((bsz, s_len, ATT_WIDTH), jnp.bfloat16),
        jax.ShapeDtypeStruct((bsz, s_len, KV_WIDTH), jnp.bfloat16),
        jax.ShapeDtypeStruct((bsz, s_len, ATT_WIDTH), jnp.bfloat16),
        jax.ShapeDtypeStruct((bsz, s_len, 2 * M_WIDTH), jnp.bfloat16),
        jax.ShapeDtypeStruct((bsz, s_len, M_WIDTH), jnp.bfloat16),
        jax.ShapeDtypeStruct((bsz, s_len, 2 * M_WIDTH), jnp.bfloat16),
        jax.ShapeDtypeStruct((bsz, s_len, LANES), jnp.float32),
    )
    return pl.pallas_call(
        kernel,
        grid=(bsz // ne, tiles_total),
        in_specs=[
            *_tile_with_halo_specs(ne, tm, d, ctx_len, lambda j: jnp.minimum(j, tiles_ctx - 1)),
            *_tile_with_halo_specs(ne, tm, d, s_len - ctx_len, lambda j: jnp.maximum(j - tiles_ctx, 0)),
            pl.BlockSpec((None, ne, 3, d), lambda b, j: (layer, b, 0, 0)),
            pl.BlockSpec((None, 1, 3, d), lambda b, j: (layer, ctx_blk, 0, 0)),
            per_layer(1, d),
            pl.BlockSpec((None, 2 * ATT_WIDTH, d), lambda b, j: (layer, 0, 0), pipeline_mode=pl.Buffered(1)),
            pl.BlockSpec((None, N_MAIN, d), lambda b, j: (layer, 0, 0), pipeline_mode=pl.Buffered(1)),
            pl.BlockSpec((None, KV_WIDTH + LANES, d), lambda b, j: (layer, 0, 0), pipeline_mode=pl.Buffered(1)),
            pl.BlockSpec((tm, LANES), lambda b, j: (j, 0)),
            pl.BlockSpec((tm, LANES), lambda b, j: (j, 0)),
            per_layer(CONV_K, 2 * M_WIDTH),
            per_layer(1, 2 * M_WIDTH),
            pl.BlockSpec((1, 2 * M_WIDTH), lambda b, j: (0, 0)),
            per_layer(1, LANES),
        ],
        out_specs=[
            pl.BlockSpec((ne, tm, ATT_WIDTH), row_map),
            pl.BlockSpec((ne, tm, ATT_WIDTH), row_map),
            pl.BlockSpec((ne, tm, KV_WIDTH), row_map),
            pl.BlockSpec((ne, tm, ATT_WIDTH), row_map),
            pl.BlockSpec((ne, tm, 2 * M_WIDTH), row_map),
            pl.BlockSpec((ne, tm, M_WIDTH), row_map),
            pl.BlockSpec((ne, tm, 2 * M_WIDTH), row_map),
            pl.BlockSpec((ne, tm, LANES), row_map),
        ],
        out_shape=out_shapes,
        compiler_params=pltpu.CompilerParams(
            dimension_semantics=("arbitrary", "arbitrary"), vmem_limit_bytes=VMEM_LIMIT),
        name="norm_mod_projection",
    )(hc, hc, hc, hx, hx, hx, mod, mod, norm_g, w_qk, w_main, w_vg, cos, sin, conv_w, conv_b, kscale, gate_b)


def _attn_kernel(sink_ref, q_ref, kp_ref, kc_ref, kn_ref, kctx_ref, vp_ref, vc_ref, vn_ref, vctx_ref,
                 az_ref, o_ref, *, layer, blk0, ctx_blocks, total_blocks):
    qb = pl.program_id(1) + blk0
    is_ctx = qb < ctx_blocks
    n = qb - ctx_blocks
    lat_blocks = total_blocks - ctx_blocks
    ctx_len = ctx_blocks * BLOCK
    win = 3 * BLOCK
    neg = -jnp.inf

    i = lax.broadcasted_iota(jnp.int32, (BLOCK, win), 0)
    jx = lax.broadcasted_iota(jnp.int32, (BLOCK, win), 1)
    lo = jnp.where(n == 0, BLOCK, 0)
    hi = jnp.where(n == lat_blocks - 1, 2 * BLOCK, win)
    valid = (jx >= i) & (jx <= i + 2 * WINDOW) & (jx >= lo) & (jx < hi) & jnp.logical_not(is_ctx)
    bias = jnp.concatenate([jnp.where(valid, 0.0, neg)] * ATT_GROUP, axis=0)

    lane_v = lax.broadcasted_iota(jnp.int32, (1, LANES), 1)
    low_half = lane_v < ATT_HEAD_DIM
    lane_q = lax.broadcasted_iota(jnp.int32, (1, KV_GROUP_COLS), 1)
    head_of_qlane = (lane_q % LANES) // (LANES // ATT_GROUP)
    row_head = lax.broadcasted_iota(jnp.int32, (ATT_GROUP * BLOCK, 1), 0) // BLOCK
    n_keys = win + ctx_len
    ones = jnp.ones((n_keys, LANES), jnp.bfloat16)

    for e_i in range(q_ref.shape[0]):
        k_all = jnp.concatenate([kp_ref[e_i], kc_ref[e_i], kn_ref[e_i], kctx_ref[e_i]], axis=0)
        v_all = jnp.concatenate([vp_ref[e_i], vc_ref[e_i], vn_ref[e_i], vctx_ref[e_i]],
                                axis=0).astype(jnp.float32)
        v_rot = pltpu.roll(v_all, ATT_HEAD_DIM, axis=1)
        for kvh in range(ATT_KV_HEADS):
            c0 = kvh * KV_GROUP_COLS
            qg = q_ref[e_i, :, c0:c0 + KV_GROUP_COLS]
            qm = jnp.concatenate(
                [jnp.where(head_of_qlane == g, qg, jnp.zeros_like(qg)) for g in range(ATT_GROUP)], axis=0)
            kx = k_all[:, c0:c0 + KV_GROUP_COLS]
            s = lax.dot_general(qm, kx, (((1,), (1,)), ((), ())), preferred_element_type=jnp.float32)
            s_win = s[:, :win] + bias
            s_ctx = s[:, win:]
            sink = jnp.zeros((ATT_GROUP * BLOCK, 1), jnp.float32)
            for g in range(ATT_GROUP):
                sink = jnp.where(row_head == g, sink_ref[layer, kvh * ATT_GROUP + g] * LOG2_E, sink)
            m = jnp.maximum(jnp.maximum(jnp.max(s_win, axis=-1, keepdims=True),
                                        jnp.max(s_ctx, axis=-1, keepdims=True)), sink)
            e = jnp.concatenate([jnp.exp2(s_win - m), jnp.exp2(s_ctx - m)], axis=1).astype(jnp.bfloat16)
            own_half = low_half if kvh == 0 else jnp.logical_not(low_half)
            v_two = jnp.where(own_half, v_all, v_rot).astype(jnp.bfloat16)
            res = jnp.dot(e, jnp.concatenate([v_two, ones], axis=1),
                          preferred_element_type=jnp.float32)
            o = res[:, :LANES] / (res[:, LANES:] + jnp.exp2(sink - m))
            az = az_ref[e_i, :, c0:c0 + KV_GROUP_COLS].astype(jnp.float32)
            for pair in range(ATT_GROUP // 2):
                both = jnp.where(low_half, o[2 * pair * BLOCK:(2 * pair + 1) * BLOCK],
                                 o[(2 * pair + 1) * BLOCK:(2 * pair + 2) * BLOCK])
                lo = c0 + pair * LANES
                o_ref[e_i, :, lo:lo + LANES] = (
                    both * _silu(az[:, pair * LANES:(pair + 1) * LANES])).astype(jnp.bfloat16)


def _attention(layer, sink, q, kx, v, az, *, ctx_len, skip_ctx):
    bsz, s_len, _ = q.shape
    ctx_blocks = ctx_len // BLOCK
    total_blocks = s_len // BLOCK
    blk0 = ctx_blocks if skip_ctx else 0
    lo, hi = ctx_blocks, total_blocks - 1
    nb = math.gcd(bsz, BATCH_PER_STEP)

    def at(off):
        return lambda b, j: (b, jnp.clip(j + blk0 + off, lo, hi), 0)

    def cur(b, j):
        return (b, j + blk0, 0)

    def ctx(b, j):
        return (b, 0, 0)

    kernel = functools.partial(_attn_kernel, layer=layer, blk0=blk0, ctx_blocks=ctx_blocks, total_blocks=total_blocks)
    return pl.pallas_call(
        kernel,
        grid=(bsz // nb, total_blocks - blk0),
        in_specs=[
            pl.BlockSpec(memory_space=pltpu.SMEM),
            pl.BlockSpec((nb, BLOCK, ATT_WIDTH), cur),
            pl.BlockSpec((nb, BLOCK, ATT_WIDTH), at(-1)),
            pl.BlockSpec((nb, BLOCK, ATT_WIDTH), at(0)),
            pl.BlockSpec((nb, BLOCK, ATT_WIDTH), at(1)),
            pl.BlockSpec((nb, ctx_len, ATT_WIDTH), ctx),
            pl.BlockSpec((nb, BLOCK, KV_WIDTH), at(-1)),
            pl.BlockSpec((nb, BLOCK, KV_WIDTH), at(0)),
            pl.BlockSpec((nb, BLOCK, KV_WIDTH), at(1)),
            pl.BlockSpec((nb, ctx_len, KV_WIDTH), ctx),
            pl.BlockSpec((nb, BLOCK, ATT_WIDTH), cur),
        ],
        out_specs=pl.BlockSpec((nb, BLOCK, ATT_WIDTH), lambda b, j: (b, j, 0)),
        out_shape=jax.ShapeDtypeStruct((bsz, s_len - blk0 * BLOCK, ATT_WIDTH), jnp.bfloat16),
        compiler_params=pltpu.CompilerParams(
            dimension_semantics=("arbitrary", "arbitrary"), vmem_limit_bytes=VMEM_LIMIT),
        name="window_ctx_attention",
    )(sink, q, kx, kx, kx, kx, v, v, v, v, az)


STATE_ROWS = M_HEAD_DIM + SUBLANES


def _split2(x):
    hi = x.astype(jnp.bfloat16)
    lo = (x - hi.astype(jnp.float32)).astype(jnp.bfloat16)
    return hi, lo


def _mlstm_kernel(qkf_ref, vf_ref, gf_ref, qkb_ref, vb_ref, gb_ref, hf_ref, hb_ref, c_ref, m_ref):
    @pl.when(pl.program_id(1) == 0)
    def _():
        c_ref[...] = jnp.zeros_like(c_ref)
        m_ref[...] = jnp.full_like(m_ref, -jnp.inf)

    rows = lax.broadcasted_iota(jnp.int32, (CHUNK, CHUNK), 0)
    cols = lax.broadcasted_iota(jnp.int32, (CHUNK, CHUNK), 1)
    le = rows <= cols
    ge = rows >= cols
    tri_le = jnp.where(le, 1.0, 0.0).astype(jnp.bfloat16)
    tri_ge = jnp.where(ge, 1.0, 0.0).astype(jnp.bfloat16)

    for e in range(qkf_ref.shape[0]):
        _mlstm_element(e, qkf_ref, vf_ref, gf_ref, qkb_ref, vb_ref, gb_ref, hf_ref, hb_ref, c_ref, m_ref,
                       le, ge, tri_le, tri_ge)


def _mlstm_element(e, qkf_ref, vf_ref, gf_ref, qkb_ref, vb_ref, gb_ref, hf_ref, hb_ref, c_ref, m_ref,
                   le, ge, tri_le, tri_ge):
    nst = N_DIR * M_HEADS
    gates = (gf_ref[e], gb_ref[e])
    gates_t = (gates[0].T, gates[1].T)

    i8 = jnp.concatenate([gates_t[0][0:M_HEADS], gates_t[1][M_HEADS:nst]], axis=0)
    lf8 = _log_sigmoid(jnp.concatenate([gates_t[0][nst:nst + M_HEADS],
                                        gates_t[1][nst + M_HEADS:2 * nst]], axis=0))
    pre8 = sum(jnp.dot(p, tri_le, preferred_element_type=jnp.float32) for p in _split2(lf8))
    tot8 = pre8[:, CHUNK - 1:CHUNK]
    is_fwd = lax.broadcasted_iota(jnp.int32, (nst, 1), 0) < M_HEADS
    b8 = jnp.where(is_fwd, pre8, tot8 - pre8 + lf8)
    r8 = i8 - b8
    g8 = tot8 + r8
    m_prev = m_ref[e]
    m_new = jnp.maximum(tot8 + m_prev, jnp.max(g8, axis=1, keepdims=True))
    a8 = jnp.exp(tot8 + m_prev - m_new)
    w8 = jnp.exp(g8 - m_new)
    inter8 = b8 + m_prev
    m_ref[e] = m_new

    ones_row = jnp.where(lax.broadcasted_iota(jnp.int32, (SUBLANES, CHUNK), 0) == 0, 1.0, 0.0)

    lf_c2 = _log_sigmoid(jnp.concatenate(gates, axis=1))
    pre_c2 = sum(jnp.dot(tri_ge, p, preferred_element_type=jnp.float32) for p in _split2(lf_c2))

    for d, (qk_ref, v_ref, h_ref) in enumerate(((qkf_ref, vf_ref, hf_ref), (qkb_ref, vb_ref, hb_ref))):
        valid_t = le if d == 0 else ge
        lf_c = lf_c2[:, d * LANES:(d + 1) * LANES]
        pre_c = pre_c2[:, d * LANES:(d + 1) * LANES]
        b_c = pre_c if d == 0 else pre_c[CHUNK - 1:CHUNK, :] - pre_c + lf_c
        r_c = pltpu.roll(gates[d], nst, axis=1) - b_c

        heads = range(M_HEADS)
        rs = [d * M_HEADS + hd for hd in heads]
        qs = [qk_ref[e, :, hd * M_HEAD_DIM:(hd + 1) * M_HEAD_DIM] for hd in heads]
        ks = [qk_ref[e, :, M_WIDTH + hd * M_HEAD_DIM:M_WIDTH + (hd + 1) * M_HEAD_DIM] for hd in heads]
        vts = [jnp.concatenate([v_ref[e, :, hd * M_HEAD_DIM:(hd + 1) * M_HEAD_DIM].astype(jnp.float32).T,
                                ones_row], axis=0) for hd in heads]
        cs = [c_ref[e, r] for r in rs]

        qts = [q.astype(jnp.float32).T for q in qs]
        st = [jnp.dot(k, qt.astype(jnp.bfloat16), preferred_element_type=jnp.float32)
              for k, qt in zip(ks, qts)]
        dt = [jnp.where(valid_t, r_c[:, nst + r:nst + r + 1] + b8[r:r + 1, :], -jnp.inf) for r in rs]
        m_t = [jnp.maximum(inter8[r:r + 1, :], jnp.max(x, axis=0, keepdims=True)) for r, x in zip(rs, dt)]
        w_i = [jnp.exp(inter8[r:r + 1, :] - m) for r, m in zip(rs, m_t)]
        pt = [s_ * jnp.exp(x - m) for s_, x, m in zip(st, dt, m_t)]
        both = [jnp.dot(jnp.concatenate([vt, c], axis=1).astype(jnp.bfloat16),
                        jnp.concatenate([p, qt * w], axis=0).astype(jnp.bfloat16),
                        preferred_element_type=jnp.float32)
                for vt, c, p, qt, w in zip(vts, cs, pt, qts, w_i)]
        for hd in heads:
            den = both[hd][M_HEAD_DIM:M_HEAD_DIM + 1]
            ht = both[hd][:M_HEAD_DIM] / jnp.maximum(jnp.abs(den), jnp.exp(-m_t[hd]))
            h_ref[e, :, hd * M_HEAD_DIM:(hd + 1) * M_HEAD_DIM] = ht.T.astype(h_ref.dtype)
        for hd, r in zip(heads, rs):
            vw = (vts[hd] * w8[r:r + 1, :]).astype(jnp.bfloat16)
            c_ref[e, r] = a8[r:r + 1, :] * cs[hd] + jnp.dot(vw, ks[hd], preferred_element_type=jnp.float32)


def _mlstm(mqk, mv, gates, *, ctx_len):
    bsz, s_len, _ = mqk.shape
    nc = s_len // CHUNK
    ncc = ctx_len // CHUNK

    def fwd(b, j):
        return (b, j, 0)

    def bwd(b, j):
        return (b, jnp.where(j < ncc, ncc - 1 - j, nc - 1 + ncc - j), 0)

    nstate = N_DIR * M_HEADS
    nb = math.gcd(bsz, BATCH_PER_STEP)
    return pl.pallas_call(
        _mlstm_kernel,
        grid=(bsz // nb, nc),
        in_specs=[
            pl.BlockSpec((nb, CHUNK, 2 * M_WIDTH), fwd),
            pl.BlockSpec((nb, CHUNK, M_WIDTH), fwd),
            pl.BlockSpec((nb, CHUNK, LANES), fwd),
            pl.BlockSpec((nb, CHUNK, 2 * M_WIDTH), bwd),
            pl.BlockSpec((nb, CHUNK, M_WIDTH), bwd),
            pl.BlockSpec((nb, CHUNK, LANES), bwd),
        ],
        out_specs=[
            pl.BlockSpec((nb, CHUNK, M_WIDTH), fwd),
            pl.BlockSpec((nb, CHUNK, M_WIDTH), bwd),
        ],
        out_shape=(jax.ShapeDtypeStruct((bsz, s_len, M_WIDTH), jnp.bfloat16),
                   jax.ShapeDtypeStruct((bsz, s_len, M_WIDTH), jnp.bfloat16)),
        scratch_shapes=[
            pltpu.VMEM((nb, nstate, STATE_ROWS, M_HEAD_DIM), jnp.float32),
            pltpu.VMEM((nb, nstate, LANES), jnp.float32),
        ],
        compiler_params=pltpu.CompilerParams(
            dimension_semantics=("arbitrary", "arbitrary"), vmem_limit_bytes=VMEM_LIMIT),
        name="mlstm_bidirectional_scan",
    )(mqk, mv, gates, mqk, mv, gates)


def _out_kernel(x_ref, att_ref, hf_ref, hb_ref, moz_ref, hg_ref, w_ref, mod_ref, *rest, final):
    if final:
        fg_ref, o_ref = rest
    else:
        (o_ref,) = rest
    ne, tm, _ = x_ref.shape
    mxs = []
    for e in range(ne):
        mo = moz_ref[e, :, :M_WIDTH].astype(jnp.float32)
        mz = moz_ref[e, :, M_WIDTH:].astype(jnp.float32)
        hs = _sigmoid(mo) * (hf_ref[e].astype(jnp.float32) + hb_ref[e].astype(jnp.float32))
        parts = []
        for hd in range(M_HEADS):
            hh = hs[:, hd * M_HEAD_DIM:(hd + 1) * M_HEAD_DIM]
            ms = jnp.mean(hh * hh, axis=-1, keepdims=True)
            parts.append(hh * lax.rsqrt(ms + EPS))
        mxs.append((jnp.concatenate(parts, axis=1) * hg_ref[...] * _silu(mz)).astype(jnp.bfloat16))
    att = jnp.concatenate([att_ref[e] for e in range(ne)], axis=0)
    out = (jnp.dot(att, w_ref[:ATT_WIDTH, :], preferred_element_type=jnp.float32)
           + jnp.dot(jnp.concatenate(mxs, axis=0), w_ref[ATT_WIDTH:, :], preferred_element_type=jnp.float32))
    for e in range(ne):
        gate = mod_ref[e if mod_ref.shape[0] > 1 else 0, 2:3, :]
        xnew = x_ref[e] + gate * out[e * tm:(e + 1) * tm]
        if final:
            ms = jnp.mean(xnew * xnew, axis=-1, keepdims=True)
            o_ref[e] = xnew * lax.rsqrt(ms + EPS) * fg_ref[...]
        else:
            o_ref[e] = xnew


def _output(layer, hseg, att, hf, hb, moz, head_g, w_out, mod, final_g, *, row0, att_row0, is_ctx, final):
    bsz, seg_len, d = hseg.shape
    tm = ROW_TILE
    ne = math.gcd(bsz, PROJ_BATCH)
    t0 = row0 // tm
    a0 = att_row0 // tm

    def seg_map(b, j):
        return (b, j, 0)

    def row_map(b, j):
        return (b, j + t0, 0)

    if is_ctx:
        mod_spec = pl.BlockSpec((None, 1, 3, d), lambda b, j: (layer, mod.shape[1] - 1, 0, 0))
    else:
        mod_spec = pl.BlockSpec((None, ne, 3, d), lambda b, j: (layer, b, 0, 0))
    in_specs = [
        pl.BlockSpec((ne, tm, d), seg_map),
        pl.BlockSpec((ne, tm, ATT_WIDTH), lambda b, j: (b, j + a0, 0)),
        pl.BlockSpec((ne, tm, M_WIDTH), row_map),
        pl.BlockSpec((ne, tm, M_WIDTH), row_map),
        pl.BlockSpec((ne, tm, 2 * M_WIDTH), row_map),
        pl.BlockSpec((None, 1, M_WIDTH), lambda b, j: (layer, 0, 0)),
        pl.BlockSpec((None, d, d), lambda b, j: (layer, 0, 0), pipeline_mode=pl.Buffered(1)),
        mod_spec,
    ]
    args = [hseg, att, hf, hb, moz, head_g, w_out, mod]
    if final:
        in_specs.append(pl.BlockSpec((1, d), lambda b, j: (0, 0)))
        args.append(final_g)
    return pl.pallas_call(
        functools.partial(_out_kernel, final=final),
        grid=(bsz // ne, seg_len // tm),
        in_specs=in_specs,
        out_specs=pl.BlockSpec((ne, tm, d), seg_map),
        out_shape=jax.ShapeDtypeStruct(hseg.shape, jnp.float32),
        compiler_params=pltpu.CompilerParams(
            dimension_semantics=("arbitrary", "arbitrary"), vmem_limit_bytes=VMEM_LIMIT),
        name="gate_norm_out_projection",
    )(*args)


def kernel(x, c, ctx, c_ctx, w_ada, b_ada, norm_g, w_in, conv_w, conv_b, gate_b, sink, head_g, w_out, final_g):
    bsz, t_len, d = x.shape
    ctx_len = ctx.shape[1]
    depth = w_in.shape[0]
    assert d == D_MODEL and t_len % ROW_TILE == 0 and ctx_len % ROW_TILE == 0 and t_len % GRID_W == 0

    hc, hx = ctx, x
    mod_rows = -(-(bsz + 1) // SUBLANES) * SUBLANES
    cc = jnp.zeros((mod_rows, d), jnp.float32).at[:bsz].set(c).at[mod_rows - 1].set(c_ctx)
    mod = _modulation(cc, w_ada, b_ada).reshape(depth, mod_rows, 3, d)

    w_qk, w_main, w_vg = _split_w_in(w_in)
    w_o = w_out.astype(jnp.bfloat16)
    cos, sin = _rope_tables(t_len, ctx_len)
    kscale = jnp.concatenate([jnp.ones((1, M_WIDTH), jnp.float32),
                              jnp.full((1, M_WIDTH), M_HEAD_DIM ** -0.5, jnp.float32)], axis=1)
    gate_bias = jnp.pad(gate_b, ((0, 0), (0, LANES - N_GATES)))[:, None, :]
    norm_g3, conv_b3, head_g3 = norm_g[:, None, :], conv_b[:, None, :], head_g[:, None, :]

    for l in range(depth):
        last = l == depth - 1
        q, kx, v, az, mqk, mv, moz, gates = _projection(
            l, hc, hx, mod, norm_g3, w_qk, w_main, w_vg, cos, sin, conv_w, conv_b3, kscale, gate_bias)
        att = _attention(l, sink, q, kx, v, az, ctx_len=ctx_len, skip_ctx=last)
        hf, hb = _mlstm(mqk, mv, gates, ctx_len=ctx_len)
        out_args = (att, hf, hb, moz, head_g3, w_o, mod, final_g[None])
        if last:
            return _output(l, hx, *out_args, row0=ctx_len, att_row0=0, is_ctx=False, final=True)
        hc, hx = (_output(l, hc, *out_args, row0=0, att_row0=0, is_ctx=True, final=False),
                  _output(l, hx, *out_args, row0=ctx_len, att_row0=ctx_len, is_ctx=False, final=False))
```

```python
import functools
import math

import jax
import jax.numpy as jnp
from jax import lax
from jax.experimental import pallas as pl
from jax.experimental.pallas import tpu as pltpu

D_MODEL = 1024
GRID_W = 64
ATT_HEADS = 8
ATT_KV_HEADS = 2
ATT_GROUP = ATT_HEADS // ATT_KV_HEADS
ATT_HEAD_DIM = 64
ATT_WIDTH = ATT_HEADS * ATT_HEAD_DIM
KV_WIDTH = ATT_KV_HEADS * ATT_HEAD_DIM
WINDOW = 128
BLOCK = 128
ROPE_BASE = 10000.0
M_HEADS = 4
M_HEAD_DIM = 128
M_WIDTH = M_HEADS * M_HEAD_DIM
CONV_K = 5
CHUNK = 128
N_DIR = 2
N_GATES = 2 * N_DIR * M_HEADS
EPS = 1e-6

LANES = 128
SUBLANES = 8
HALO = SUBLANES
ROW_TILE = 256
BATCH_PER_STEP = 8
PROJ_BATCH = 4
PROJ_STACK = 1
VMEM_LIMIT = 56 * 1024 * 1024

W_IN_MAIN_START = ATT_WIDTH + 2 * KV_WIDTH
C_AZ = 0
C_MQK = C_AZ + ATT_WIDTH
C_MV = C_MQK + 2 * M_WIDTH
C_MOZ = C_MV + M_WIDTH
N_MAIN = C_MOZ + 2 * M_WIDTH
KV_GROUP_COLS = ATT_GROUP * ATT_HEAD_DIM
LOG2_E = 1.4426950408889634
Q_PRESCALE = ATT_HEAD_DIM ** -0.5 * LOG2_E


def _sigmoid(x):
    return 1.0 / (1.0 + jnp.exp(-x))


def _silu(x):
    return x * _sigmoid(x)


def _log_sigmoid(x):
    return jnp.minimum(x, 0.0) - jnp.log(1.0 + jnp.exp(-jnp.abs(x)))


def _split_w_in(w_in):
    depth, d, n_in = w_in.shape
    w_t = jnp.swapaxes(w_in, 1, 2)
    quarter = ATT_HEAD_DIM // 4
    k0 = ATT_WIDTH
    v0 = k0 + KV_WIDTH
    wq = w_t[:, :k0].reshape(depth, ATT_KV_HEADS, ATT_GROUP, 2, 2, quarter, d)
    wq = wq.transpose(0, 1, 4, 2, 3, 5, 6).reshape(depth, ATT_WIDTH, d)
    wk = w_t[:, k0:v0].reshape(depth, ATT_KV_HEADS, 1, 2, 2, quarter, d)
    wk = jnp.broadcast_to(wk.transpose(0, 1, 4, 2, 3, 5, 6),
                          (depth, ATT_KV_HEADS, 2, ATT_GROUP, 2, quarter, d)).reshape(depth, ATT_WIDTH, d)
    w_qk = jnp.swapaxes(jnp.concatenate([wq, wk], axis=1), 1, 2).astype(jnp.bfloat16)
    main_blk0 = W_IN_MAIN_START // ROW_TILE
    v_blk = (W_IN_MAIN_START - KV_WIDTH) // KV_WIDTH
    g_blk = (W_IN_MAIN_START + N_MAIN) // N_GATES
    w_main, w_vg = pl.pallas_call(
        _cast_w_kernel,
        grid=(depth, N_MAIN // ROW_TILE),
        in_specs=[pl.BlockSpec((1, ROW_TILE, d), lambda l, r: (l, r + main_blk0, 0)),
                  pl.BlockSpec((1, KV_WIDTH, d), lambda l, r: (l, v_blk, 0)),
                  pl.BlockSpec((1, N_GATES, d), lambda l, r: (l, g_blk, 0))],
        out_specs=[pl.BlockSpec((1, d, ROW_TILE), lambda l, r: (l, 0, r)),
                   pl.BlockSpec((1, d, KV_WIDTH + LANES), lambda l, r: (l, 0, 0))],
        out_shape=(jax.ShapeDtypeStruct((depth, d, N_MAIN), jnp.bfloat16),
                   jax.ShapeDtypeStruct((depth, d, KV_WIDTH + LANES), jnp.bfloat16)),
        compiler_params=pltpu.CompilerParams(dimension_semantics=("arbitrary", "arbitrary")),
        name="cast_projection_weights",
    )(w_t, w_t, w_t)
    return w_qk, w_main, w_vg


def _cast_w_kernel(wm_in, wv_in, wg_in, wm_ref, wvg_ref):
    wm_ref[0] = wm_in[0].T.astype(jnp.bfloat16)

    @pl.when(pl.program_id(1) == 0)
    def _():
        gates = jnp.concatenate([wg_in[0], jnp.zeros((LANES - N_GATES, wg_in.shape[2]), jnp.float32)], axis=0)
        wvg_ref[0, :, :KV_WIDTH] = wv_in[0].T.astype(jnp.bfloat16)
        wvg_ref[0, :, KV_WIDTH:] = gates.T.astype(jnp.bfloat16)


def _rope_tables(t_len, ctx_len):
    rows = t_len // GRID_W
    row = jnp.repeat(jnp.arange(rows), GRID_W).astype(jnp.float32)
    col = jnp.tile(jnp.arange(GRID_W), rows).astype(jnp.float32)
    half = ATT_HEAD_DIM // 2
    inv = ROPE_BASE ** (-jnp.arange(0, half, 2, dtype=jnp.float32) / half)
    ang = jnp.concatenate([row[:, None] * inv, col[:, None] * inv], axis=-1)
    ang = jnp.tile(ang, (1, LANES // half))
    cos = jnp.concatenate([jnp.ones((ctx_len, LANES), jnp.float32), jnp.cos(ang)], axis=0)
    sin = jnp.concatenate([jnp.zeros((ctx_len, LANES), jnp.float32), jnp.sin(ang)], axis=0)
    return cos, sin


def _mod_kernel(c_ref, w_ref, b_ref, o_ref):
    sc = _silu(c_ref[...])
    o_ref[0] = jnp.dot(sc, w_ref[0], preferred_element_type=jnp.float32,
                       precision=lax.Precision.HIGHEST) + b_ref[0]


def _modulation(cc, w_ada, b_ada):
    depth, d, d3 = w_ada.shape
    rows = cc.shape[0]
    nt = d3 // d
    return pl.pallas_call(
        _mod_kernel,
        grid=(depth, nt),
        in_specs=[
            pl.BlockSpec((rows, d), lambda l, n: (0, 0)),
            pl.BlockSpec((1, d, d), lambda l, n: (l, 0, n)),
            pl.BlockSpec((1, 1, d), lambda l, n: (l, 0, n)),
        ],
        out_specs=pl.BlockSpec((1, rows, d), lambda l, n: (l, 0, n)),
        out_shape=jax.ShapeDtypeStruct((depth, rows, d3), jnp.float32),
        compiler_params=pltpu.CompilerParams(dimension_semantics=("arbitrary", "arbitrary")),
        name="adaln_modulation",
    )(cc, w_ada, b_ada.reshape(depth, 1, d3))


def _proj_kernel(*refs, tm, tiles_ctx, tiles_total):
    n_elem = refs[4].shape[0]
    for e0 in range(0, n_elem, PROJ_STACK):
        _proj_group(e0, min(PROJ_STACK, n_elem - e0), *refs, tm=tm, tiles_ctx=tiles_ctx, tiles_total=tiles_total)


def _proj_group(e0, ne, cp_ref, c_ref, cn_ref, xp_ref, x_ref, xn_ref, mod_ref, modc_ref, g_ref,
                wqk_ref, wm_ref, wvg_ref,
                cos_ref, sin_ref, cw_ref, cb_ref,
                ks_ref, gb_ref,
                q_ref, kx_ref, v_ref, az_ref, mqk_ref, mv_ref, moz_ref, gate_ref,
                *, tm, tiles_ctx, tiles_total):
    te = tm + 2 * HALO
    j = pl.program_id(1)
    is_ctx = j < tiles_ctx
    seg_first = jnp.logical_or(j == 0, j == tiles_ctx)
    seg_last = jnp.logical_or(j == tiles_ctx - 1, j == tiles_total - 1)

    ext = []
    for e in range(e0, e0 + ne):
        xe = jnp.concatenate([jnp.where(is_ctx, cp_ref[e], xp_ref[e]), jnp.where(is_ctx, c_ref[e], x_ref[e]),
                              jnp.where(is_ctx, cn_ref[e], xn_ref[e])], axis=0)
        ms = jnp.mean(xe * xe, axis=-1, keepdims=True)
        shift = jnp.where(is_ctx, modc_ref[0, 0:1, :], mod_ref[e, 0:1, :])
        scale = jnp.where(is_ctx, modc_ref[0, 1:2, :], mod_ref[e, 1:2, :])
        gain = g_ref[...] * (1.0 + scale)
        ext.append((xe * lax.rsqrt(ms + EPS) * gain + shift).astype(jnp.bfloat16))
    xn_ext = jnp.concatenate(ext, axis=0)
    xn = jnp.concatenate([x_[HALO:HALO + tm] for x_ in ext], axis=0)

    def proj(rows, w_ref, c0, c1):
        return jnp.dot(rows, w_ref[:, c0:c1], preferred_element_type=jnp.float32)

    row = lax.broadcasted_iota(jnp.int32, (te, 1), 0)
    keep = jnp.logical_and(jnp.logical_or(row >= HALO, jnp.logical_not(seg_first)),
                           jnp.logical_or(row < HALO + tm, jnp.logical_not(seg_last)))
    nb = tm // SUBLANES
    sub = lax.broadcasted_iota(jnp.int32, (1, SUBLANES, 1), 1)

    def conv_chunk(c0, width):
        y_all = proj(xn_ext, wm_ref, C_MQK + c0, C_MQK + c0 + width)
        for i, e in enumerate(range(e0, e0 + ne)):
            ye = jnp.where(keep, y_all[i * te:(i + 1) * te], 0.0).reshape(nb + 2, SUBLANES, width)
            acc = jnp.broadcast_to(cb_ref[:, c0:c0 + width].reshape(1, 1, width), (nb, SUBLANES, width))
            for t in range(CONV_K):
                delta = t - CONV_K // 2
                w_t = cw_ref[t:t + 1, c0:c0 + width].reshape(1, 1, width)
                if delta == 0:
                    tap = ye[1:nb + 1]
                else:
                    rot = pltpu.roll(ye, (-delta) % SUBLANES, axis=1)
                    if delta > 0:
                        tap = jnp.where(sub < SUBLANES - delta, rot[1:nb + 1], rot[2:nb + 2])
                    else:
                        tap = jnp.where(sub >= -delta, rot[1:nb + 1], rot[0:nb])
                acc = acc + w_t * tap
            acc = acc.reshape(tm, width)
            mqk_ref[e, :, c0:c0 + width] = (_silu(acc) * ks_ref[:, c0:c0 + width]).astype(jnp.bfloat16)

    def rope_group(grp):
        is_q = grp < ATT_KV_HEADS
        cos = cos_ref[...] * Q_PRESCALE if is_q else cos_ref[...]
        sin = sin_ref[...] * Q_PRESCALE if is_q else sin_ref[...]
        yq = proj(xn, wqk_ref, grp * KV_GROUP_COLS, (grp + 1) * KV_GROUP_COLS)
        dst = q_ref if is_q else kx_ref
        c0 = (grp % ATT_KV_HEADS) * KV_GROUP_COLS
        for i, e in enumerate(range(e0, e0 + ne)):
            first = yq[i * tm:(i + 1) * tm, :LANES]
            second = yq[i * tm:(i + 1) * tm, LANES:]
            dst[e, :, c0:c0 + LANES] = (first * cos - second * sin).astype(jnp.bfloat16)
            dst[e, :, c0 + LANES:c0 + 2 * LANES] = (second * cos + first * sin).astype(jnp.bfloat16)

    def plain(dst_ref, c0, width, off=0):
        y = proj(xn, wm_ref, c0, c0 + width)
        for i, e in enumerate(range(e0, e0 + ne)):
            dst_ref[e, :, off:off + width] = y[i * tm:(i + 1) * tm].astype(dst_ref.dtype)

    others = [functools.partial(rope_group, g) for g in range(2 * ATT_KV_HEADS)]
    others += [functools.partial(plain, az_ref, C_AZ, ATT_WIDTH),
               functools.partial(plain, mv_ref, C_MV, M_WIDTH),
               functools.partial(plain, moz_ref, C_MOZ, M_WIDTH),
               functools.partial(plain, moz_ref, C_MOZ + M_WIDTH, M_WIDTH, M_WIDTH)]
    per_chunk = 2
    cw = 2 * M_WIDTH * per_chunk // len(others)
    for i in range(len(others) // per_chunk):
        conv_chunk(i * cw, cw)
        for other in others[i * per_chunk:(i + 1) * per_chunk]:
            other()
    y = proj(xn, wvg_ref, 0, KV_WIDTH + LANES)
    for i, e in enumerate(range(e0, e0 + ne)):
        v_ref[e] = y[i * tm:(i + 1) * tm, :KV_WIDTH].astype(jnp.bfloat16)
        gate_ref[e] = y[i * tm:(i + 1) * tm, KV_WIDTH:] + gb_ref[...]


def _tile_with_halo_specs(ne, tm, d, n_rows, tile_of):
    hb = tm // HALO
    last = n_rows // HALO - 1
    return (pl.BlockSpec((ne, HALO, d), lambda b, j: (b, jnp.maximum(tile_of(j) * hb - 1, 0), 0)),
            pl.BlockSpec((ne, tm, d), lambda b, j: (b, tile_of(j), 0)),
            pl.BlockSpec((ne, HALO, d), lambda b, j: (b, jnp.minimum((tile_of(j) + 1) * hb, last), 0)))


def _projection(layer, hc, hx, mod, norm_g, w_qk, w_main, w_vg, cos, sin, conv_w, conv_b, kscale, gate_b):
    bsz, ctx_len, d = hc.shape
    s_len = ctx_len + hx.shape[1]
    tm = ROW_TILE
    ne = math.gcd(bsz, PROJ_BATCH)
    tiles_total = s_len // tm
    tiles_ctx = ctx_len // tm
    ctx_blk = mod.shape[1] - 1

    def row_map(b, j):
        return (b, j, 0)

    def per_layer(*block):
        return pl.BlockSpec((None,) + block, lambda b, j: (layer,) + (0,) * len(block))

    kernel = functools.partial(_proj_kernel, tm=tm, tiles_ctx=tiles_ctx, tiles_total=tiles_total)
    out_shapes = (
        jax.ShapeDtypeStruct((bsz, s_len, ATT_WIDTH), jnp.bfloat16),
        jax.ShapeDtypeStruct((bsz, s_len, ATT_WIDTH), jnp.bfloat16),
        jax.ShapeDtypeStruct((bsz, s_len, KV_WIDTH), jnp.bfloat16),
        jax.ShapeDtypeStruct((bsz, s_len, ATT_WIDTH), jnp.bfloat16),
        jax.ShapeDtypeStruct((bsz, s_len, 2 * M_WIDTH), jnp.bfloat16),
        jax.ShapeDtypeStruct((bsz, s_len, M_WIDTH), jnp.bfloat16),
        jax.ShapeDtypeStruct((bsz, s_len, 2 * M_WIDTH), jnp.bfloat16),
        jax.ShapeDtypeStruct((bsz, s_len, LANES), jnp.float32),
    )
    return pl.pallas_call(
        kernel,
        grid=(bsz // ne, tiles_total),
        in_specs=[
            *_tile_with_halo_specs(ne, tm, d, ctx_len, lambda j: jnp.minimum(j, tiles_ctx - 1)),
            *_tile_with_halo_specs(ne, tm, d, s_len - ctx_len, lambda j: jnp.maximum(j - tiles_ctx, 0)),
            pl.BlockSpec((None, ne, 3, d), lambda b, j: (layer, b, 0, 0)),
            pl.BlockSpec((None, 1, 3, d), lambda b, j: (layer, ctx_blk, 0, 0)),
            per_layer(1, d),
            pl.BlockSpec((None, d, 2 * ATT_WIDTH), lambda b, j: (layer, 0, 0), pipeline_mode=pl.Buffered(1)),
            pl.BlockSpec((None, d, N_MAIN), lambda b, j: (layer, 0, 0), pipeline_mode=pl.Buffered(1)),
            pl.BlockSpec((None, d, KV_WIDTH + LANES), lambda b, j: (layer, 0, 0), pipeline_mode=pl.Buffered(1)),
            pl.BlockSpec((tm, LANES), lambda b, j: (j, 0)),
            pl.BlockSpec((tm, LANES), lambda b, j: (j, 0)),
            per_layer(CONV_K, 2 * M_WIDTH),
            per_layer(1, 2 * M_WIDTH),
            pl.BlockSpec((1, 2 * M_WIDTH), lambda b, j: (0, 0)),
            per_layer(1, LANES),
        ],
        out_specs=[
            pl.BlockSpec((ne, tm, ATT_WIDTH), row_map),
            pl.BlockSpec((ne, tm, ATT_WIDTH), row_map),
            pl.BlockSpec((ne, tm, KV_WIDTH), row_map),
            pl.BlockSpec((ne, tm, ATT_WIDTH), row_map),
            pl.BlockSpec((ne, tm, 2 * M_WIDTH), row_map),
            pl.BlockSpec((ne, tm, M_WIDTH), row_map),
            pl.BlockSpec((ne, tm, 2 * M_WIDTH), row_map),
            pl.BlockSpec((ne, tm, LANES), row_map),
        ],
        out_shape=out_shapes,
        compiler_params=pltpu.CompilerParams(
            dimension_semantics=("arbitrary", "arbitrary"), vmem_limit_bytes=VMEM_LIMIT),
        name="norm_mod_projection",
    )(hc, hc, hc, hx, hx, hx, mod, mod, norm_g, w_qk, w_main, w_vg, cos, sin, conv_w, conv_b, kscale, gate_b)


def _attn_kernel(sink_ref, q_ref, kp_ref, kc_ref, kn_ref, kctx_ref, vp_ref, vc_ref, vn_ref, vctx_ref,
                 az_ref, o_ref, *, layer, blk0, ctx_blocks, total_blocks):
    qb = pl.program_id(1) + blk0
    is_ctx = qb < ctx_blocks
    n = qb - ctx_blocks
    lat_blocks = total_blocks - ctx_blocks
    ctx_len = ctx_blocks * BLOCK
    win = 3 * BLOCK
    neg = -jnp.inf

    i = lax.broadcasted_iota(jnp.int32, (BLOCK, win), 0)
    jx = lax.broadcasted_iota(jnp.int32, (BLOCK, win), 1)
    lo = jnp.where(n == 0, BLOCK, 0)
    hi = jnp.where(n == lat_blocks - 1, 2 * BLOCK, win)
    valid = (jx >= i) & (jx <= i + 2 * WINDOW) & (jx >= lo) & (jx < hi) & jnp.logical_not(is_ctx)
    bias = jnp.concatenate([jnp.where(valid, 0.0, neg)] * ATT_GROUP, axis=0)

    lane_v = lax.broadcasted_iota(jnp.int32, (1, LANES), 1)
    low_half = lane_v < ATT_HEAD_DIM
    lane_q = lax.broadcasted_iota(jnp.int32, (1, KV_GROUP_COLS), 1)
    head_of_qlane = (lane_q % LANES) // (LANES // ATT_GROUP)
    row_head = lax.broadcasted_iota(jnp.int32, (ATT_GROUP * BLOCK, 1), 0) // BLOCK
    n_keys = win + ctx_len
    ones = jnp.ones((n_keys, LANES), jnp.bfloat16)

    for e_i in range(q_ref.shape[0]):
        k_all = jnp.concatenate([kp_ref[e_i], kc_ref[e_i], kn_ref[e_i], kctx_ref[e_i]], axis=0)
        v_all = jnp.concatenate([vp_ref[e_i], vc_ref[e_i], vn_ref[e_i], vctx_ref[e_i]],
                                axis=0).astype(jnp.float32)
        v_rot = pltpu.roll(v_all, ATT_HEAD_DIM, axis=1)
        for kvh in range(ATT_KV_HEADS):
            c0 = kvh * KV_GROUP_COLS
            qg = q_ref[e_i, :, c0:c0 + KV_GROUP_COLS]
            qm = jnp.concatenate(
                [jnp.where(head_of_qlane == g, qg, jnp.zeros_like(qg)) for g in range(ATT_GROUP)], axis=0)
            kx = k_all[:, c0:c0 + KV_GROUP_COLS]
            s = lax.dot_general(qm, kx, (((1,), (1,)), ((), ())), preferred_element_type=jnp.float32)
            s_win = s[:, :win] + bias
            s_ctx = s[:, win:]
            sink = jnp.zeros((ATT_GROUP * BLOCK, 1), jnp.float32)
            for g in range(ATT_GROUP):
                sink = jnp.where(row_head == g, sink_ref[layer, kvh * ATT_GROUP + g] * LOG2_E, sink)
            m = jnp.maximum(jnp.maximum(jnp.max(s_win, axis=-1, keepdims=True),
                                        jnp.max(s_ctx, axis=-1, keepdims=True)), sink)
            e = jnp.concatenate([jnp.exp2(s_win - m), jnp.exp2(s_ctx - m)], axis=1).astype(jnp.bfloat16)
            own_half = low_half if kvh == 0 else jnp.logical_not(low_half)
            v_two = jnp.where(own_half, v_all, v_rot).astype(jnp.bfloat16)
            res = jnp.dot(e, jnp.concatenate([v_two, ones], axis=1),
                          preferred_element_type=jnp.float32)
            o = res[:, :LANES] / (res[:, LANES:] + jnp.exp2(sink - m))
            az = az_ref[e_i, :, c0:c0 + KV_GROUP_COLS].astype(jnp.float32)
            for pair in range(ATT_GROUP // 2):
                both = jnp.where(low_half, o[2 * pair * BLOCK:(2 * pair + 1) * BLOCK],
                                 o[(2 * pair + 1) * BLOCK:(2 * pair + 2) * BLOCK])
                lo = c0 + pair * LANES
                o_ref[e_i, :, lo:lo + LANES] = (
                    both * _silu(az[:, pair * LANES:(pair + 1) * LANES])).astype(jnp.bfloat16)


def _attention(layer, sink, q, kx, v, az, *, ctx_len, skip_ctx):
    bsz, s_len, _ = q.shape
    ctx_blocks = ctx_len // BLOCK
    total_blocks = s_len // BLOCK
    blk0 = ctx_blocks if skip_ctx else 0
    lo, hi = ctx_blocks, total_blocks - 1
    nb = math.gcd(bsz, BATCH_PER_STEP)

    def at(off):
        return lambda b, j: (b, jnp.clip(j + blk0 + off, lo, hi), 0)

    def cur(b, j):
        return (b, j + blk0, 0)

    def ctx(b, j):
        return (b, 0, 0)

    kernel = functools.partial(_attn_kernel, layer=layer, blk0=blk0, ctx_blocks=ctx_blocks, total_blocks=total_blocks)
    return pl.pallas_call(
        kernel,
        grid=(bsz // nb, total_blocks - blk0),
        in_specs=[
            pl.BlockSpec(memory_space=pltpu.SMEM),
            pl.BlockSpec((nb, BLOCK, ATT_WIDTH), cur),
            pl.BlockSpec((nb, BLOCK, ATT_WIDTH), at(-1)),
            pl.BlockSpec((nb, BLOCK, ATT_WIDTH), at(0)),
            pl.BlockSpec((nb, BLOCK, ATT_WIDTH), at(1)),
            pl.BlockSpec((nb, ctx_len, ATT_WIDTH), ctx),
            pl.BlockSpec((nb, BLOCK, KV_WIDTH), at(-1)),
            pl.BlockSpec((nb, BLOCK, KV_WIDTH), at(0)),
            pl.BlockSpec((nb, BLOCK, KV_WIDTH), at(1)),
            pl.BlockSpec((nb, ctx_len, KV_WIDTH), ctx),
            pl.BlockSpec((nb, BLOCK, ATT_WIDTH), cur),
        ],
        out_specs=pl.BlockSpec((nb, BLOCK, ATT_WIDTH), lambda b, j: (b, j, 0)),
        out_shape=jax.ShapeDtypeStruct((bsz, s_len - blk0 * BLOCK, ATT_WIDTH), jnp.bfloat16),
        compiler_params=pltpu.CompilerParams(
            dimension_semantics=("arbitrary", "arbitrary"), vmem_limit_bytes=VMEM_LIMIT),
        name="window_ctx_attention",
    )(sink, q, kx, kx, kx, kx, v, v, v, v, az)


STATE_ROWS = M_HEAD_DIM + SUBLANES


def _split2(x):
    hi = x.astype(jnp.bfloat16)
    lo = (x - hi.astype(jnp.float32)).astype(jnp.bfloat16)
    return hi, lo


def _mlstm_kernel(qkf_ref, vf_ref, gf_ref, qkb_ref, vb_ref, gb_ref, hf_ref, hb_ref, c_ref, m_ref):
    @pl.when(pl.program_id(1) == 0)
    def _():
        c_ref[...] = jnp.zeros_like(c_ref)
        m_ref[...] = jnp.full_like(m_ref, -jnp.inf)

    rows = lax.broadcasted_iota(jnp.int32, (CHUNK, CHUNK), 0)
    cols = lax.broadcasted_iota(jnp.int32, (CHUNK, CHUNK), 1)
    le = rows <= cols
    ge = rows >= cols
    tri_le = jnp.where(le, 1.0, 0.0).astype(jnp.bfloat16)
    tri_ge = jnp.where(ge, 1.0, 0.0).astype(jnp.bfloat16)

    for e in range(qkf_ref.shape[0]):
        _mlstm_element(e, qkf_ref, vf_ref, gf_ref, qkb_ref, vb_ref, gb_ref, hf_ref, hb_ref, c_ref, m_ref,
                       le, ge, tri_le, tri_ge)


def _mlstm_element(e, qkf_ref, vf_ref, gf_ref, qkb_ref, vb_ref, gb_ref, hf_ref, hb_ref, c_ref, m_ref,
                   le, ge, tri_le, tri_ge):
    nst = N_DIR * M_HEADS
    gates = (gf_ref[e], gb_ref[e])
    gates_t = (gates[0].T, gates[1].T)

    i8 = jnp.concatenate([gates_t[0][0:M_HEADS], gates_t[1][M_HEADS:nst]], axis=0)
    lf8 = _log_sigmoid(jnp.concatenate([gates_t[0][nst:nst + M_HEADS],
                                        gates_t[1][nst + M_HEADS:2 * nst]], axis=0))
    pre8 = sum(jnp.dot(p, tri_le, preferred_element_type=jnp.float32) for p in _split2(lf8))
    tot8 = pre8[:, CHUNK - 1:CHUNK]
    is_fwd = lax.broadcasted_iota(jnp.int32, (nst, 1), 0) < M_HEADS
    b8 = jnp.where(is_fwd, pre8, tot8 - pre8 + lf8)
    r8 = i8 - b8
    g8 = tot8 + r8
    m_prev = m_ref[e]
    m_new = jnp.maximum(tot8 + m_prev, jnp.max(g8, axis=1, keepdims=True))
    a8 = jnp.exp(tot8 + m_prev - m_new)
    w8 = jnp.exp(g8 - m_new)
    inter8 = b8 + m_prev
    m_ref[e] = m_new

    ones_row = jnp.where(lax.broadcasted_iota(jnp.int32, (SUBLANES, CHUNK), 0) == 0, 1.0, 0.0)

    lf_c2 = _log_sigmoid(jnp.concatenate(gates, axis=1))
    pre_c2 = sum(jnp.dot(tri_ge, p, preferred_element_type=jnp.float32) for p in _split2(lf_c2))

    for d, (qk_ref, v_ref, h_ref) in enumerate(((qkf_ref, vf_ref, hf_ref), (qkb_ref, vb_ref, hb_ref))):
        valid_t = le if d == 0 else ge
        lf_c = lf_c2[:, d * LANES:(d + 1) * LANES]
        pre_c = pre_c2[:, d * LANES:(d + 1) * LANES]
        b_c = pre_c if d == 0 else pre_c[CHUNK - 1:CHUNK, :] - pre_c + lf_c
        r_c = pltpu.roll(gates[d], nst, axis=1) - b_c

        heads = range(M_HEADS)
        rs = [d * M_HEADS + hd for hd in heads]
        qs = [qk_ref[e, :, hd * M_HEAD_DIM:(hd + 1) * M_HEAD_DIM] for hd in heads]
        ks = [qk_ref[e, :, M_WIDTH + hd * M_HEAD_DIM:M_WIDTH + (hd + 1) * M_HEAD_DIM] for hd in heads]
        vts = [jnp.concatenate([v_ref[e, :, hd * M_HEAD_DIM:(hd + 1) * M_HEAD_DIM].astype(jnp.float32).T,
                                ones_row], axis=0) for hd in heads]
        cs = [c_ref[e, r] for r in rs]

        qts = [q.astype(jnp.float32).T for q in qs]
        st = [jnp.dot(k, qt.astype(jnp.bfloat16), preferred_element_type=jnp.float32)
              for k, qt in zip(ks, qts)]
        dt = [jnp.where(valid_t, r_c[:, nst + r:nst + r + 1] + b8[r:r + 1, :], -jnp.inf) for r in rs]
        m_t = [jnp.maximum(inter8[r:r + 1, :], jnp.max(x, axis=0, keepdims=True)) for r, x in zip(rs, dt)]
        w_i = [jnp.exp(inter8[r:r + 1, :] - m) for r, m in zip(rs, m_t)]
        pt = [s_ * jnp.exp(x - m) for s_, x, m in zip(st, dt, m_t)]
        both = [jnp.dot(jnp.concatenate([vt, c], axis=1).astype(jnp.bfloat16),
                        jnp.concatenate([p, qt * w], axis=0).astype(jnp.bfloat16),
                        preferred_element_type=jnp.float32)
                for vt, c, p, qt, w in zip(vts, cs, pt, qts, w_i)]
        for hd in heads:
            den = both[hd][M_HEAD_DIM:M_HEAD_DIM + 1]
            ht = both[hd][:M_HEAD_DIM] / jnp.maximum(jnp.abs(den), jnp.exp(-m_t[hd]))
            h_ref[e, :, hd * M_HEAD_DIM:(hd + 1) * M_HEAD_DIM] = ht.T.astype(h_ref.dtype)
        for hd, r in zip(heads, rs):
            vw = (vts[hd] * w8[r:r + 1, :]).astype(jnp.bfloat16)
            c_ref[e, r] = a8[r:r + 1, :] * cs[hd] + jnp.dot(vw, ks[hd], preferred_element_type=jnp.float32)


def _mlstm(mqk, mv, gates, *, ctx_len):
    bsz, s_len, _ = mqk.shape
    nc = s_len // CHUNK
    ncc = ctx_len // CHUNK

    def fwd(b, j):
        return (b, j, 0)

    def bwd(b, j):
        return (b, jnp.where(j < ncc, ncc - 1 - j, nc - 1 + ncc - j), 0)

    nstate = N_DIR * M_HEADS
    nb = math.gcd(bsz, BATCH_PER_STEP)
    return pl.pallas_call(
        _mlstm_kernel,
        grid=(bsz // nb, nc),
        in_specs=[
            pl.BlockSpec((nb, CHUNK, 2 * M_WIDTH), fwd),
            pl.BlockSpec((nb, CHUNK, M_WIDTH), fwd),
            pl.BlockSpec((nb, CHUNK, LANES), fwd),
            pl.BlockSpec((nb, CHUNK, 2 * M_WIDTH), bwd),
            pl.BlockSpec((nb, CHUNK, M_WIDTH), bwd),
            pl.BlockSpec((nb, CHUNK, LANES), bwd),
        ],
        out_specs=[
            pl.BlockSpec((nb, CHUNK, M_WIDTH), fwd),
            pl.BlockSpec((nb, CHUNK, M_WIDTH), bwd),
        ],
        out_shape=(jax.ShapeDtypeStruct((bsz, s_len, M_WIDTH), jnp.bfloat16),
                   jax.ShapeDtypeStruct((bsz, s_len, M_WIDTH), jnp.bfloat16)),
        scratch_shapes=[
            pltpu.VMEM((nb, nstate, STATE_ROWS, M_HEAD_DIM), jnp.float32),
            pltpu.VMEM((nb, nstate, LANES), jnp.float32),
        ],
        compiler_params=pltpu.CompilerParams(
            dimension_semantics=("arbitrary", "arbitrary"), vmem_limit_bytes=VMEM_LIMIT),
        name="mlstm_bidirectional_scan",
    )(mqk, mv, gates, mqk, mv, gates)


def _out_kernel(x_ref, att_ref, hf_ref, hb_ref, moz_ref, hg_ref, w_ref, mod_ref, *rest, final):
    if final:
        fg_ref, o_ref = rest
    else:
        (o_ref,) = rest
    ne, tm, _ = x_ref.shape
    mxs = []
    for e in range(ne):
        mo = moz_ref[e, :, :M_WIDTH].astype(jnp.float32)
        mz = moz_ref[e, :, M_WIDTH:].astype(jnp.float32)
        hs = _sigmoid(mo) * (hf_ref[e].astype(jnp.float32) + hb_ref[e].astype(jnp.float32))
        parts = []
        for hd in range(M_HEADS):
            hh = hs[:, hd * M_HEAD_DIM:(hd + 1) * M_HEAD_DIM]
            ms = jnp.mean(hh * hh, axis=-1, keepdims=True)
            parts.append(hh * lax.rsqrt(ms + EPS))
        mxs.append((jnp.concatenate(parts, axis=1) * hg_ref[...] * _silu(mz)).astype(jnp.bfloat16))
    att = jnp.concatenate([att_ref[e] for e in range(ne)], axis=0)
    out = (jnp.dot(att, w_ref[:ATT_WIDTH, :], preferred_element_type=jnp.float32)
           + jnp.dot(jnp.concatenate(mxs, axis=0), w_ref[ATT_WIDTH:, :], preferred_element_type=jnp.float32))
    for e in range(ne):
        gate = mod_ref[e if mod_ref.shape[0] > 1 else 0, 2:3, :]
        xnew = x_ref[e] + gate * out[e * tm:(e + 1) * tm]
        if final:
            ms = jnp.mean(xnew * xnew, axis=-1, keepdims=True)
            o_ref[e] = xnew * lax.rsqrt(ms + EPS) * fg_ref[...]
        else:
            o_ref[e] = xnew


def _output(layer, hseg, att, hf, hb, moz, head_g, w_out, mod, final_g, *, row0, att_row0, is_ctx, final):
    bsz, seg_len, d = hseg.shape
    tm = ROW_TILE
    ne = math.gcd(bsz, PROJ_BATCH)
    t0 = row0 // tm
    a0 = att_row0 // tm

    def seg_map(b, j):
        return (b, j, 0)

    def row_map(b, j):
        return (b, j + t0, 0)

    if is_ctx:
        mod_spec = pl.BlockSpec((None, 1, 3, d), lambda b, j: (layer, mod.shape[1] - 1, 0, 0))
    else:
        mod_spec = pl.BlockSpec((None, ne, 3, d), lambda b, j: (layer, b, 0, 0))
    in_specs = [
        pl.BlockSpec((ne, tm, d), seg_map),
        pl.BlockSpec((ne, tm, ATT_WIDTH), lambda b, j: (b, j + a0, 0)),
        pl.BlockSpec((ne, tm, M_WIDTH), row_map),
        pl.BlockSpec((ne, tm, M_WIDTH), row_map),
        pl.BlockSpec((ne, tm, 2 * M_WIDTH), row_map),
        pl.BlockSpec((None, 1, M_WIDTH), lambda b, j: (layer, 0, 0)),
        pl.BlockSpec((None, d, d), lambda b, j: (layer, 0, 0), pipeline_mode=pl.Buffered(1)),
        mod_spec,
    ]
    args = [hseg, att, hf, hb, moz, head_g, w_out, mod]
    if final:
        in_specs.append(pl.BlockSpec((1, d), lambda b, j: (0, 0)))
        args.append(final_g)
    return pl.pallas_call(
        functools.partial(_out_kernel, final=final),
        grid=(bsz // ne, seg_len // tm),
        in_specs=in_specs,
        out_specs=pl.BlockSpec((ne, tm, d), seg_map),
        out_shape=jax.ShapeDtypeStruct(hseg.shape, jnp.float32),
        compiler_params=pltpu.CompilerParams(
            dimension_semantics=("arbitrary", "arbitrary"), vmem_limit_bytes=VMEM_LIMIT),
        name="gate_norm_out_projection",
    )(*args)


def kernel(x, c, ctx, c_ctx, w_ada, b_ada, norm_g, w_in, conv_w, conv_b, gate_b, sink, head_g, w_out, final_g):
    bsz, t_len, d = x.shape
    ctx_len = ctx.shape[1]
    depth = w_in.shape[0]
    assert d == D_MODEL and t_len % ROW_TILE == 0 and ctx_len % ROW_TILE == 0 and t_len % GRID_W == 0

    hc, hx = ctx, x
    mod_rows = -(-(bsz + 1) // SUBLANES) * SUBLANES
    cc = jnp.zeros((mod_rows, d), jnp.float32).at[:bsz].set(c).at[mod_rows - 1].set(c_ctx)
    mod = _modulation(cc, w_ada, b_ada).reshape(depth, mod_rows, 3, d)

    w_qk, w_main, w_vg = _split_w_in(w_in)
    w_o = w_out.astype(jnp.bfloat16)
    cos, sin = _rope_tables(t_len, ctx_len)
    kscale = jnp.concatenate([jnp.ones((1, M_WIDTH), jnp.float32),
                              jnp.full((1, M_WIDTH), M_HEAD_DIM ** -0.5, jnp.float32)], axis=1)
    gate_bias = jnp.pad(gate_b, ((0, 0), (0, LANES - N_GATES)))[:, None, :]
    norm_g3, conv_b3, head_g3 = norm_g[:, None, :], conv_b[:, None, :], head_g[:, None, :]

    for l in range(depth):
        last = l == depth - 1
        q, kx, v, az, mqk, mv, moz, gates = _projection(
            l, hc, hx, mod, norm_g3, w_qk, w_main, w_vg, cos, sin, conv_w, conv_b3, kscale, gate_bias)
        att = _attention(l, sink, q, kx, v, az, ctx_len=ctx_len, skip_ctx=last)
        hf, hb = _mlstm(mqk, mv, gates, ctx_len=ctx_len)
        out_args = (att, hf, hb, moz, head_g3, w_o, mod, final_g[None])
        if last:
            return _output(l, hx, *out_args, row0=ctx_len, att_row0=0, is_ctx=False, final=True)
        hc, hx = (_output(l, hc, *out_args, row0=0, att_row0=0, is_ctx=True, final=False),
                  _output(l, hx, *out_args, row0=ctx_len, att_row0=ctx_len, is_ctx=False, final=False))
```

```python
import functools
import math

import jax
import jax.numpy as jnp
from jax import lax
from jax.experimental import pallas as pl
from jax.experimental.pallas import tpu as pltpu

D_MODEL = 1024
GRID_W = 64
ATT_HEADS = 8
ATT_KV_HEADS = 2
ATT_GROUP = ATT_HEADS // ATT_KV_HEADS
ATT_HEAD_DIM = 64
ATT_WIDTH = ATT_HEADS * ATT_HEAD_DIM
KV_WIDTH = ATT_KV_HEADS * ATT_HEAD_DIM
WINDOW = 128
BLOCK = 128
ROPE_BASE = 10000.0
M_HEADS = 4
M_HEAD_DIM = 128
M_WIDTH = M_HEADS * M_HEAD_DIM
CONV_K = 5
CHUNK = 128
N_DIR = 2
N_GATES = 2 * N_DIR * M_HEADS
EPS = 1e-6

LANES = 128
SUBLANES = 8
HALO = SUBLANES
ROW_TILE = 256
BATCH_PER_STEP = 8
PROJ_BATCH = 4
PROJ_STACK = 1
VMEM_LIMIT = 56 * 1024 * 1024

W_IN_MAIN_START = ATT_WIDTH + 2 * KV_WIDTH
C_AZ = 0
C_MQK = C_AZ + ATT_WIDTH
C_MV = C_MQK + 2 * M_WIDTH
C_MOZ = C_MV + M_WIDTH
N_MAIN = C_MOZ + 2 * M_WIDTH
KV_GROUP_COLS = ATT_GROUP * ATT_HEAD_DIM
LOG2_E = 1.4426950408889634
Q_PRESCALE = ATT_HEAD_DIM ** -0.5 * LOG2_E


def _sigmoid(x):
    return 1.0 / (1.0 + jnp.exp(-x))


def _silu(x):
    return x * _sigmoid(x)


def _log_sigmoid(x):
    return jnp.minimum(x, 0.0) - jnp.log(1.0 + jnp.exp(-jnp.abs(x)))


def _split_w_in(w_in):
    depth, d, n_in = w_in.shape
    w_t = jnp.swapaxes(w_in, 1, 2)
    quarter = ATT_HEAD_DIM // 4
    k0 = ATT_WIDTH
    v0 = k0 + KV_WIDTH
    wq = w_t[:, :k0].reshape(depth, ATT_KV_HEADS, ATT_GROUP, 2, 2, quarter, d)
    wq = wq.transpose(0, 1, 4, 2, 3, 5, 6).reshape(depth, ATT_WIDTH, d)
    wk = w_t[:, k0:v0].reshape(depth, ATT_KV_HEADS, 1, 2, 2, quarter, d)
    wk = jnp.broadcast_to(wk.transpose(0, 1, 4, 2, 3, 5, 6),
                          (depth, ATT_KV_HEADS, 2, ATT_GROUP, 2, quarter, d)).reshape(depth, ATT_WIDTH, d)
    w_qk = jnp.swapaxes(jnp.concatenate([wq, wk], axis=1), 1, 2).astype(jnp.bfloat16)
    cast_rows = W_IN_MAIN_START
    v_blk = (W_IN_MAIN_START - KV_WIDTH) // KV_WIDTH
    g_blk = (W_IN_MAIN_START + N_MAIN) // N_GATES
    w_main, w_vg = pl.pallas_call(
        _cast_w_kernel,
        grid=(depth, N_MAIN // cast_rows),
        in_specs=[pl.BlockSpec((1, cast_rows, d), lambda l, r: (l, r + 1, 0)),
                  pl.BlockSpec((1, KV_WIDTH, d), lambda l, r: (l, v_blk, 0)),
                  pl.BlockSpec((1, N_GATES, d), lambda l, r: (l, g_blk, 0))],
        out_specs=[pl.BlockSpec((1, d, cast_rows), lambda l, r: (l, 0, r)),
                   pl.BlockSpec((1, d, KV_WIDTH + LANES), lambda l, r: (l, 0, 0))],
        out_shape=(jax.ShapeDtypeStruct((depth, d, N_MAIN), jnp.bfloat16),
                   jax.ShapeDtypeStruct((depth, d, KV_WIDTH + LANES), jnp.bfloat16)),
        compiler_params=pltpu.CompilerParams(dimension_semantics=("arbitrary", "arbitrary")),
        name="cast_projection_weights",
    )(w_t, w_t, w_t)
    return w_qk, w_main, w_vg


def _cast_w_kernel(wm_in, wv_in, wg_in, wm_ref, wvg_ref):
    wm_ref[0] = wm_in[0].T.astype(jnp.bfloat16)

    @pl.when(pl.program_id(1) == 0)
    def _():
        gates = jnp.concatenate([wg_in[0], jnp.zeros((LANES - N_GATES, wg_in.shape[2]), jnp.float32)], axis=0)
        wvg_ref[0, :, :KV_WIDTH] = wv_in[0].T.astype(jnp.bfloat16)
        wvg_ref[0, :, KV_WIDTH:] = gates.T.astype(jnp.bfloat16)


def _rope_tables(t_len, ctx_len):
    rows = t_len // GRID_W
    row = jnp.repeat(jnp.arange(rows), GRID_W).astype(jnp.float32)
    col = jnp.tile(jnp.arange(GRID_W), rows).astype(jnp.float32)
    half = ATT_HEAD_DIM // 2
    inv = ROPE_BASE ** (-jnp.arange(0, half, 2, dtype=jnp.float32) / half)
    ang = jnp.concatenate([row[:, None] * inv, col[:, None] * inv], axis=-1)
    ang = jnp.tile(ang, (1, LANES // half))
    cos = jnp.concatenate([jnp.ones((ctx_len, LANES), jnp.float32), jnp.cos(ang)], axis=0)
    sin = jnp.concatenate([jnp.zeros((ctx_len, LANES), jnp.float32), jnp.sin(ang)], axis=0)
    return cos, sin


def _mod_kernel(c_ref, w_ref, b_ref, o_ref):
    sc = _silu(c_ref[...])
    o_ref[0] = jnp.dot(sc.astype(jnp.bfloat16), w_ref[0].astype(jnp.bfloat16),
                       preferred_element_type=jnp.float32) + b_ref[0]


def _modulation(cc, w_ada, b_ada):
    depth, d, d3 = w_ada.shape
    rows = cc.shape[0]
    nt = d3 // d
    return pl.pallas_call(
        _mod_kernel,
        grid=(depth, nt),
        in_specs=[
            pl.BlockSpec((rows, d), lambda l, n: (0, 0)),
            pl.BlockSpec((1, d, d), lambda l, n: (l, 0, n)),
            pl.BlockSpec((1, 1, d), lambda l, n: (l, 0, n)),
        ],
        out_specs=pl.BlockSpec((1, rows, d), lambda l, n: (l, 0, n)),
        out_shape=jax.ShapeDtypeStruct((depth, rows, d3), jnp.float32),
        compiler_params=pltpu.CompilerParams(dimension_semantics=("arbitrary", "arbitrary")),
        name="adaln_modulation",
    )(cc, w_ada, b_ada.reshape(depth, 1, d3))


def _proj_kernel(*refs, tm, tiles_ctx, tiles_total):
    n_elem = refs[4].shape[0]
    for e0 in range(0, n_elem, PROJ_STACK):
        _proj_group(e0, min(PROJ_STACK, n_elem - e0), *refs, tm=tm, tiles_ctx=tiles_ctx, tiles_total=tiles_total)


def _proj_group(e0, ne, cp_ref, c_ref, cn_ref, xp_ref, x_ref, xn_ref, mod_ref, modc_ref, g_ref,
                wqk_ref, wm_ref, wvg_ref,
                cos_ref, sin_ref, cw_ref, cb_ref,
                ks_ref, gb_ref,
                q_ref, kx_ref, v_ref, az_ref, mqk_ref, mv_ref, moz_ref, gate_ref,
                *, tm, tiles_ctx, tiles_total):
    te = tm + 2 * HALO
    j = pl.program_id(1)
    is_ctx = j < tiles_ctx
    seg_first = jnp.logical_or(j == 0, j == tiles_ctx)
    seg_last = jnp.logical_or(j == tiles_ctx - 1, j == tiles_total - 1)

    ext = []
    for e in range(e0, e0 + ne):
        xe = jnp.concatenate([jnp.where(is_ctx, cp_ref[e], xp_ref[e]), jnp.where(is_ctx, c_ref[e], x_ref[e]),
                              jnp.where(is_ctx, cn_ref[e], xn_ref[e])], axis=0)
        ms = jnp.mean(xe * xe, axis=-1, keepdims=True)
        shift = jnp.where(is_ctx, modc_ref[0, 0:1, :], mod_ref[e, 0:1, :])
        scale = jnp.where(is_ctx, modc_ref[0, 1:2, :], mod_ref[e, 1:2, :])
        gain = g_ref[...] * (1.0 + scale)
        ext.append((xe * lax.rsqrt(ms + EPS) * gain + shift).astype(jnp.bfloat16))
    xn_ext = jnp.concatenate(ext, axis=0)
    xn = jnp.concatenate([x_[HALO:HALO + tm] for x_ in ext], axis=0)

    def proj(rows, w_ref, c0, c1):
        return jnp.dot(rows, w_ref[:, c0:c1], preferred_element_type=jnp.float32)

    row = lax.broadcasted_iota(jnp.int32, (te, 1), 0)
    keep = jnp.logical_and(jnp.logical_or(row >= HALO, jnp.logical_not(seg_first)),
                           jnp.logical_or(row < HALO + tm, jnp.logical_not(seg_last)))
    nb = tm // SUBLANES
    sub = lax.broadcasted_iota(jnp.int32, (1, SUBLANES, 1), 1)

    def conv_chunk(c0, width):
        y_all = proj(xn_ext, wm_ref, C_MQK + c0, C_MQK + c0 + width)
        for i, e in enumerate(range(e0, e0 + ne)):
            ye = jnp.where(keep, y_all[i * te:(i + 1) * te], 0.0).reshape(nb + 2, SUBLANES, width)
            acc = jnp.broadcast_to(cb_ref[:, c0:c0 + width].reshape(1, 1, width), (nb, SUBLANES, width))
            for t in range(CONV_K):
                delta = t - CONV_K // 2
                w_t = cw_ref[t:t + 1, c0:c0 + width].reshape(1, 1, width)
                if delta == 0:
                    tap = ye[1:nb + 1]
                else:
                    rot = pltpu.roll(ye, (-delta) % SUBLANES, axis=1)
                    if delta > 0:
                        tap = jnp.where(sub < SUBLANES - delta, rot[1:nb + 1], rot[2:nb + 2])
                    else:
                        tap = jnp.where(sub >= -delta, rot[1:nb + 1], rot[0:nb])
                acc = acc + w_t * tap
            acc = acc.reshape(tm, width)
            mqk_ref[e, :, c0:c0 + width] = (_silu(acc) * ks_ref[:, c0:c0 + width]).astype(jnp.bfloat16)

    def rope_group(grp):
        is_q = grp < ATT_KV_HEADS
        cos = cos_ref[...] * Q_PRESCALE if is_q else cos_ref[...]
        sin = sin_ref[...] * Q_PRESCALE if is_q else sin_ref[...]
        yq = proj(xn, wqk_ref, grp * KV_GROUP_COLS, (grp + 1) * KV_GROUP_COLS)
        dst = q_ref if is_q else kx_ref
        c0 = (grp % ATT_KV_HEADS) * KV_GROUP_COLS
        for i, e in enumerate(range(e0, e0 + ne)):
            first = yq[i * tm:(i + 1) * tm, :LANES]
            second = yq[i * tm:(i + 1) * tm, LANES:]
            dst[e, :, c0:c0 + LANES] = (first * cos - second * sin).astype(jnp.bfloat16)
            dst[e, :, c0 + LANES:c0 + 2 * LANES] = (second * cos + first * sin).astype(jnp.bfloat16)

    def plain(dst_ref, c0, width, off=0):
        y = proj(xn, wm_ref, c0, c0 + width)
        for i, e in enumerate(range(e0, e0 + ne)):
            dst_ref[e, :, off:off + width] = y[i * tm:(i + 1) * tm].astype(dst_ref.dtype)

    others = [functools.partial(rope_group, g) for g in range(2 * ATT_KV_HEADS)]
    others += [functools.partial(plain, az_ref, C_AZ, ATT_WIDTH),
               functools.partial(plain, mv_ref, C_MV, M_WIDTH),
               functools.partial(plain, moz_ref, C_MOZ, M_WIDTH),
               functools.partial(plain, moz_ref, C_MOZ + M_WIDTH, M_WIDTH, M_WIDTH)]
    per_chunk = 2
    cw = 2 * M_WIDTH * per_chunk // len(others)
    for i in range(len(others) // per_chunk):
        conv_chunk(i * cw, cw)
        for other in others[i * per_chunk:(i + 1) * per_chunk]:
            other()
    y = proj(xn, wvg_ref, 0, KV_WIDTH + LANES)
    for i, e in enumerate(range(e0, e0 + ne)):
        v_ref[e] = y[i * tm:(i + 1) * tm, :KV_WIDTH].astype(jnp.bfloat16)
        gate_ref[e] = y[i * tm:(i + 1) * tm, KV_WIDTH:] + gb_ref[...]


def _tile_with_halo_specs(ne, tm, d, n_rows, tile_of):
    hb = tm // HALO
    last = n_rows // HALO - 1
    return (pl.BlockSpec((ne, HALO, d), lambda b, j: (b, jnp.maximum(tile_of(j) * hb - 1, 0), 0)),
            pl.BlockSpec((ne, tm, d), lambda b, j: (b, tile_of(j), 0)),
            pl.BlockSpec((ne, HALO, d), lambda b, j: (b, jnp.minimum((tile_of(j) + 1) * hb, last), 0)))


def _projection(layer, hc, hx, mod, norm_g, w_qk, w_main, w_vg, cos, sin, conv_w, conv_b, kscale, gate_b):
    bsz, ctx_len, d = hc.shape
    s_len = ctx_len + hx.shape[1]
    tm = ROW_TILE
    ne = math.gcd(bsz, PROJ_BATCH)
    tiles_total = s_len // tm
    tiles_ctx = ctx_len // tm
    ctx_blk = mod.shape[1] - 1

    def row_map(b, j):
        return (b, j, 0)

    def per_layer(*block):
        return pl.BlockSpec((None,) + block, lambda b, j: (layer,) + (0,) * len(block))

    kernel = functools.partial(_proj_kernel, tm=tm, tiles_ctx=tiles_ctx, tiles_total=tiles_total)
    out_shapes = (
        jax.ShapeDtypeStruct((bsz, s_len, ATT_WIDTH), jnp.bfloat16),
        jax.ShapeDtypeStruct((bsz, s_len, ATT_WIDTH), jnp.bfloat16),
        jax.ShapeDtypeStruct((bsz, s_len, KV_WIDTH), jnp.bfloat16),
        jax.ShapeDtypeStruct((bsz, s_len, ATT_WIDTH), jnp.bfloat16),
        jax.ShapeDtypeStruct((bsz, s_len, 2 * M_WIDTH), jnp.bfloat16),
        jax.ShapeDtypeStruct((bsz, s_len, M_WIDTH), jnp.bfloat16),
        jax.ShapeDtypeStruct((bsz, s_len, 2 * M_WIDTH), jnp.bfloat16),
        jax.ShapeDtypeStruct((bsz, s_len, LANES), jnp.float32),
    )
    return pl.pallas_call(
        kernel,
        grid=(bsz // ne, tiles_total),
        in_specs=[
            *_tile_with_halo_specs(ne, tm, d, ctx_len, lambda j: jnp.minimum(j, tiles_ctx - 1)),
            *_tile_with_halo_specs(ne, tm, d, s_len - ctx_len, lambda j: jnp.maximum(j - tiles_ctx, 0)),
            pl.BlockSpec((None, ne, 3, d), lambda b, j: (layer, b, 0, 0)),
            pl.BlockSpec((None, 1, 3, d), lambda b, j: (layer, ctx_blk, 0, 0)),
            per_layer(1, d),
            pl.BlockSpec((None, d, 2 * ATT_WIDTH), lambda b, j: (layer, 0, 0), pipeline_mode=pl.Buffered(1)),
            pl.BlockSpec((None, d, N_MAIN), lambda b, j: (layer, 0, 0), pipeline_mode=pl.Buffered(1)),
            pl.BlockSpec((None, d, KV_WIDTH + LANES), lambda b, j: (layer, 0, 0), pipeline_mode=pl.Buffered(1)),
            pl.BlockSpec((tm, LANES), lambda b, j: (j, 0)),
            pl.BlockSpec((tm, LANES), lambda b, j: (j, 0)),
            per_layer(CONV_K, 2 * M_WIDTH),
            per_layer(1, 2 * M_WIDTH),
            pl.BlockSpec((1, 2 * M_WIDTH), lambda b, j: (0, 0)),
            per_layer(1, LANES),
        ],
        out_specs=[
            pl.BlockSpec((ne, tm, ATT_WIDTH), row_map),
            pl.BlockSpec((ne, tm, ATT_WIDTH), row_map),
            pl.BlockSpec((ne, tm, KV_WIDTH), row_map),
            pl.BlockSpec((ne, tm, ATT_WIDTH), row_map),
            pl.BlockSpec((ne, tm, 2 * M_WIDTH), row_map),
            pl.BlockSpec((ne, tm, M_WIDTH), row_map),
            pl.BlockSpec((ne, tm, 2 * M_WIDTH), row_map),
            pl.BlockSpec((ne, tm, LANES), row_map),
        ],
        out_shape=out_shapes,
        compiler_params=pltpu.CompilerParams(
            dimension_semantics=("arbitrary", "arbitrary"), vmem_limit_bytes=VMEM_LIMIT),
        name="norm_mod_projection",
    )(hc, hc, hc, hx, hx, hx, mod, mod, norm_g, w_qk, w_main, w_vg, cos, sin, conv_w, conv_b, kscale, gate_b)


def _attn_kernel(sink_ref, q_ref, kp_ref, kc_ref, kn_ref, kctx_ref, vp_ref, vc_ref, vn_ref, vctx_ref,
                 az_ref, o_ref, *, layer, blk0, ctx_blocks, total_blocks):
    qb = pl.program_id(1) + blk0
    is_ctx = qb < ctx_blocks
    n = qb - ctx_blocks
    lat_blocks = total_blocks - ctx_blocks
    ctx_len = ctx_blocks * BLOCK
    win = 3 * BLOCK
    neg = -jnp.inf

    i = lax.broadcasted_iota(jnp.int32, (BLOCK, win), 0)
    jx = lax.broadcasted_iota(jnp.int32, (BLOCK, win), 1)
    lo = jnp.where(n == 0, BLOCK, 0)
    hi = jnp.where(n == lat_blocks - 1, 2 * BLOCK, win)
    valid = (jx >= i) & (jx <= i + 2 * WINDOW) & (jx >= lo) & (jx < hi) & jnp.logical_not(is_ctx)
    bias = jnp.concatenate([jnp.where(valid, 0.0, neg)] * ATT_GROUP, axis=0)

    lane_v = lax.broadcasted_iota(jnp.int32, (1, LANES), 1)
    low_half = lane_v < ATT_HEAD_DIM
    lane_q = lax.broadcasted_iota(jnp.int32, (1, KV_GROUP_COLS), 1)
    head_of_qlane = (lane_q % LANES) // (LANES // ATT_GROUP)
    row_head = lax.broadcasted_iota(jnp.int32, (ATT_GROUP * BLOCK, 1), 0) // BLOCK
    n_keys = win + ctx_len
    ones = jnp.ones((n_keys, LANES), jnp.bfloat16)

    for e_i in range(q_ref.shape[0]):
        k_all = jnp.concatenate([kp_ref[e_i], kc_ref[e_i], kn_ref[e_i], kctx_ref[e_i]], axis=0)
        v_all = jnp.concatenate([vp_ref[e_i], vc_ref[e_i], vn_ref[e_i], vctx_ref[e_i]],
                                axis=0).astype(jnp.float32)
        v_rot = pltpu.roll(v_all, ATT_HEAD_DIM, axis=1)
        for kvh in range(ATT_KV_HEADS):
            c0 = kvh * KV_GROUP_COLS
            qg = q_ref[e_i, :, c0:c0 + KV_GROUP_COLS]
            qm = jnp.concatenate(
                [jnp.where(head_of_qlane == g, qg, jnp.zeros_like(qg)) for g in range(ATT_GROUP)], axis=0)
            kx = k_all[:, c0:c0 + KV_GROUP_COLS]
            s = lax.dot_general(qm, kx, (((1,), (1,)), ((), ())), preferred_element_type=jnp.float32)
            s_win = s[:, :win] + bias
            s_ctx = s[:, win:]
            sink = jnp.zeros((ATT_GROUP * BLOCK, 1), jnp.float32)
            for g in range(ATT_GROUP):
                sink = jnp.where(row_head == g, sink_ref[layer, kvh * ATT_GROUP + g] * LOG2_E, sink)
            m = jnp.maximum(jnp.maximum(jnp.max(s_win, axis=-1, keepdims=True),
                                        jnp.max(s_ctx, axis=-1, keepdims=True)), sink)
            e = jnp.concatenate([jnp.exp2(s_win - m), jnp.exp2(s_ctx - m)], axis=1).astype(jnp.bfloat16)
            own_half = low_half if kvh == 0 else jnp.logical_not(low_half)
            v_two = jnp.where(own_half, v_all, v_rot).astype(jnp.bfloat16)
            res = jnp.dot(e, jnp.concatenate([v_two, ones], axis=1),
                          preferred_element_type=jnp.float32)
            o = res[:, :LANES] / (res[:, LANES:] + jnp.exp2(sink - m))
            az = az_ref[e_i, :, c0:c0 + KV_GROUP_COLS].astype(jnp.float32)
            for pair in range(ATT_GROUP // 2):
                both = jnp.where(low_half, o[2 * pair * BLOCK:(2 * pair + 1) * BLOCK],
                                 o[(2 * pair + 1) * BLOCK:(2 * pair + 2) * BLOCK])
                lo = c0 + pair * LANES
                o_ref[e_i, :, lo:lo + LANES] = (
                    both * _silu(az[:, pair * LANES:(pair + 1) * LANES])).astype(jnp.bfloat16)


def _attention(layer, sink, q, kx, v, az, *, ctx_len, skip_ctx):
    bsz, s_len, _ = q.shape
    ctx_blocks = ctx_len // BLOCK
    total_blocks = s_len // BLOCK
    blk0 = ctx_blocks if skip_ctx else 0
    lo, hi = ctx_blocks, total_blocks - 1
    nb = math.gcd(bsz, BATCH_PER_STEP)

    def at(off):
        return lambda b, j: (b, jnp.clip(j + blk0 + off, lo, hi), 0)

    def cur(b, j):
        return (b, j + blk0, 0)

    def ctx(b, j):
        return (b, 0, 0)

    kernel = functools.partial(_attn_kernel, layer=layer, blk0=blk0, ctx_blocks=ctx_blocks, total_blocks=total_blocks)
    return pl.pallas_call(
        kernel,
        grid=(bsz // nb, total_blocks - blk0),
        in_specs=[
            pl.BlockSpec(memory_space=pltpu.SMEM),
            pl.BlockSpec((nb, BLOCK, ATT_WIDTH), cur),
            pl.BlockSpec((nb, BLOCK, ATT_WIDTH), at(-1)),
            pl.BlockSpec((nb, BLOCK, ATT_WIDTH), at(0)),
            pl.BlockSpec((nb, BLOCK, ATT_WIDTH), at(1)),
            pl.BlockSpec((nb, ctx_len, ATT_WIDTH), ctx),
            pl.BlockSpec((nb, BLOCK, KV_WIDTH), at(-1)),
            pl.BlockSpec((nb, BLOCK, KV_WIDTH), at(0)),
            pl.BlockSpec((nb, BLOCK, KV_WIDTH), at(1)),
            pl.BlockSpec((nb, ctx_len, KV_WIDTH), ctx),
            pl.BlockSpec((nb, BLOCK, ATT_WIDTH), cur),
        ],
        out_specs=pl.BlockSpec((nb, BLOCK, ATT_WIDTH), lambda b, j: (b, j, 0)),
        out_shape=jax.ShapeDtypeStruct((bsz, s_len - blk0 * BLOCK, ATT_WIDTH), jnp.bfloat16),
        compiler_params=pltpu.CompilerParams(
            dimension_semantics=("arbitrary", "arbitrary"), vmem_limit_bytes=VMEM_LIMIT),
        name="window_ctx_attention",
    )(sink, q, kx, kx, kx, kx, v, v, v, v, az)


STATE_ROWS = M_HEAD_DIM + 2 * SUBLANES


def _split2(x):
    hi = x.astype(jnp.bfloat16)
    lo = (x - hi.astype(jnp.float32)).astype(jnp.bfloat16)
    return hi, lo


def _mlstm_kernel(qkf_ref, vf_ref, gf_ref, qkb_ref, vb_ref, gb_ref, hf_ref, hb_ref, c_ref, m_ref):
    @pl.when(pl.program_id(1) == 0)
    def _():
        c_ref[...] = jnp.zeros_like(c_ref)
        m_ref[...] = jnp.full_like(m_ref, -jnp.inf)

    rows = lax.broadcasted_iota(jnp.int32, (CHUNK, CHUNK), 0)
    cols = lax.broadcasted_iota(jnp.int32, (CHUNK, CHUNK), 1)
    le = rows <= cols
    ge = rows >= cols
    tri_le = jnp.where(le, 1.0, 0.0).astype(jnp.bfloat16)
    tri_ge = jnp.where(ge, 1.0, 0.0).astype(jnp.bfloat16)

    for e in range(qkf_ref.shape[0]):
        _mlstm_element(e, qkf_ref, vf_ref, gf_ref, qkb_ref, vb_ref, gb_ref, hf_ref, hb_ref, c_ref, m_ref,
                       le, ge, tri_le, tri_ge)


def _mlstm_element(e, qkf_ref, vf_ref, gf_ref, qkb_ref, vb_ref, gb_ref, hf_ref, hb_ref, c_ref, m_ref,
                   le, ge, tri_le, tri_ge):
    nst = N_DIR * M_HEADS
    lane = lax.broadcasted_iota(jnp.int32, (1, LANES), 1)
    fwd_lane = lane % nst < M_HEADS
    gates = jnp.where(fwd_lane, gf_ref[e], gb_ref[e])
    gates_t = gates.T

    i8 = gates_t[0:nst]
    lf8 = _log_sigmoid(gates_t[nst:2 * nst])
    pre8 = sum(jnp.dot(p, tri_le, preferred_element_type=jnp.float32) for p in _split2(lf8))
    tot8 = pre8[:, CHUNK - 1:CHUNK]
    is_fwd = lax.broadcasted_iota(jnp.int32, (nst, 1), 0) < M_HEADS
    b8 = jnp.where(is_fwd, pre8, tot8 - pre8 + lf8)
    r8 = i8 - b8
    g8 = tot8 + r8
    m_prev = m_ref[e]
    m_new = jnp.maximum(tot8 + m_prev, jnp.max(g8, axis=1, keepdims=True))
    a8 = jnp.exp(tot8 + m_prev - m_new)
    w8 = jnp.exp(g8 - m_new)
    inter8 = b8 + m_prev
    m_ref[e] = m_new

    ones_row = jnp.where(lax.broadcasted_iota(jnp.int32, (2 * SUBLANES, CHUNK), 0) == 0, 1.0, 0.0
                         ).astype(jnp.bfloat16)

    lf_c = _log_sigmoid(gates)
    pre_c = sum(jnp.dot(tri_ge, p, preferred_element_type=jnp.float32) for p in _split2(lf_c))
    b_c = jnp.where(fwd_lane, pre_c, pre_c[CHUNK - 1:CHUNK, :] - pre_c + lf_c)
    r_c = pltpu.roll(gates, nst, axis=1) - b_c

    for d, (qk_ref, v_ref, h_ref) in enumerate(((qkf_ref, vf_ref, hf_ref), (qkb_ref, vb_ref, hb_ref))):
        valid_t = le if d == 0 else ge

        heads = range(M_HEADS)
        rs = [d * M_HEADS + hd for hd in heads]
        qs = [qk_ref[e, :, hd * M_HEAD_DIM:(hd + 1) * M_HEAD_DIM] for hd in heads]
        ks = [qk_ref[e, :, M_WIDTH + hd * M_HEAD_DIM:M_WIDTH + (hd + 1) * M_HEAD_DIM] for hd in heads]
        vts = [jnp.concatenate([v_ref[e, :, hd * M_HEAD_DIM:(hd + 1) * M_HEAD_DIM].T,
                                ones_row], axis=0) for hd in heads]
        cs = [c_ref[e, r] for r in rs]

        qts = [q.T for q in qs]
        st = [jnp.dot(k, qt, preferred_element_type=jnp.float32)
              for k, qt in zip(ks, qts)]
        dt = [jnp.where(valid_t, r_c[:, nst + r:nst + r + 1] + b8[r:r + 1, :], -jnp.inf) for r in rs]
        m_t = [jnp.maximum(inter8[r:r + 1, :], jnp.max(x, axis=0, keepdims=True)) for r, x in zip(rs, dt)]
        w_i = [jnp.exp(inter8[r:r + 1, :] - m) for r, m in zip(rs, m_t)]
        pt = [s_ * jnp.exp(x - m) for s_, x, m in zip(st, dt, m_t)]
        both = [jnp.dot(jnp.concatenate([vt, c.astype(jnp.bfloat16)], axis=1),
                        jnp.concatenate([p, qt * w], axis=0).astype(jnp.bfloat16),
                        preferred_element_type=jnp.float32)
                for vt, c, p, qt, w in zip(vts, cs, pt, qts, w_i)]
        for hd in heads:
            den = both[hd][M_HEAD_DIM:M_HEAD_DIM + 1]
            ht = both[hd][:M_HEAD_DIM] / jnp.maximum(jnp.abs(den), jnp.exp(-m_t[hd]))
            h_ref[e, :, hd * M_HEAD_DIM:(hd + 1) * M_HEAD_DIM] = ht.T.astype(h_ref.dtype)
        for hd, r in zip(heads, rs):
            vw = (vts[hd] * w8[r:r + 1, :]).astype(jnp.bfloat16)
            c_ref[e, r] = a8[r:r + 1, :] * cs[hd] + jnp.dot(vw, ks[hd], preferred_element_type=jnp.float32)


def _mlstm(mqk, mv, gates, *, ctx_len):
    bsz, s_len, _ = mqk.shape
    nc = s_len // CHUNK
    ncc = ctx_len // CHUNK

    def fwd(b, j):
        return (b, j, 0)

    def bwd(b, j):
        return (b, jnp.where(j < ncc, ncc - 1 - j, nc - 1 + ncc - j), 0)

    nstate = N_DIR * M_HEADS
    nb = math.gcd(bsz, BATCH_PER_STEP)
    return pl.pallas_call(
        _mlstm_kernel,
        grid=(bsz // nb, nc),
        in_specs=[
            pl.BlockSpec((nb, CHUNK, 2 * M_WIDTH), fwd),
            pl.BlockSpec((nb, CHUNK, M_WIDTH), fwd),
            pl.BlockSpec((nb, CHUNK, LANES), fwd),
            pl.BlockSpec((nb, CHUNK, 2 * M_WIDTH), bwd),
            pl.BlockSpec((nb, CHUNK, M_WIDTH), bwd),
            pl.BlockSpec((nb, CHUNK, LANES), bwd),
        ],
        out_specs=[
            pl.BlockSpec((nb, CHUNK, M_WIDTH), fwd),
            pl.BlockSpec((nb, CHUNK, M_WIDTH), bwd),
        ],
        out_shape=(jax.ShapeDtypeStruct((bsz, s_len, M_WIDTH), jnp.bfloat16),
                   jax.ShapeDtypeStruct((bsz, s_len, M_WIDTH), jnp.bfloat16)),
        scratch_shapes=[
            pltpu.VMEM((nb, nstate, STATE_ROWS, M_HEAD_DIM), jnp.float32),
            pltpu.VMEM((nb, nstate, LANES), jnp.float32),
        ],
        compiler_params=pltpu.CompilerParams(
            dimension_semantics=("arbitrary", "arbitrary"), vmem_limit_bytes=VMEM_LIMIT),
        name="mlstm_bidirectional_scan",
    )(mqk, mv, gates, mqk, mv, gates)


def _out_kernel(x_ref, att_ref, hf_ref, hb_ref, moz_ref, hg_ref, w_ref, mod_ref, *rest, final):
    if final:
        fg_ref, o_ref = rest
    else:
        (o_ref,) = rest
    ne, tm, _ = x_ref.shape
    mxs = []
    for e in range(ne):
        mo = moz_ref[e, :, :M_WIDTH].astype(jnp.float32)
        mz = moz_ref[e, :, M_WIDTH:].astype(jnp.float32)
        hs = _sigmoid(mo) * (hf_ref[e].astype(jnp.float32) + hb_ref[e].astype(jnp.float32))
        parts = []
        for hd in range(M_HEADS):
            hh = hs[:, hd * M_HEAD_DIM:(hd + 1) * M_HEAD_DIM]
            ms = jnp.mean(hh * hh, axis=-1, keepdims=True)
            parts.append(hh * lax.rsqrt(ms + EPS))
        mxs.append((jnp.concatenate(parts, axis=1) * hg_ref[...] * _silu(mz)).astype(jnp.bfloat16))
    att = jnp.concatenate([att_ref[e] for e in range(ne)], axis=0)
    out = (jnp.dot(att, w_ref[:ATT_WIDTH, :], preferred_element_type=jnp.float32)
           + jnp.dot(jnp.concatenate(mxs, axis=0), w_ref[ATT_WIDTH:, :], preferred_element_type=jnp.float32))
    for e in range(ne):
        gate = mod_ref[e if mod_ref.shape[0] > 1 else 0, 2:3, :]
        xnew = x_ref[e] + gate * out[e * tm:(e + 1) * tm]
        if final:
            ms = jnp.mean(xnew * xnew, axis=-1, keepdims=True)
            o_ref[e] = xnew * lax.rsqrt(ms + EPS) * fg_ref[...]
        else:
            o_ref[e] = xnew


def _output(layer, hseg, att, hf, hb, moz, head_g, w_out, mod, final_g, *, row0, att_row0, is_ctx, final):
    bsz, seg_len, d = hseg.shape
    tm = ROW_TILE
    ne = math.gcd(bsz, PROJ_BATCH)
    t0 = row0 // tm
    a0 = att_row0 // tm

    def seg_map(b, j):
        return (b, j, 0)

    def row_map(b, j):
        return (b, j + t0, 0)

    if is_ctx:
        mod_spec = pl.BlockSpec((None, 1, 3, d), lambda b, j: (layer, mod.shape[1] - 1, 0, 0))
    else:
        mod_spec = pl.BlockSpec((None, ne, 3, d), lambda b, j: (layer, b, 0, 0))
    in_specs = [
        pl.BlockSpec((ne, tm, d), seg_map),
        pl.BlockSpec((ne, tm, ATT_WIDTH), lambda b, j: (b, j + a0, 0)),
        pl.BlockSpec((ne, tm, M_WIDTH), row_map),
        pl.BlockSpec((ne, tm, M_WIDTH), row_map),
        pl.BlockSpec((ne, tm, 2 * M_WIDTH), row_map),
        pl.BlockSpec((None, 1, M_WIDTH), lambda b, j: (layer, 0, 0)),
        pl.BlockSpec((None, d, d), lambda b, j: (layer, 0, 0), pipeline_mode=pl.Buffered(1)),
        mod_spec,
    ]
    args = [hseg, att, hf, hb, moz, head_g, w_out, mod]
    if final:
        in_specs.append(pl.BlockSpec((1, d), lambda b, j: (0, 0)))
        args.append(final_g)
    return pl.pallas_call(
        functools.partial(_out_kernel, final=final),
        grid=(bsz // ne, seg_len // tm),
        in_specs=in_specs,
        out_specs=pl.BlockSpec((ne, tm, d), seg_map),
        out_shape=jax.ShapeDtypeStruct(hseg.shape, jnp.float32),
        compiler_params=pltpu.CompilerParams(
            dimension_semantics=("arbitrary", "arbitrary"), vmem_limit_bytes=VMEM_LIMIT),
        name="gate_norm_out_projection",
    )(*args)


def kernel(x, c, ctx, c_ctx, w_ada, b_ada, norm_g, w_in, conv_w, conv_b, gate_b, sink, head_g, w_out, final_g):
    bsz, t_len, d = x.shape
    ctx_len = ctx.shape[1]
    depth = w_in.shape[0]
    assert d == D_MODEL and t_len % ROW_TILE == 0 and ctx_len % ROW_TILE == 0 and t_len % GRID_W == 0

    hc, hx = ctx, x
    mod_rows = -(-(bsz + 1) // SUBLANES) * SUBLANES
    cc = jnp.zeros((mod_rows, d), jnp.float32).at[:bsz].set(c).at[mod_rows - 1].set(c_ctx)
    mod = _modulation(cc, w_ada, b_ada).reshape(depth, mod_rows, 3, d)

    w_qk, w_main, w_vg = _split_w_in(w_in)
    w_o = w_out.astype(jnp.bfloat16)
    cos, sin = _rope_tables(t_len, ctx_len)
    kscale = jnp.concatenate([jnp.ones((1, M_WIDTH), jnp.float32),
                              jnp.full((1, M_WIDTH), M_HEAD_DIM ** -0.5, jnp.float32)], axis=1)
    gate_bias = jnp.pad(gate_b, ((0, 0), (0, LANES - N_GATES)))[:, None, :]
    norm_g3, conv_b3, head_g3 = norm_g[:, None, :], conv_b[:, None, :], head_g[:, None, :]

    for l in range(depth):
        last = l == depth - 1
        q, kx, v, az, mqk, mv, moz, gates = _projection(
            l, hc, hx, mod, norm_g3, w_qk, w_main, w_vg, cos, sin, conv_w, conv_b3, kscale, gate_bias)
        att = _attention(l, sink, q, kx, v, az, ctx_len=ctx_len, skip_ctx=last)
        hf, hb = _mlstm(mqk, mv, gates, ctx_len=ctx_len)
        out_args = (att, hf, hb, moz, head_g3, w_o, mod, final_g[None])
        if last:
            return _output(l, hx, *out_args, row0=ctx_len, att_row0=0, is_ctx=False, final=True)
        hc, hx = (_output(l, hc, *out_args, row0=0, att_row0=0, is_ctx=True, final=False),
                  _output(l, hx, *out_args, row0=ctx_len, att_row0=ctx_len, is_ctx=False, final=False))
```

```python
import functools
import math

import jax
import jax.numpy as jnp
from jax import lax
from jax.experimental import pallas as pl
from jax.experimental.pallas import tpu as pltpu

D_MODEL = 1024
GRID_W = 64
ATT_HEADS = 8
ATT_KV_HEADS = 2
ATT_GROUP = ATT_HEADS // ATT_KV_HEADS
ATT_HEAD_DIM = 64
ATT_WIDTH = ATT_HEADS * ATT_HEAD_DIM
KV_WIDTH = ATT_KV_HEADS * ATT_HEAD_DIM
WINDOW = 128
BLOCK = 128
ROPE_BASE = 10000.0
M_HEADS = 4
M_HEAD_DIM = 128
M_WIDTH = M_HEADS * M_HEAD_DIM
CONV_K = 5
CHUNK = 128
N_DIR = 2
N_GATES = 2 * N_DIR * M_HEADS
EPS = 1e-6

LANES = 128
SUBLANES = 8
HALO = SUBLANES
ROW_TILE = 256
BATCH_PER_STEP = 8
PROJ_BATCH = 2
PROJ_STACK = 1
VMEM_LIMIT = 56 * 1024 * 1024

W_IN_MAIN_START = ATT_WIDTH + 2 * KV_WIDTH
C_AZ = 0
C_MQK = C_AZ + ATT_WIDTH
C_MV = C_MQK + 2 * M_WIDTH
C_MOZ = C_MV + M_WIDTH
N_MAIN = C_MOZ + 2 * M_WIDTH
KV_GROUP_COLS = ATT_GROUP * ATT_HEAD_DIM
LOG2_E = 1.4426950408889634
Q_PRESCALE = ATT_HEAD_DIM ** -0.5 * LOG2_E


def _sigmoid(x):
    return 1.0 / (1.0 + jnp.exp(-x))


def _silu(x):
    return x * _sigmoid(x)


def _log_sigmoid(x):
    return jnp.minimum(x, 0.0) - jnp.log(1.0 + jnp.exp(-jnp.abs(x)))


def _split_w_in(w_in):
    depth, d, n_in = w_in.shape
    w_t = jnp.swapaxes(w_in, 1, 2)
    quarter = ATT_HEAD_DIM // 4
    k0 = ATT_WIDTH
    v0 = k0 + KV_WIDTH
    wq = w_t[:, :k0].reshape(depth, ATT_KV_HEADS, ATT_GROUP, 2, 2, quarter, d)
    wq = wq.transpose(0, 1, 4, 2, 3, 5, 6).reshape(depth, ATT_WIDTH, d)
    wk = w_t[:, k0:v0].reshape(depth, ATT_KV_HEADS, 1, 2, 2, quarter, d)
    wk = jnp.broadcast_to(wk.transpose(0, 1, 4, 2, 3, 5, 6),
                          (depth, ATT_KV_HEADS, 2, ATT_GROUP, 2, quarter, d)).reshape(depth, ATT_WIDTH, d)
    w_qk = jnp.swapaxes(jnp.concatenate([wq, wk], axis=1), 1, 2).astype(jnp.bfloat16)
    cast_rows = W_IN_MAIN_START
    v_blk = (W_IN_MAIN_START - KV_WIDTH) // KV_WIDTH
    g_blk = (W_IN_MAIN_START + N_MAIN) // N_GATES
    w_main, w_vg = pl.pallas_call(
        _cast_w_kernel,
        grid=(depth, N_MAIN // cast_rows),
        in_specs=[pl.BlockSpec((1, cast_rows, d), lambda l, r: (l, r + 1, 0)),
                  pl.BlockSpec((1, KV_WIDTH, d), lambda l, r: (l, v_blk, 0)),
                  pl.BlockSpec((1, N_GATES, d), lambda l, r: (l, g_blk, 0))],
        out_specs=[pl.BlockSpec((1, d, cast_rows), lambda l, r: (l, 0, r)),
                   pl.BlockSpec((1, d, KV_WIDTH + LANES), lambda l, r: (l, 0, 0))],
        out_shape=(jax.ShapeDtypeStruct((depth, d, N_MAIN), jnp.bfloat16),
                   jax.ShapeDtypeStruct((depth, d, KV_WIDTH + LANES), jnp.bfloat16)),
        compiler_params=pltpu.CompilerParams(dimension_semantics=("arbitrary", "arbitrary")),
        name="cast_projection_weights",
    )(w_t, w_t, w_t)
    return w_qk, w_main, w_vg


def _cast_w_kernel(wm_in, wv_in, wg_in, wm_ref, wvg_ref):
    wm_ref[0] = wm_in[0].T.astype(jnp.bfloat16)

    @pl.when(pl.program_id(1) == 0)
    def _():
        gates = jnp.concatenate([wg_in[0], jnp.zeros((LANES - N_GATES, wg_in.shape[2]), jnp.float32)], axis=0)
        wvg_ref[0, :, :KV_WIDTH] = wv_in[0].T.astype(jnp.bfloat16)
        wvg_ref[0, :, KV_WIDTH:] = gates.T.astype(jnp.bfloat16)


def _rope_tables(t_len, ctx_len):
    rows = t_len // GRID_W
    row = jnp.repeat(jnp.arange(rows), GRID_W).astype(jnp.float32)
    col = jnp.tile(jnp.arange(GRID_W), rows).astype(jnp.float32)
    half = ATT_HEAD_DIM // 2
    inv = ROPE_BASE ** (-jnp.arange(0, half, 2, dtype=jnp.float32) / half)
    ang = jnp.concatenate([row[:, None] * inv, col[:, None] * inv], axis=-1)
    ang = jnp.tile(ang, (1, LANES // half))
    cos = jnp.concatenate([jnp.ones((ctx_len, LANES), jnp.float32), jnp.cos(ang)], axis=0)
    sin = jnp.concatenate([jnp.zeros((ctx_len, LANES), jnp.float32), jnp.sin(ang)], axis=0)
    return cos, sin


def _mod_kernel(c_ref, w_ref, b_ref, o_ref):
    sc = _silu(c_ref[...])
    o_ref[0] = jnp.dot(sc.astype(jnp.bfloat16), w_ref[0].astype(jnp.bfloat16),
                       preferred_element_type=jnp.float32) + b_ref[0]


def _modulation(cc, w_ada, b_ada):
    depth, d, d3 = w_ada.shape
    rows = cc.shape[0]
    nt = d3 // d
    return pl.pallas_call(
        _mod_kernel,
        grid=(depth, nt),
        in_specs=[
            pl.BlockSpec((rows, d), lambda l, n: (0, 0)),
            pl.BlockSpec((1, d, d), lambda l, n: (l, 0, n)),
            pl.BlockSpec((1, 1, d), lambda l, n: (l, 0, n)),
        ],
        out_specs=pl.BlockSpec((1, rows, d), lambda l, n: (l, 0, n)),
        out_shape=jax.ShapeDtypeStruct((depth, rows, d3), jnp.float32),
        compiler_params=pltpu.CompilerParams(dimension_semantics=("arbitrary", "arbitrary")),
        name="adaln_modulation",
    )(cc, w_ada, b_ada.reshape(depth, 1, d3))


def _proj_kernel(*refs, tm, tiles_ctx, tiles_total):
    n_elem = refs[4].shape[0]
    for e0 in range(0, n_elem, PROJ_STACK):
        _proj_group(e0, min(PROJ_STACK, n_elem - e0), *refs, tm=tm, tiles_ctx=tiles_ctx, tiles_total=tiles_total)


def _proj_group(e0, ne, cp_ref, c_ref, cn_ref, xp_ref, x_ref, xn_ref, mod_ref, modc_ref, g_ref,
                wqk_ref, wm_ref, wvg_ref,
                cos_ref, sin_ref, cw_ref, cb_ref,
                ks_ref, gb_ref,
                q_ref, kx_ref, v_ref, az_ref, mqk_ref, mv_ref, moz_ref, gate_ref,
                *, tm, tiles_ctx, tiles_total):
    te = tm + 2 * HALO
    j = pl.program_id(1)
    is_ctx = j < tiles_ctx
    seg_first = jnp.logical_or(j == 0, j == tiles_ctx)
    seg_last = jnp.logical_or(j == tiles_ctx - 1, j == tiles_total - 1)

    ext = []
    for e in range(e0, e0 + ne):
        xe = jnp.concatenate([jnp.where(is_ctx, cp_ref[e], xp_ref[e]), jnp.where(is_ctx, c_ref[e], x_ref[e]),
                              jnp.where(is_ctx, cn_ref[e], xn_ref[e])], axis=0)
        ms = jnp.mean(xe * xe, axis=-1, keepdims=True)
        shift = jnp.where(is_ctx, modc_ref[0, 0:1, :], mod_ref[e, 0:1, :])
        scale = jnp.where(is_ctx, modc_ref[0, 1:2, :], mod_ref[e, 1:2, :])
        gain = g_ref[...] * (1.0 + scale)
        ext.append((xe * lax.rsqrt(ms + EPS) * gain + shift).astype(jnp.bfloat16))
    xn_ext = jnp.concatenate(ext, axis=0)
    xn = jnp.concatenate([x_[HALO:HALO + tm] for x_ in ext], axis=0)

    def proj(rows, w_ref, c0, c1):
        return jnp.dot(rows, w_ref[:, c0:c1], preferred_element_type=jnp.float32)

    row = lax.broadcasted_iota(jnp.int32, (te, 1), 0)
    keep = jnp.logical_and(jnp.logical_or(row >= HALO, jnp.logical_not(seg_first)),
                           jnp.logical_or(row < HALO + tm, jnp.logical_not(seg_last)))
    nb = tm // SUBLANES
    sub = lax.broadcasted_iota(jnp.int32, (1, SUBLANES, 1), 1)

    def conv_chunk(c0, width):
        y_all = proj(xn_ext, wm_ref, C_MQK + c0, C_MQK + c0 + width)
        for i, e in enumerate(range(e0, e0 + ne)):
            ye = jnp.where(keep, y_all[i * te:(i + 1) * te], 0.0).reshape(nb + 2, SUBLANES, width)
            acc = jnp.broadcast_to(cb_ref[:, c0:c0 + width].reshape(1, 1, width), (nb, SUBLANES, width))
            for t in range(CONV_K):
                delta = t - CONV_K // 2
                w_t = cw_ref[t:t + 1, c0:c0 + width].reshape(1, 1, width)
                if delta == 0:
                    tap = ye[1:nb + 1]
                else:
                    rot = pltpu.roll(ye, (-delta) % SUBLANES, axis=1)
                    if delta > 0:
                        tap = jnp.where(sub < SUBLANES - delta, rot[1:nb + 1], rot[2:nb + 2])
                    else:
                        tap = jnp.where(sub >= -delta, rot[1:nb + 1], rot[0:nb])
                acc = acc + w_t * tap
            acc = acc.reshape(tm, width)
            mqk_ref[e, :, c0:c0 + width] = (_silu(acc) * ks_ref[:, c0:c0 + width]).astype(jnp.bfloat16)

    def rope_group(grp):
        is_q = grp < ATT_KV_HEADS
        cos = cos_ref[...] * Q_PRESCALE if is_q else cos_ref[...]
        sin = sin_ref[...] * Q_PRESCALE if is_q else sin_ref[...]
        yq = proj(xn, wqk_ref, grp * KV_GROUP_COLS, (grp + 1) * KV_GROUP_COLS)
        dst = q_ref if is_q else kx_ref
        c0 = (grp % ATT_KV_HEADS) * KV_GROUP_COLS
        for i, e in enumerate(range(e0, e0 + ne)):
            first = yq[i * tm:(i + 1) * tm, :LANES]
            second = yq[i * tm:(i + 1) * tm, LANES:]
            dst[e, :, c0:c0 + LANES] = (first * cos - second * sin).astype(jnp.bfloat16)
            dst[e, :, c0 + LANES:c0 + 2 * LANES] = (second * cos + first * sin).astype(jnp.bfloat16)

    def plain(dst_ref, c0, width, off=0):
        y = proj(xn, wm_ref, c0, c0 + width)
        for i, e in enumerate(range(e0, e0 + ne)):
            dst_ref[e, :, off:off + width] = y[i * tm:(i + 1) * tm].astype(dst_ref.dtype)

    others = [functools.partial(rope_group, g) for g in range(2 * ATT_KV_HEADS)]
    others += [functools.partial(plain, az_ref, C_AZ, ATT_WIDTH),
               functools.partial(plain, mv_ref, C_MV, M_WIDTH),
               functools.partial(plain, moz_ref, C_MOZ, M_WIDTH),
               functools.partial(plain, moz_ref, C_MOZ + M_WIDTH, M_WIDTH, M_WIDTH)]
    per_chunk = 2
    cw = 2 * M_WIDTH * per_chunk // len(others)
    for i in range(len(others) // per_chunk):
        conv_chunk(i * cw, cw)
        for other in others[i * per_chunk:(i + 1) * per_chunk]:
            other()
    y = proj(xn, wvg_ref, 0, KV_WIDTH + LANES)
    for i, e in enumerate(range(e0, e0 + ne)):
        v_ref[e] = y[i * tm:(i + 1) * tm, :KV_WIDTH].astype(jnp.bfloat16)
        gate_ref[e] = y[i * tm:(i + 1) * tm, KV_WIDTH:] + gb_ref[...]


def _tile_with_halo_specs(ne, tm, d, n_rows, tile_of):
    hb = tm // HALO
    last = n_rows // HALO - 1
    return (pl.BlockSpec((ne, HALO, d), lambda b, j: (b, jnp.maximum(tile_of(j) * hb - 1, 0), 0)),
            pl.BlockSpec((ne, tm, d), lambda b, j: (b, tile_of(j), 0)),
            pl.BlockSpec((ne, HALO, d), lambda b, j: (b, jnp.minimum((tile_of(j) + 1) * hb, last), 0)))


def _projection(layer, hc, hx, mod, norm_g, w_qk, w_main, w_vg, cos, sin, conv_w, conv_b, kscale, gate_b):
    bsz, ctx_len, d = hc.shape
    s_len = ctx_len + hx.shape[1]
    tm = ROW_TILE
    ne = math.gcd(bsz, PROJ_BATCH)
    tiles_total = s_len // tm
    tiles_ctx = ctx_len // tm
    ctx_blk = mod.shape[1] - 1

    def row_map(b, j):
        return (b, j, 0)

    def per_layer(*block):
        return pl.BlockSpec((None,) + block, lambda b, j: (layer,) + (0,) * len(block))

    kernel = functools.partial(_proj_kernel, tm=tm, tiles_ctx=tiles_ctx, tiles_total=tiles_total)
    out_shapes = (
        jax.ShapeDtypeStruct((bsz, s_len, ATT_WIDTH), jnp.bfloat16),
        jax.ShapeDtypeStruct((bsz, s_len, ATT_WIDTH), jnp.bfloat16),
        jax.ShapeDtypeStruct((bsz, s_len, KV_WIDTH), jnp.bfloat16),
        jax.ShapeDtypeStruct((bsz, s_len, ATT_WIDTH), jnp.bfloat16),
        jax.ShapeDtypeStruct((bsz, s_len, 2 * M_WIDTH), jnp.bfloat16),
        jax.ShapeDtypeStruct((bsz, s_len, M_WIDTH), jnp.bfloat16),
        jax.ShapeDtypeStruct((bsz, s_len, 2 * M_WIDTH), jnp.bfloat16),
        jax.ShapeDtypeStruct((bsz, s_len, LANES), jnp.float32),
    )
    return pl.pallas_call(
        kernel,
        grid=(bsz // ne, tiles_total),
        in_specs=[
            *_tile_with_halo_specs(ne, tm, d, ctx_len, lambda j: jnp.minimum(j, tiles_ctx - 1)),
            *_tile_with_halo_specs(ne, tm, d, s_len - ctx_len, lambda j: jnp.maximum(j - tiles_ctx, 0)),
            pl.BlockSpec((None, ne, 3, d), lambda b, j: (layer, b, 0, 0)),
            pl.BlockSpec((None, 1, 3, d), lambda b, j: (layer, ctx_blk, 0, 0)),
            per_layer(1, d),
            pl.BlockSpec((None, d, 2 * ATT_WIDTH), lambda b, j: (layer, 0, 0), pipeline_mode=pl.Buffered(1)),
            pl.BlockSpec((None, d, N_MAIN), lambda b, j: (layer, 0, 0), pipeline_mode=pl.Buffered(1)),
            pl.BlockSpec((None, d, KV_WIDTH + LANES), lambda b, j: (layer, 0, 0), pipeline_mode=pl.Buffered(1)),
            pl.BlockSpec((tm, LANES), lambda b, j: (j, 0)),
            pl.BlockSpec((tm, LANES), lambda b, j: (j, 0)),
            per_layer(CONV_K, 2 * M_WIDTH),
            per_layer(1, 2 * M_WIDTH),
            pl.BlockSpec((1, 2 * M_WIDTH), lambda b, j: (0, 0)),
            per_layer(1, LANES),
        ],
        out_specs=[
            pl.BlockSpec((ne, tm, ATT_WIDTH), row_map),
            pl.BlockSpec((ne, tm, ATT_WIDTH), row_map),
            pl.BlockSpec((ne, tm, KV_WIDTH), row_map),
            pl.BlockSpec((ne, tm, ATT_WIDTH), row_map),
            pl.BlockSpec((ne, tm, 2 * M_WIDTH), row_map),
            pl.BlockSpec((ne, tm, M_WIDTH), row_map),
            pl.BlockSpec((ne, tm, 2 * M_WIDTH), row_map),
            pl.BlockSpec((ne, tm, LANES), row_map),
        ],
        out_shape=out_shapes,
        compiler_params=pltpu.CompilerParams(
            dimension_semantics=("arbitrary", "arbitrary"), vmem_limit_bytes=VMEM_LIMIT),
        name="norm_mod_projection",
    )(hc, hc, hc, hx, hx, hx, mod, mod, norm_g, w_qk, w_main, w_vg, cos, sin, conv_w, conv_b, kscale, gate_b)


def _attn_kernel(sink_ref, q_ref, kp_ref, kc_ref, kn_ref, kctx_ref, vp_ref, vc_ref, vn_ref, vctx_ref,
                 az_ref, o_ref, *, layer, blk0, ctx_blocks, total_blocks):
    qb = pl.program_id(1) + blk0
    is_ctx = qb < ctx_blocks
    n = qb - ctx_blocks
    lat_blocks = total_blocks - ctx_blocks
    ctx_len = ctx_blocks * BLOCK
    win = 3 * BLOCK
    neg = -jnp.inf

    i = lax.broadcasted_iota(jnp.int32, (BLOCK, win), 0)
    jx = lax.broadcasted_iota(jnp.int32, (BLOCK, win), 1)
    lo = jnp.where(n == 0, BLOCK, 0)
    hi = jnp.where(n == lat_blocks - 1, 2 * BLOCK, win)
    valid = (jx >= i) & (jx <= i + 2 * WINDOW) & (jx >= lo) & (jx < hi) & jnp.logical_not(is_ctx)
    bias = jnp.concatenate([jnp.where(valid, 0.0, neg)] * ATT_GROUP, axis=0)

    lane_v = lax.broadcasted_iota(jnp.int32, (1, LANES), 1)
    low_half = lane_v < ATT_HEAD_DIM
    lane_q = lax.broadcasted_iota(jnp.int32, (1, KV_GROUP_COLS), 1)
    head_of_qlane = (lane_q % LANES) // (LANES // ATT_GROUP)
    row_head = lax.broadcasted_iota(jnp.int32, (ATT_GROUP * BLOCK, 1), 0) // BLOCK
    n_keys = win + ctx_len
    ones = jnp.ones((n_keys, LANES), jnp.bfloat16)

    for e_i in range(q_ref.shape[0]):
        k_all = jnp.concatenate([kp_ref[e_i], kc_ref[e_i], kn_ref[e_i], kctx_ref[e_i]], axis=0)
        v_all = jnp.concatenate([vp_ref[e_i], vc_ref[e_i], vn_ref[e_i], vctx_ref[e_i]],
                                axis=0).astype(jnp.float32)
        v_rot = pltpu.roll(v_all, ATT_HEAD_DIM, axis=1)
        for kvh in range(ATT_KV_HEADS):
            c0 = kvh * KV_GROUP_COLS
            qg = q_ref[e_i, :, c0:c0 + KV_GROUP_COLS]
            qm = jnp.concatenate(
                [jnp.where(head_of_qlane == g, qg, jnp.zeros_like(qg)) for g in range(ATT_GROUP)], axis=0)
            kx = k_all[:, c0:c0 + KV_GROUP_COLS]
            s = lax.dot_general(qm, kx, (((1,), (1,)), ((), ())), preferred_element_type=jnp.float32)
            s_win = s[:, :win] + bias
            s_ctx = s[:, win:]
            sink = jnp.zeros((ATT_GROUP * BLOCK, 1), jnp.float32)
            for g in range(ATT_GROUP):
                sink = jnp.where(row_head == g, sink_ref[layer, kvh * ATT_GROUP + g] * LOG2_E, sink)
            m = jnp.maximum(jnp.maximum(jnp.max(s_win, axis=-1, keepdims=True),
                                        jnp.max(s_ctx, axis=-1, keepdims=True)), sink)
            e = jnp.concatenate([jnp.exp2(s_win - m), jnp.exp2(s_ctx - m)], axis=1).astype(jnp.bfloat16)
            own_half = low_half if kvh == 0 else jnp.logical_not(low_half)
            v_two = jnp.where(own_half, v_all, v_rot).astype(jnp.bfloat16)
            res = jnp.dot(e, jnp.concatenate([v_two, ones], axis=1),
                          preferred_element_type=jnp.float32)
            o = res[:, :LANES] / (res[:, LANES:] + jnp.exp2(sink - m))
            az = az_ref[e_i, :, c0:c0 + KV_GROUP_COLS].astype(jnp.float32)
            for pair in range(ATT_GROUP // 2):
                both = jnp.where(low_half, o[2 * pair * BLOCK:(2 * pair + 1) * BLOCK],
                                 o[(2 * pair + 1) * BLOCK:(2 * pair + 2) * BLOCK])
                lo = c0 + pair * LANES
                o_ref[e_i, :, lo:lo + LANES] = (
                    both * _silu(az[:, pair * LANES:(pair + 1) * LANES])).astype(jnp.bfloat16)


def _attention(layer, sink, q, kx, v, az, *, ctx_len, skip_ctx):
    bsz, s_len, _ = q.shape
    ctx_blocks = ctx_len // BLOCK
    total_blocks = s_len // BLOCK
    blk0 = ctx_blocks if skip_ctx else 0
    lo, hi = ctx_blocks, total_blocks - 1
    nb = math.gcd(bsz, BATCH_PER_STEP)

    def at(off):
        return lambda b, j: (b, jnp.clip(j + blk0 + off, lo, hi), 0)

    def cur(b, j):
        return (b, j + blk0, 0)

    def ctx(b, j):
        return (b, 0, 0)

    kernel = functools.partial(_attn_kernel, layer=layer, blk0=blk0, ctx_blocks=ctx_blocks, total_blocks=total_blocks)
    return pl.pallas_call(
        kernel,
        grid=(bsz // nb, total_blocks - blk0),
        in_specs=[
            pl.BlockSpec(memory_space=pltpu.SMEM),
            pl.BlockSpec((nb, BLOCK, ATT_WIDTH), cur),
            pl.BlockSpec((nb, BLOCK, ATT_WIDTH), at(-1)),
            pl.BlockSpec((nb, BLOCK, ATT_WIDTH), at(0)),
            pl.BlockSpec((nb, BLOCK, ATT_WIDTH), at(1)),
            pl.BlockSpec((nb, ctx_len, ATT_WIDTH), ctx),
            pl.BlockSpec((nb, BLOCK, KV_WIDTH), at(-1)),
            pl.BlockSpec((nb, BLOCK, KV_WIDTH), at(0)),
            pl.BlockSpec((nb, BLOCK, KV_WIDTH), at(1)),
            pl.BlockSpec((nb, ctx_len, KV_WIDTH), ctx),
            pl.BlockSpec((nb, BLOCK, ATT_WIDTH), cur),
        ],
        out_specs=pl.BlockSpec((nb, BLOCK, ATT_WIDTH), lambda b, j: (b, j, 0)),
        out_shape=jax.ShapeDtypeStruct((bsz, s_len - blk0 * BLOCK, ATT_WIDTH), jnp.bfloat16),
        compiler_params=pltpu.CompilerParams(
            dimension_semantics=("arbitrary", "arbitrary"), vmem_limit_bytes=VMEM_LIMIT),
        name="window_ctx_attention",
    )(sink, q, kx, kx, kx, kx, v, v, v, v, az)


STATE_ROWS = M_HEAD_DIM + 2 * SUBLANES


def _split2(x):
    hi = x.astype(jnp.bfloat16)
    lo = (x - hi.astype(jnp.float32)).astype(jnp.bfloat16)
    return hi, lo


def _mlstm_kernel(qkf_ref, vf_ref, gf_ref, qkb_ref, vb_ref, gb_ref, hf_ref, hb_ref, c_ref, m_ref):
    @pl.when(pl.program_id(1) == 0)
    def _():
        c_ref[...] = jnp.zeros_like(c_ref)
        m_ref[...] = jnp.full_like(m_ref, -jnp.inf)

    rows = lax.broadcasted_iota(jnp.int32, (CHUNK, CHUNK), 0)
    cols = lax.broadcasted_iota(jnp.int32, (CHUNK, CHUNK), 1)
    le = rows <= cols
    ge = rows >= cols
    tri_le = jnp.where(le, 1.0, 0.0).astype(jnp.bfloat16)
    tri_ge = jnp.where(ge, 1.0, 0.0).astype(jnp.bfloat16)

    for e in range(qkf_ref.shape[0]):
        _mlstm_element(e, qkf_ref, vf_ref, gf_ref, qkb_ref, vb_ref, gb_ref, hf_ref, hb_ref, c_ref, m_ref,
                       le, ge, tri_le, tri_ge)


def _mlstm_element(e, qkf_ref, vf_ref, gf_ref, qkb_ref, vb_ref, gb_ref, hf_ref, hb_ref, c_ref, m_ref,
                   le, ge, tri_le, tri_ge):
    nst = N_DIR * M_HEADS
    lane = lax.broadcasted_iota(jnp.int32, (1, LANES), 1)
    fwd_lane = lane % nst < M_HEADS
    gates = jnp.where(fwd_lane, gf_ref[e], gb_ref[e])
    gates_t = gates.T

    i8 = gates_t[0:nst]
    lf8 = _log_sigmoid(gates_t[nst:2 * nst])
    pre8 = sum(jnp.dot(p, tri_le, preferred_element_type=jnp.float32) for p in _split2(lf8))
    tot8 = pre8[:, CHUNK - 1:CHUNK]
    is_fwd = lax.broadcasted_iota(jnp.int32, (nst, 1), 0) < M_HEADS
    b8 = jnp.where(is_fwd, pre8, tot8 - pre8 + lf8)
    r8 = i8 - b8
    g8 = tot8 + r8
    m_prev = m_ref[e]
    m_new = jnp.maximum(tot8 + m_prev, jnp.max(g8, axis=1, keepdims=True))
    a8 = jnp.exp(tot8 + m_prev - m_new)
    w8 = jnp.exp(g8 - m_new)
    inter8 = b8 + m_prev
    m_ref[e] = m_new

    ones_row = jnp.where(lax.broadcasted_iota(jnp.int32, (2 * SUBLANES, CHUNK), 0) == 0, 1.0, 0.0
                         ).astype(jnp.bfloat16)

    lf_c = _log_sigmoid(gates)
    pre_c = sum(jnp.dot(tri_ge, p, preferred_element_type=jnp.float32) for p in _split2(lf_c))
    b_c = jnp.where(fwd_lane, pre_c, pre_c[CHUNK - 1:CHUNK, :] - pre_c + lf_c)
    r_c = pltpu.roll(gates, nst, axis=1) - b_c

    for d, (qk_ref, v_ref, h_ref) in enumerate(((qkf_ref, vf_ref, hf_ref), (qkb_ref, vb_ref, hb_ref))):
        valid_t = le if d == 0 else ge

        heads = range(M_HEADS)
        rs = [d * M_HEADS + hd for hd in heads]
        qs = [qk_ref[e, :, hd * M_HEAD_DIM:(hd + 1) * M_HEAD_DIM] for hd in heads]
        ks = [qk_ref[e, :, M_WIDTH + hd * M_HEAD_DIM:M_WIDTH + (hd + 1) * M_HEAD_DIM] for hd in heads]
        vts = [jnp.concatenate([v_ref[e, :, hd * M_HEAD_DIM:(hd + 1) * M_HEAD_DIM].T,
                                ones_row], axis=0) for hd in heads]
        cs = [c_ref[e, r] for r in rs]

        qts = [q.T for q in qs]
        st = [jnp.dot(k, qt, preferred_element_type=jnp.float32)
              for k, qt in zip(ks, qts)]
        dt = [jnp.where(valid_t, r_c[:, nst + r:nst + r + 1] + b8[r:r + 1, :], -jnp.inf) for r in rs]
        m_t = [jnp.maximum(inter8[r:r + 1, :], jnp.max(x, axis=0, keepdims=True)) for r, x in zip(rs, dt)]
        w_i = [jnp.exp(inter8[r:r + 1, :] - m) for r, m in zip(rs, m_t)]
        pt = [s_ * jnp.exp(x - m) for s_, x, m in zip(st, dt, m_t)]
        both = [jnp.dot(jnp.concatenate([vt, c.astype(jnp.bfloat16)], axis=1),
                        jnp.concatenate([p, qt * w], axis=0).astype(jnp.bfloat16),
                        preferred_element_type=jnp.float32)
                for vt, c, p, qt, w in zip(vts, cs, pt, qts, w_i)]
        for hd in heads:
            den = both[hd][M_HEAD_DIM:M_HEAD_DIM + 1]
            ht = both[hd][:M_HEAD_DIM] / jnp.maximum(jnp.abs(den), jnp.exp(-m_t[hd]))
            h_ref[e, :, hd * M_HEAD_DIM:(hd + 1) * M_HEAD_DIM] = ht.T.astype(h_ref.dtype)
        for hd, r in zip(heads, rs):
            vw = (vts[hd] * w8[r:r + 1, :]).astype(jnp.bfloat16)
            c_ref[e, r] = a8[r:r + 1, :] * cs[hd] + jnp.dot(vw, ks[hd], preferred_element_type=jnp.float32)


def _mlstm(mqk, mv, gates, *, ctx_len):
    bsz, s_len, _ = mqk.shape
    nc = s_len // CHUNK
    ncc = ctx_len // CHUNK

    def fwd(b, j):
        return (b, j, 0)

    def bwd(b, j):
        return (b, jnp.where(j < ncc, ncc - 1 - j, nc - 1 + ncc - j), 0)

    nstate = N_DIR * M_HEADS
    nb = math.gcd(bsz, BATCH_PER_STEP)
    return pl.pallas_call(
        _mlstm_kernel,
        grid=(bsz // nb, nc),
        in_specs=[
            pl.BlockSpec((nb, CHUNK, 2 * M_WIDTH), fwd),
            pl.BlockSpec((nb, CHUNK, M_WIDTH), fwd),
            pl.BlockSpec((nb, CHUNK, LANES), fwd),
            pl.BlockSpec((nb, CHUNK, 2 * M_WIDTH), bwd),
            pl.BlockSpec((nb, CHUNK, M_WIDTH), bwd),
            pl.BlockSpec((nb, CHUNK, LANES), bwd),
        ],
        out_specs=[
            pl.BlockSpec((nb, CHUNK, M_WIDTH), fwd),
            pl.BlockSpec((nb, CHUNK, M_WIDTH), bwd),
        ],
        out_shape=(jax.ShapeDtypeStruct((bsz, s_len, M_WIDTH), jnp.bfloat16),
                   jax.ShapeDtypeStruct((bsz, s_len, M_WIDTH), jnp.bfloat16)),
        scratch_shapes=[
            pltpu.VMEM((nb, nstate, STATE_ROWS, M_HEAD_DIM), jnp.float32),
            pltpu.VMEM((nb, nstate, LANES), jnp.float32),
        ],
        compiler_params=pltpu.CompilerParams(
            dimension_semantics=("arbitrary", "arbitrary"), vmem_limit_bytes=VMEM_LIMIT),
        name="mlstm_bidirectional_scan",
    )(mqk, mv, gates, mqk, mv, gates)


def _out_kernel(x_ref, att_ref, hf_ref, hb_ref, moz_ref, hg_ref, w_ref, mod_ref, *rest, final):
    if final:
        fg_ref, o_ref = rest
    else:
        (o_ref,) = rest
    ne, tm, _ = x_ref.shape
    mxs = []
    for e in range(ne):
        mo = moz_ref[e, :, :M_WIDTH].astype(jnp.float32)
        mz = moz_ref[e, :, M_WIDTH:].astype(jnp.float32)
        hs = _sigmoid(mo) * (hf_ref[e].astype(jnp.float32) + hb_ref[e].astype(jnp.float32))
        parts = []
        for hd in range(M_HEADS):
            hh = hs[:, hd * M_HEAD_DIM:(hd + 1) * M_HEAD_DIM]
            ms = jnp.mean(hh * hh, axis=-1, keepdims=True)
            parts.append(hh * lax.rsqrt(ms + EPS))
        mxs.append((jnp.concatenate(parts, axis=1) * hg_ref[...] * _silu(mz)).astype(jnp.bfloat16))
    att = jnp.concatenate([att_ref[e] for e in range(ne)], axis=0)
    out = (jnp.dot(att, w_ref[:ATT_WIDTH, :], preferred_element_type=jnp.float32)
           + jnp.dot(jnp.concatenate(mxs, axis=0), w_ref[ATT_WIDTH:, :], preferred_element_type=jnp.float32))
    for e in range(ne):
        gate = mod_ref[e if mod_ref.shape[0] > 1 else 0, 2:3, :]
        xnew = x_ref[e] + gate * out[e * tm:(e + 1) * tm]
        if final:
            ms = jnp.mean(xnew * xnew, axis=-1, keepdims=True)
            o_ref[e] = xnew * lax.rsqrt(ms + EPS) * fg_ref[...]
        else:
            o_ref[e] = xnew


def _output(layer, hseg, att, hf, hb, moz, head_g, w_out, mod, final_g, *, row0, att_row0, is_ctx, final):
    bsz, seg_len, d = hseg.shape
    tm = ROW_TILE
    ne = math.gcd(bsz, PROJ_BATCH)
    t0 = row0 // tm
    a0 = att_row0 // tm

    def seg_map(b, j):
        return (b, j, 0)

    def row_map(b, j):
        return (b, j + t0, 0)

    if is_ctx:
        mod_spec = pl.BlockSpec((None, 1, 3, d), lambda b, j: (layer, mod.shape[1] - 1, 0, 0))
    else:
        mod_spec = pl.BlockSpec((None, ne, 3, d), lambda b, j: (layer, b, 0, 0))
    in_specs = [
        pl.BlockSpec((ne, tm, d), seg_map),
        pl.BlockSpec((ne, tm, ATT_WIDTH), lambda b, j: (b, j + a0, 0)),
        pl.BlockSpec((ne, tm, M_WIDTH), row_map),
        pl.BlockSpec((ne, tm, M_WIDTH), row_map),
        pl.BlockSpec((ne, tm, 2 * M_WIDTH), row_map),
        pl.BlockSpec((None, 1, M_WIDTH), lambda b, j: (layer, 0, 0)),
        pl.BlockSpec((None, d, d), lambda b, j: (layer, 0, 0), pipeline_mode=pl.Buffered(1)),
        mod_spec,
    ]
    args = [hseg, att, hf, hb, moz, head_g, w_out, mod]
    if final:
        in_specs.append(pl.BlockSpec((1, d), lambda b, j: (0, 0)))
        args.append(final_g)
    return pl.pallas_call(
        functools.partial(_out_kernel, final=final),
        grid=(bsz // ne, seg_len // tm),
        in_specs=in_specs,
        out_specs=pl.BlockSpec((ne, tm, d), seg_map),
        out_shape=jax.ShapeDtypeStruct(hseg.shape, jnp.float32),
        compiler_params=pltpu.CompilerParams(
            dimension_semantics=("arbitrary", "arbitrary"), vmem_limit_bytes=VMEM_LIMIT),
        name="gate_norm_out_projection",
    )(*args)


def kernel(x, c, ctx, c_ctx, w_ada, b_ada, norm_g, w_in, conv_w, conv_b, gate_b, sink, head_g, w_out, final_g):
    bsz, t_len, d = x.shape
    ctx_len = ctx.shape[1]
    depth = w_in.shape[0]
    assert d == D_MODEL and t_len % ROW_TILE == 0 and ctx_len % ROW_TILE == 0 and t_len % GRID_W == 0

    hc, hx = ctx, x
    mod_rows = -(-(bsz + 1) // SUBLANES) * SUBLANES
    cc = jnp.zeros((mod_rows, d), jnp.float32).at[:bsz].set(c).at[mod_rows - 1].set(c_ctx)
    mod = _modulation(cc, w_ada, b_ada).reshape(depth, mod_rows, 3, d)

    w_qk, w_main, w_vg = _split_w_in(w_in)
    w_o = w_out.astype(jnp.bfloat16)
    cos, sin = _rope_tables(t_len, ctx_len)
    kscale = jnp.concatenate([jnp.ones((1, M_WIDTH), jnp.float32),
                              jnp.full((1, M_WIDTH), M_HEAD_DIM ** -0.5, jnp.float32)], axis=1)
    gate_bias = jnp.pad(gate_b, ((0, 0), (0, LANES - N_GATES)))[:, None, :]
    norm_g3, conv_b3, head_g3 = norm_g[:, None, :], conv_b[:, None, :], head_g[:, None, :]

    for l in range(depth):
        last = l == depth - 1
        q, kx, v, az, mqk, mv, moz, gates = _projection(
            l, hc, hx, mod, norm_g3, w_qk, w_main, w_vg, cos, sin, conv_w, conv_b3, kscale, gate_bias)
        att = _attention(l, sink, q, kx, v, az, ctx_len=ctx_len, skip_ctx=last)
        hf, hb = _mlstm(mqk, mv, gates, ctx_len=ctx_len)
        out_args = (att, hf, hb, moz, head_g3, w_o, mod, final_g[None])
        if last:
            return _output(l, hx, *out_args, row0=ctx_len, att_row0=0, is_ctx=False, final=True)
        hc, hx = (_output(l, hc, *out_args, row0=0, att_row0=0, is_ctx=True, final=False),
                  _output(l, hx, *out_args, row0=ctx_len, att_row0=ctx_len, is_ctx=False, final=False))
```

```python
import functools
import math

import jax
import jax.numpy as jnp
from jax import lax
from jax.experimental import pallas as pl
from jax.experimental.pallas import tpu as pltpu

D_MODEL = 1024
GRID_W = 64
ATT_HEADS = 8
ATT_KV_HEADS = 2
ATT_GROUP = ATT_HEADS // ATT_KV_HEADS
ATT_HEAD_DIM = 64
ATT_WIDTH = ATT_HEADS * ATT_HEAD_DIM
KV_WIDTH = ATT_KV_HEADS * ATT_HEAD_DIM
WINDOW = 128
BLOCK = 128
ROPE_BASE = 10000.0
M_HEADS = 4
M_HEAD_DIM = 128
M_WIDTH = M_HEADS * M_HEAD_DIM
CONV_K = 5
CHUNK = 128
N_DIR = 2
N_GATES = 2 * N_DIR * M_HEADS
EPS = 1e-6

LANES = 128
SUBLANES = 8
HALO = SUBLANES
ROW_TILE = 256
BATCH_PER_STEP = 8
PROJ_BATCH = 4
PROJ_STACK = 1
VMEM_LIMIT = 56 * 1024 * 1024

W_IN_MAIN_START = ATT_WIDTH + 2 * KV_WIDTH
C_AZ = 0
C_MQK = C_AZ + ATT_WIDTH
C_MV = C_MQK + 2 * M_WIDTH
C_MOZ = C_MV + M_WIDTH
N_MAIN = C_MOZ + 2 * M_WIDTH
KV_GROUP_COLS = ATT_GROUP * ATT_HEAD_DIM
LOG2_E = 1.4426950408889634
Q_PRESCALE = ATT_HEAD_DIM ** -0.5 * LOG2_E


def _sigmoid(x):
    return 1.0 / (1.0 + jnp.exp(-x))


def _silu(x):
    return x * _sigmoid(x)


def _log_sigmoid(x):
    return jnp.minimum(x, 0.0) - jnp.log(1.0 + jnp.exp(-jnp.abs(x)))


def _split_w_in(w_in):
    depth, d, n_in = w_in.shape
    w_t = jnp.swapaxes(w_in, 1, 2)
    quarter = ATT_HEAD_DIM // 4
    k0 = ATT_WIDTH
    v0 = k0 + KV_WIDTH
    wq = w_t[:, :k0].reshape(depth, ATT_KV_HEADS, ATT_GROUP, 2, 2, quarter, d)
    wq = wq.transpose(0, 1, 4, 2, 3, 5, 6).reshape(depth, ATT_WIDTH, d)
    wk = w_t[:, k0:v0].reshape(depth, ATT_KV_HEADS, 1, 2, 2, quarter, d)
    wk = jnp.broadcast_to(wk.transpose(0, 1, 4, 2, 3, 5, 6),
                          (depth, ATT_KV_HEADS, 2, ATT_GROUP, 2, quarter, d)).reshape(depth, ATT_WIDTH, d)
    w_qk = jnp.swapaxes(jnp.concatenate([wq, wk], axis=1), 1, 2).astype(jnp.bfloat16)
    cast_rows = W_IN_MAIN_START
    v_blk = (W_IN_MAIN_START - KV_WIDTH) // KV_WIDTH
    g_blk = (W_IN_MAIN_START + N_MAIN) // N_GATES
    w_main, w_vg = pl.pallas_call(
        _cast_w_kernel,
        grid=(depth, N_MAIN // cast_rows),
        in_specs=[pl.BlockSpec((1, cast_rows, d), lambda l, r: (l, r + 1, 0)),
                  pl.BlockSpec((1, KV_WIDTH, d), lambda l, r: (l, v_blk, 0)),
                  pl.BlockSpec((1, N_GATES, d), lambda l, r: (l, g_blk, 0))],
        out_specs=[pl.BlockSpec((1, d, cast_rows), lambda l, r: (l, 0, r)),
                   pl.BlockSpec((1, d, KV_WIDTH + LANES), lambda l, r: (l, 0, 0))],
        out_shape=(jax.ShapeDtypeStruct((depth, d, N_MAIN), jnp.bfloat16),
                   jax.ShapeDtypeStruct((depth, d, KV_WIDTH + LANES), jnp.bfloat16)),
        compiler_params=pltpu.CompilerParams(dimension_semantics=("arbitrary", "arbitrary")),
        name="cast_projection_weights",
    )(w_t, w_t, w_t)
    return w_qk, w_main, w_vg


def _cast_w_kernel(wm_in, wv_in, wg_in, wm_ref, wvg_ref):
    wm_ref[0] = wm_in[0].T.astype(jnp.bfloat16)

    @pl.when(pl.program_id(1) == 0)
    def _():
        gates = jnp.concatenate([wg_in[0], jnp.zeros((LANES - N_GATES, wg_in.shape[2]), jnp.float32)], axis=0)
        wvg_ref[0, :, :KV_WIDTH] = wv_in[0].T.astype(jnp.bfloat16)
        wvg_ref[0, :, KV_WIDTH:] = gates.T.astype(jnp.bfloat16)


def _rope_tables(t_len, ctx_len):
    rows = t_len // GRID_W
    row = jnp.repeat(jnp.arange(rows), GRID_W).astype(jnp.float32)
    col = jnp.tile(jnp.arange(GRID_W), rows).astype(jnp.float32)
    half = ATT_HEAD_DIM // 2
    inv = ROPE_BASE ** (-jnp.arange(0, half, 2, dtype=jnp.float32) / half)
    ang = jnp.concatenate([row[:, None] * inv, col[:, None] * inv], axis=-1)
    ang = jnp.tile(ang, (1, LANES // half))
    cos = jnp.concatenate([jnp.ones((ctx_len, LANES), jnp.float32), jnp.cos(ang)], axis=0)
    sin = jnp.concatenate([jnp.zeros((ctx_len, LANES), jnp.float32), jnp.sin(ang)], axis=0)
    return cos, sin


def _mod_kernel(c_ref, w_ref, b_ref, o_ref):
    sc = _silu(c_ref[...])
    o_ref[0] = jnp.dot(sc.astype(jnp.bfloat16), w_ref[0].astype(jnp.bfloat16),
                       preferred_element_type=jnp.float32) + b_ref[0]


def _modulation(cc, w_ada, b_ada):
    depth, d, d3 = w_ada.shape
    rows = cc.shape[0]
    nt = d3 // d
    return pl.pallas_call(
        _mod_kernel,
        grid=(depth, nt),
        in_specs=[
            pl.BlockSpec((rows, d), lambda l, n: (0, 0)),
            pl.BlockSpec((1, d, d), lambda l, n: (l, 0, n)),
            pl.BlockSpec((1, 1, d), lambda l, n: (l, 0, n)),
        ],
        out_specs=pl.BlockSpec((1, rows, d), lambda l, n: (l, 0, n)),
        out_shape=jax.ShapeDtypeStruct((depth, rows, d3), jnp.float32),
        compiler_params=pltpu.CompilerParams(dimension_semantics=("arbitrary", "arbitrary")),
        name="adaln_modulation",
    )(cc, w_ada, b_ada.reshape(depth, 1, d3))


def _proj_kernel(*refs, tm, tiles_ctx, tiles_total):
    n_elem = refs[4].shape[0]
    for e0 in range(0, n_elem, PROJ_STACK):
        _proj_group(e0, min(PROJ_STACK, n_elem - e0), *refs, tm=tm, tiles_ctx=tiles_ctx, tiles_total=tiles_total)


def _proj_group(e0, ne, cp_ref, c_ref, cn_ref, xp_ref, x_ref, xn_ref, mod_ref, modc_ref, g_ref,
                wqk_ref, wm_ref, wvg_ref,
                cos_ref, sin_ref, cw_ref, cb_ref,
                ks_ref, gb_ref,
                q_ref, kx_ref, v_ref, az_ref, mqk_ref, mv_ref, moz_ref, gate_ref,
                *, tm, tiles_ctx, tiles_total):
    te = tm + 2 * HALO
    j = pl.program_id(1)
    is_ctx = j < tiles_ctx
    seg_first = jnp.logical_or(j == 0, j == tiles_ctx)
    seg_last = jnp.logical_or(j == tiles_ctx - 1, j == tiles_total - 1)

    ext = []
    for e in range(e0, e0 + ne):
        xe = jnp.concatenate([jnp.where(is_ctx, cp_ref[e], xp_ref[e]), jnp.where(is_ctx, c_ref[e], x_ref[e]),
                              jnp.where(is_ctx, cn_ref[e], xn_ref[e])], axis=0)
        ms = jnp.mean(xe * xe, axis=-1, keepdims=True)
        shift = jnp.where(is_ctx, modc_ref[0, 0:1, :], mod_ref[e, 0:1, :])
        scale = jnp.where(is_ctx, modc_ref[0, 1:2, :], mod_ref[e, 1:2, :])
        gain = g_ref[...] * (1.0 + scale)
        ext.append((xe * lax.rsqrt(ms + EPS) * gain + shift).astype(jnp.bfloat16))
    xn_ext = jnp.concatenate(ext, axis=0)
    xn = jnp.concatenate([x_[HALO:HALO + tm] for x_ in ext], axis=0)

    def proj(rows, w_ref, c0, c1):
        return jnp.dot(rows, w_ref[:, c0:c1], preferred_element_type=jnp.float32)

    row = lax.broadcasted_iota(jnp.int32, (te, 1), 0)
    keep = jnp.logical_and(jnp.logical_or(row >= HALO, jnp.logical_not(seg_first)),
                           jnp.logical_or(row < HALO + tm, jnp.logical_not(seg_last)))
    nb = tm // SUBLANES
    sub = lax.broadcasted_iota(jnp.int32, (1, SUBLANES, 1), 1)

    def conv_chunk(c0, width):
        y_all = proj(xn_ext, wm_ref, C_MQK + c0, C_MQK + c0 + width)
        for i, e in enumerate(range(e0, e0 + ne)):
            ye = jnp.where(keep, y_all[i * te:(i + 1) * te], 0.0).reshape(nb + 2, SUBLANES, width)
            acc = jnp.broadcast_to(cb_ref[:, c0:c0 + width].reshape(1, 1, width), (nb, SUBLANES, width))
            for t in range(CONV_K):
                delta = t - CONV_K // 2
                w_t = cw_ref[t:t + 1, c0:c0 + width].reshape(1, 1, width)
                if delta == 0:
                    tap = ye[1:nb + 1]
                else:
                    rot = pltpu.roll(ye, (-delta) % SUBLANES, axis=1)
                    if delta > 0:
                        tap = jnp.where(sub < SUBLANES - delta, rot[1:nb + 1], rot[2:nb + 2])
                    else:
                        tap = jnp.where(sub >= -delta, rot[1:nb + 1], rot[0:nb])
                acc = acc + w_t * tap
            acc = acc.reshape(tm, width)
            mqk_ref[e, :, c0:c0 + width] = (_silu(acc) * ks_ref[:, c0:c0 + width]).astype(jnp.bfloat16)

    def rope_group(grp):
        is_q = grp < ATT_KV_HEADS
        cos = cos_ref[...] * Q_PRESCALE if is_q else cos_ref[...]
        sin = sin_ref[...] * Q_PRESCALE if is_q else sin_ref[...]
        yq = proj(xn, wqk_ref, grp * KV_GROUP_COLS, (grp + 1) * KV_GROUP_COLS)
        dst = q_ref if is_q else kx_ref
        c0 = (grp % ATT_KV_HEADS) * KV_GROUP_COLS
        for i, e in enumerate(range(e0, e0 + ne)):
            first = yq[i * tm:(i + 1) * tm, :LANES]
            second = yq[i * tm:(i + 1) * tm, LANES:]
            dst[e, :, c0:c0 + LANES] = (first * cos - second * sin).astype(jnp.bfloat16)
            dst[e, :, c0 + LANES:c0 + 2 * LANES] = (second * cos + first * sin).astype(jnp.bfloat16)

    def plain(dst_ref, c0, width, off=0):
        y = proj(xn, wm_ref, c0, c0 + width)
        for i, e in enumerate(range(e0, e0 + ne)):
            dst_ref[e, :, off:off + width] = y[i * tm:(i + 1) * tm].astype(dst_ref.dtype)

    others = [functools.partial(rope_group, g) for g in range(2 * ATT_KV_HEADS)]
    others += [functools.partial(plain, az_ref, C_AZ, ATT_WIDTH),
               functools.partial(plain, mv_ref, C_MV, M_WIDTH),
               functools.partial(plain, moz_ref, C_MOZ, M_WIDTH),
               functools.partial(plain, moz_ref, C_MOZ + M_WIDTH, M_WIDTH, M_WIDTH)]
    per_chunk = 2
    cw = 2 * M_WIDTH * per_chunk // len(others)
    for i in range(len(others) // per_chunk):
        conv_chunk(i * cw, cw)
        for other in others[i * per_chunk:(i + 1) * per_chunk]:
            other()
    y = proj(xn, wvg_ref, 0, KV_WIDTH + LANES)
    for i, e in enumerate(range(e0, e0 + ne)):
        v_ref[e] = y[i * tm:(i + 1) * tm, :KV_WIDTH].astype(jnp.bfloat16)
        gate_ref[e] = y[i * tm:(i + 1) * tm, KV_WIDTH:] + gb_ref[...]


def _tile_with_halo_specs(ne, tm, d, n_rows, tile_of):
    hb = tm // HALO
    last = n_rows // HALO - 1
    return (pl.BlockSpec((ne, HALO, d), lambda b, j: (b, jnp.maximum(tile_of(j) * hb - 1, 0), 0)),
            pl.BlockSpec((ne, tm, d), lambda b, j: (b, tile_of(j), 0)),
            pl.BlockSpec((ne, HALO, d), lambda b, j: (b, jnp.minimum((tile_of(j) + 1) * hb, last), 0)))


def _projection(layer, hc, hx, mod, norm_g, w_qk, w_main, w_vg, cos, sin, conv_w, conv_b, kscale, gate_b):
    bsz, ctx_len, d = hc.shape
    s_len = ctx_len + hx.shape[1]
    tm = ROW_TILE
    ne = math.gcd(bsz, PROJ_BATCH)
    tiles_total = s_len // tm
    tiles_ctx = ctx_len // tm
    ctx_blk = mod.shape[1] - 1

    def row_map(b, j):
        return (b, j, 0)

    def per_layer(*block):
        return pl.BlockSpec((None,) + block, lambda b, j: (layer,) + (0,) * len(block))

    kernel = functools.partial(_proj_kernel, tm=tm, tiles_ctx=tiles_ctx, tiles_total=tiles_total)
    out_shapes = (
        jax.ShapeDtypeStruct((bsz, s_len, ATT_WIDTH), jnp.bfloat16),
        jax.ShapeDtypeStruct((bsz, s_len, ATT_WIDTH), jnp.bfloat16),
        jax.ShapeDtypeStruct((bsz, s_len, KV_WIDTH), jnp.bfloat16),
        jax.ShapeDtypeStruct((bsz, s_len, ATT_WIDTH), jnp.bfloat16),
        jax.ShapeDtypeStruct((bsz, s_len, 2 * M_WIDTH), jnp.bfloat16),
        jax.ShapeDtypeStruct((bsz, s_len, M_WIDTH), jnp.bfloat16),
        jax.ShapeDtypeStruct((bsz, s_len, 2 * M_WIDTH), jnp.bfloat16),
        jax.ShapeDtypeStruct((bsz, s_len, LANES), jnp.float32),
    )
    return pl.pallas_call(
        kernel,
        grid=(bsz // ne, tiles_total),
        in_specs=[
            *_tile_with_halo_specs(ne, tm, d, ctx_len, lambda j: jnp.minimum(j, tiles_ctx - 1)),
            *_tile_with_halo_specs(ne, tm, d, s_len - ctx_len, lambda j: jnp.maximum(j - tiles_ctx, 0)),
            pl.BlockSpec((None, ne, 3, d), lambda b, j: (layer, b, 0, 0)),
            pl.BlockSpec((None, 1, 3, d), lambda b, j: (layer, ctx_blk, 0, 0)),
            per_layer(1, d),
            pl.BlockSpec((None, d, 2 * ATT_WIDTH), lambda b, j: (layer, 0, 0), pipeline_mode=pl.Buffered(1)),
            pl.BlockSpec((None, d, N_MAIN), lambda b, j: (layer, 0, 0), pipeline_mode=pl.Buffered(1)),
            pl.BlockSpec((None, d, KV_WIDTH + LANES), lambda b, j: (layer, 0, 0), pipeline_mode=pl.Buffered(1)),
            pl.BlockSpec((tm, LANES), lambda b, j: (j, 0)),
            pl.BlockSpec((tm, LANES), lambda b, j: (j, 0)),
            per_layer(CONV_K, 2 * M_WIDTH),
            per_layer(1, 2 * M_WIDTH),
            pl.BlockSpec((1, 2 * M_WIDTH), lambda b, j: (0, 0)),
            per_layer(1, LANES),
        ],
        out_specs=[
            pl.BlockSpec((ne, tm, ATT_WIDTH), row_map),
            pl.BlockSpec((ne, tm, ATT_WIDTH), row_map),
            pl.BlockSpec((ne, tm, KV_WIDTH), row_map),
            pl.BlockSpec((ne, tm, ATT_WIDTH), row_map),
            pl.BlockSpec((ne, tm, 2 * M_WIDTH), row_map),
            pl.BlockSpec((ne, tm, M_WIDTH), row_map),
            pl.BlockSpec((ne, tm, 2 * M_WIDTH), row_map),
            pl.BlockSpec((ne, tm, LANES), row_map),
        ],
        out_shape=out_shapes,
        compiler_params=pltpu.CompilerParams(
            dimension_semantics=("arbitrary", "arbitrary"), vmem_limit_bytes=VMEM_LIMIT),
        name="norm_mod_projection",
    )(hc, hc, hc, hx, hx, hx, mod, mod, norm_g, w_qk, w_main, w_vg, cos, sin, conv_w, conv_b, kscale, gate_b)


def _attn_kernel(sink_ref, q_ref, *refs, layer, with_window, blk0, ctx_blocks, total_blocks):
    if with_window:
        kp_ref, kc_ref, kn_ref, kctx_ref, vp_ref, vc_ref, vn_ref, vctx_ref, az_ref, o_ref = refs
    else:
        kctx_ref, vctx_ref, az_ref, o_ref = refs
    ctx_len = ctx_blocks * BLOCK
    win = 3 * BLOCK if with_window else 0
    n_keys = win + ctx_len

    if with_window:
        n = pl.program_id(1) + blk0 - ctx_blocks
        lat_blocks = total_blocks - ctx_blocks
        i = lax.broadcasted_iota(jnp.int32, (BLOCK, win), 0)
        jx = lax.broadcasted_iota(jnp.int32, (BLOCK, win), 1)
        lo = jnp.where(n == 0, BLOCK, 0)
        hi = jnp.where(n == lat_blocks - 1, 2 * BLOCK, win)
        valid = (jx >= i) & (jx <= i + 2 * WINDOW) & (jx >= lo) & (jx < hi)
        bias = jnp.concatenate([jnp.where(valid, 0.0, -jnp.inf)] * ATT_GROUP, axis=0)

    lane_v = lax.broadcasted_iota(jnp.int32, (1, LANES), 1)
    low_half = lane_v < ATT_HEAD_DIM
    lane_q = lax.broadcasted_iota(jnp.int32, (1, KV_GROUP_COLS), 1)
    head_of_qlane = (lane_q % LANES) // (LANES // ATT_GROUP)
    row_head = lax.broadcasted_iota(jnp.int32, (ATT_GROUP * BLOCK, 1), 0) // BLOCK
    ones = jnp.ones((n_keys, LANES), jnp.bfloat16)

    for e_i in range(q_ref.shape[0]):
        if with_window:
            k_all = jnp.concatenate([kp_ref[e_i], kc_ref[e_i], kn_ref[e_i], kctx_ref[e_i]], axis=0)
            v_all = jnp.concatenate([vp_ref[e_i], vc_ref[e_i], vn_ref[e_i], vctx_ref[e_i]], axis=0)
        else:
            k_all, v_all = kctx_ref[e_i], vctx_ref[e_i]
        v_all = v_all.astype(jnp.float32)
        v_rot = pltpu.roll(v_all, ATT_HEAD_DIM, axis=1)
        for kvh in range(ATT_KV_HEADS):
            c0 = kvh * KV_GROUP_COLS
            qg = q_ref[e_i, :, c0:c0 + KV_GROUP_COLS]
            qm = jnp.concatenate(
                [jnp.where(head_of_qlane == g, qg, jnp.zeros_like(qg)) for g in range(ATT_GROUP)], axis=0)
            kx = k_all[:, c0:c0 + KV_GROUP_COLS]
            s = lax.dot_general(qm, kx, (((1,), (1,)), ((), ())), preferred_element_type=jnp.float32)
            sink = jnp.zeros((ATT_GROUP * BLOCK, 1), jnp.float32)
            for g in range(ATT_GROUP):
                sink = jnp.where(row_head == g, sink_ref[layer, kvh * ATT_GROUP + g] * LOG2_E, sink)
            if with_window:
                s_win = s[:, :win] + bias
                s_ctx = s[:, win:]
                m = jnp.maximum(jnp.maximum(jnp.max(s_win, axis=-1, keepdims=True),
                                            jnp.max(s_ctx, axis=-1, keepdims=True)), sink)
                e = jnp.concatenate([jnp.exp2(s_win - m), jnp.exp2(s_ctx - m)], axis=1).astype(jnp.bfloat16)
            else:
                m = jnp.maximum(jnp.max(s, axis=-1, keepdims=True), sink)
                e = jnp.exp2(s - m).astype(jnp.bfloat16)
            own_half = low_half if kvh == 0 else jnp.logical_not(low_half)
            v_two = jnp.where(own_half, v_all, v_rot).astype(jnp.bfloat16)
            res = jnp.dot(e, jnp.concatenate([v_two, ones], axis=1),
                          preferred_element_type=jnp.float32)
            o = res[:, :LANES] / (res[:, LANES:] + jnp.exp2(sink - m))
            az = az_ref[e_i, :, c0:c0 + KV_GROUP_COLS].astype(jnp.float32)
            for pair in range(ATT_GROUP // 2):
                both = jnp.where(low_half, o[2 * pair * BLOCK:(2 * pair + 1) * BLOCK],
                                 o[(2 * pair + 1) * BLOCK:(2 * pair + 2) * BLOCK])
                lo_col = c0 + pair * LANES
                o_ref[e_i, :, lo_col:lo_col + LANES] = (
                    both * _silu(az[:, pair * LANES:(pair + 1) * LANES])).astype(jnp.bfloat16)


def _attention(layer, sink, q, kx, v, az, *, ctx_len, with_window):
    bsz, s_len, _ = q.shape
    ctx_blocks = ctx_len // BLOCK
    total_blocks = s_len // BLOCK
    blk0 = ctx_blocks if with_window else 0
    n_blocks = total_blocks - ctx_blocks if with_window else ctx_blocks
    lo, hi = ctx_blocks, total_blocks - 1
    nb = math.gcd(bsz, BATCH_PER_STEP)

    def at(off):
        return lambda b, j: (b, jnp.clip(j + blk0 + off, lo, hi), 0)

    def cur(b, j):
        return (b, j + blk0, 0)

    def ctx(b, j):
        return (b, 0, 0)

    window_k = [pl.BlockSpec((nb, BLOCK, ATT_WIDTH), at(off)) for off in (-1, 0, 1)] if with_window else []
    window_v = [pl.BlockSpec((nb, BLOCK, KV_WIDTH), at(off)) for off in (-1, 0, 1)] if with_window else []
    n_win = len(window_k)
    kernel = functools.partial(_attn_kernel, layer=layer, with_window=with_window, blk0=blk0,
                               ctx_blocks=ctx_blocks, total_blocks=total_blocks)
    return pl.pallas_call(
        kernel,
        grid=(bsz // nb, n_blocks),
        in_specs=[
            pl.BlockSpec(memory_space=pltpu.SMEM),
            pl.BlockSpec((nb, BLOCK, ATT_WIDTH), cur),
            *window_k,
            pl.BlockSpec((nb, ctx_len, ATT_WIDTH), ctx),
            *window_v,
            pl.BlockSpec((nb, ctx_len, KV_WIDTH), ctx),
            pl.BlockSpec((nb, BLOCK, ATT_WIDTH), cur),
        ],
        out_specs=pl.BlockSpec((nb, BLOCK, ATT_WIDTH), lambda b, j: (b, j, 0)),
        out_shape=jax.ShapeDtypeStruct((bsz, n_blocks * BLOCK, ATT_WIDTH), jnp.bfloat16),
        compiler_params=pltpu.CompilerParams(
            dimension_semantics=("arbitrary", "arbitrary"), vmem_limit_bytes=VMEM_LIMIT),
        name="window_ctx_attention" if with_window else "ctx_attention",
    )(sink, q, *([kx] * (n_win + 1)), *([v] * (n_win + 1)), az)


STATE_ROWS = M_HEAD_DIM + 2 * SUBLANES


def _split2(x):
    hi = x.astype(jnp.bfloat16)
    lo = (x - hi.astype(jnp.float32)).astype(jnp.bfloat16)
    return hi, lo


def _mlstm_kernel(qkf_ref, vf_ref, gf_ref, qkb_ref, vb_ref, gb_ref, hf_ref, hb_ref, c_ref, m_ref):
    @pl.when(pl.program_id(1) == 0)
    def _():
        c_ref[...] = jnp.zeros_like(c_ref)
        m_ref[...] = jnp.full_like(m_ref, -jnp.inf)

    rows = lax.broadcasted_iota(jnp.int32, (CHUNK, CHUNK), 0)
    cols = lax.broadcasted_iota(jnp.int32, (CHUNK, CHUNK), 1)
    le = rows <= cols
    ge = rows >= cols
    tri_le = jnp.where(le, 1.0, 0.0).astype(jnp.bfloat16)
    tri_ge = jnp.where(ge, 1.0, 0.0).astype(jnp.bfloat16)

    for e in range(qkf_ref.shape[0]):
        _mlstm_element(e, qkf_ref, vf_ref, gf_ref, qkb_ref, vb_ref, gb_ref, hf_ref, hb_ref, c_ref, m_ref,
                       le, ge, tri_le, tri_ge)


def _mlstm_element(e, qkf_ref, vf_ref, gf_ref, qkb_ref, vb_ref, gb_ref, hf_ref, hb_ref, c_ref, m_ref,
                   le, ge, tri_le, tri_ge):
    nst = N_DIR * M_HEADS
    lane = lax.broadcasted_iota(jnp.int32, (1, LANES), 1)
    fwd_lane = lane % nst < M_HEADS
    gates = jnp.where(fwd_lane, gf_ref[e], gb_ref[e])
    gates_t = gates.T

    i8 = gates_t[0:nst]
    lf8 = _log_sigmoid(gates_t[nst:2 * nst])
    pre8 = sum(jnp.dot(p, tri_le, preferred_element_type=jnp.float32) for p in _split2(lf8))
    tot8 = pre8[:, CHUNK - 1:CHUNK]
    is_fwd = lax.broadcasted_iota(jnp.int32, (nst, 1), 0) < M_HEADS
    b8 = jnp.where(is_fwd, pre8, tot8 - pre8 + lf8)
    r8 = i8 - b8
    g8 = tot8 + r8
    m_prev = m_ref[e]
    m_new = jnp.maximum(tot8 + m_prev, jnp.max(g8, axis=1, keepdims=True))
    a8 = jnp.exp(tot8 + m_prev - m_new)
    w8 = jnp.exp(g8 - m_new)
    inter8 = b8 + m_prev
    m_ref[e] = m_new

    ones_row = jnp.where(lax.broadcasted_iota(jnp.int32, (2 * SUBLANES, CHUNK), 0) == 0, 1.0, 0.0
                         ).astype(jnp.bfloat16)

    lf_c = _log_sigmoid(gates)
    pre_c = sum(jnp.dot(tri_ge, p, preferred_element_type=jnp.float32) for p in _split2(lf_c))
    b_c = jnp.where(fwd_lane, pre_c, pre_c[CHUNK - 1:CHUNK, :] - pre_c + lf_c)
    r_c = pltpu.roll(gates, nst, axis=1) - b_c

    for d, (qk_ref, v_ref, h_ref) in enumerate(((qkf_ref, vf_ref, hf_ref), (qkb_ref, vb_ref, hb_ref))):
        valid_t = le if d == 0 else ge

        heads = range(M_HEADS)
        rs = [d * M_HEADS + hd for hd in heads]
        qs = [qk_ref[e, :, hd * M_HEAD_DIM:(hd + 1) * M_HEAD_DIM] for hd in heads]
        ks = [qk_ref[e, :, M_WIDTH + hd * M_HEAD_DIM:M_WIDTH + (hd + 1) * M_HEAD_DIM] for hd in heads]
        vts = [jnp.concatenate([v_ref[e, :, hd * M_HEAD_DIM:(hd + 1) * M_HEAD_DIM].T,
                                ones_row], axis=0) for hd in heads]
        cs = [c_ref[e, r] for r in rs]

        qts = [q.T for q in qs]
        st = [jnp.dot(k, qt, preferred_element_type=jnp.float32)
              for k, qt in zip(ks, qts)]
        dt = [jnp.where(valid_t, r_c[:, nst + r:nst + r + 1] + b8[r:r + 1, :], -jnp.inf) for r in rs]
        m_t = [jnp.maximum(inter8[r:r + 1, :], jnp.max(x, axis=0, keepdims=True)) for r, x in zip(rs, dt)]
        w_i = [jnp.exp(inter8[r:r + 1, :] - m) for r, m in zip(rs, m_t)]
        pt = [s_ * jnp.exp(x - m) for s_, x, m in zip(st, dt, m_t)]
        both = [jnp.dot(jnp.concatenate([vt, c.astype(jnp.bfloat16)], axis=1),
                        jnp.concatenate([p, qt * w], axis=0).astype(jnp.bfloat16),
                        preferred_element_type=jnp.float32)
                for vt, c, p, qt, w in zip(vts, cs, pt, qts, w_i)]
        for hd in heads:
            den = both[hd][M_HEAD_DIM:M_HEAD_DIM + 1]
            ht = both[hd][:M_HEAD_DIM] / jnp.maximum(jnp.abs(den), jnp.exp(-m_t[hd]))
            h_ref[e, :, hd * M_HEAD_DIM:(hd + 1) * M_HEAD_DIM] = ht.T.astype(h_ref.dtype)
        for hd, r in zip(heads, rs):
            vw = (vts[hd] * w8[r:r + 1, :]).astype(jnp.bfloat16)
            c_ref[e, r] = a8[r:r + 1, :] * cs[hd] + jnp.dot(vw, ks[hd], preferred_element_type=jnp.float32)


def _mlstm(mqk, mv, gates, *, ctx_len):
    bsz, s_len, _ = mqk.shape
    nc = s_len // CHUNK
    ncc = ctx_len // CHUNK

    def fwd(b, j):
        return (b, j, 0)

    def bwd(b, j):
        return (b, jnp.where(j < ncc, ncc - 1 - j, nc - 1 + ncc - j), 0)

    nstate = N_DIR * M_HEADS
    nb = math.gcd(bsz, BATCH_PER_STEP)
    return pl.pallas_call(
        _mlstm_kernel,
        grid=(bsz // nb, nc),
        in_specs=[
            pl.BlockSpec((nb, CHUNK, 2 * M_WIDTH), fwd),
            pl.BlockSpec((nb, CHUNK, M_WIDTH), fwd),
            pl.BlockSpec((nb, CHUNK, LANES), fwd),
            pl.BlockSpec((nb, CHUNK, 2 * M_WIDTH), bwd),
            pl.BlockSpec((nb, CHUNK, M_WIDTH), bwd),
            pl.BlockSpec((nb, CHUNK, LANES), bwd),
        ],
        out_specs=[
            pl.BlockSpec((nb, CHUNK, M_WIDTH), fwd),
            pl.BlockSpec((nb, CHUNK, M_WIDTH), bwd),
        ],
        out_shape=(jax.ShapeDtypeStruct((bsz, s_len, M_WIDTH), jnp.bfloat16),
                   jax.ShapeDtypeStruct((bsz, s_len, M_WIDTH), jnp.bfloat16)),
        scratch_shapes=[
            pltpu.VMEM((nb, nstate, STATE_ROWS, M_HEAD_DIM), jnp.float32),
            pltpu.VMEM((nb, nstate, LANES), jnp.float32),
        ],
        compiler_params=pltpu.CompilerParams(
            dimension_semantics=("arbitrary", "arbitrary"), vmem_limit_bytes=VMEM_LIMIT),
        name="mlstm_bidirectional_scan",
    )(mqk, mv, gates, mqk, mv, gates)


def _out_kernel(x_ref, att_ref, hf_ref, hb_ref, moz_ref, hg_ref, w_ref, mod_ref, *rest, final):
    if final:
        fg_ref, o_ref = rest
    else:
        (o_ref,) = rest
    ne, tm, _ = x_ref.shape
    mxs = []
    for e in range(ne):
        mo = moz_ref[e, :, :M_WIDTH].astype(jnp.float32)
        mz = moz_ref[e, :, M_WIDTH:].astype(jnp.float32)
        hs = _sigmoid(mo) * (hf_ref[e].astype(jnp.float32) + hb_ref[e].astype(jnp.float32))
        parts = []
        for hd in range(M_HEADS):
            hh = hs[:, hd * M_HEAD_DIM:(hd + 1) * M_HEAD_DIM]
            ms = jnp.mean(hh * hh, axis=-1, keepdims=True)
            parts.append(hh * lax.rsqrt(ms + EPS))
        mxs.append((jnp.concatenate(parts, axis=1) * hg_ref[...] * _silu(mz)).astype(jnp.bfloat16))
    att = jnp.concatenate([att_ref[e] for e in range(ne)], axis=0)
    out = (jnp.dot(att, w_ref[:ATT_WIDTH, :], preferred_element_type=jnp.float32)
           + jnp.dot(jnp.concatenate(mxs, axis=0), w_ref[ATT_WIDTH:, :], preferred_element_type=jnp.float32))
    for e in range(ne):
        gate = mod_ref[e if mod_ref.shape[0] > 1 else 0, 2:3, :]
        xnew = x_ref[e] + gate * out[e * tm:(e + 1) * tm]
        if final:
            ms = jnp.mean(xnew * xnew, axis=-1, keepdims=True)
            o_ref[e] = xnew * lax.rsqrt(ms + EPS) * fg_ref[...]
        else:
            o_ref[e] = xnew


def _output(layer, hseg, att, hf, hb, moz, head_g, w_out, mod, final_g, *, row0, is_ctx, final):
    bsz, seg_len, d = hseg.shape
    tm = ROW_TILE
    ne = math.gcd(bsz, PROJ_BATCH)
    t0 = row0 // tm

    def seg_map(b, j):
        return (b, j, 0)

    def row_map(b, j):
        return (b, j + t0, 0)

    if is_ctx:
        mod_spec = pl.BlockSpec((None, 1, 3, d), lambda b, j: (layer, mod.shape[1] - 1, 0, 0))
    else:
        mod_spec = pl.BlockSpec((None, ne, 3, d), lambda b, j: (layer, b, 0, 0))
    in_specs = [
        pl.BlockSpec((ne, tm, d), seg_map),
        pl.BlockSpec((ne, tm, ATT_WIDTH), seg_map),
        pl.BlockSpec((ne, tm, M_WIDTH), row_map),
        pl.BlockSpec((ne, tm, M_WIDTH), row_map),
        pl.BlockSpec((ne, tm, 2 * M_WIDTH), row_map),
        pl.BlockSpec((None, 1, M_WIDTH), lambda b, j: (layer, 0, 0)),
        pl.BlockSpec((None, d, d), lambda b, j: (layer, 0, 0), pipeline_mode=pl.Buffered(1)),
        mod_spec,
    ]
    args = [hseg, att, hf, hb, moz, head_g, w_out, mod]
    if final:
        in_specs.append(pl.BlockSpec((1, d), lambda b, j: (0, 0)))
        args.append(final_g)
    return pl.pallas_call(
        functools.partial(_out_kernel, final=final),
        grid=(bsz // ne, seg_len // tm),
        in_specs=in_specs,
        out_specs=pl.BlockSpec((ne, tm, d), seg_map),
        out_shape=jax.ShapeDtypeStruct(hseg.shape, jnp.float32),
        compiler_params=pltpu.CompilerParams(
            dimension_semantics=("arbitrary", "arbitrary"), vmem_limit_bytes=VMEM_LIMIT),
        name="gate_norm_out_projection",
    )(*args)


def kernel(x, c, ctx, c_ctx, w_ada, b_ada, norm_g, w_in, conv_w, conv_b, gate_b, sink, head_g, w_out, final_g):
    bsz, t_len, d = x.shape
    ctx_len = ctx.shape[1]
    depth = w_in.shape[0]
    assert d == D_MODEL and t_len % ROW_TILE == 0 and ctx_len % ROW_TILE == 0 and t_len % GRID_W == 0

    hc, hx = ctx, x
    mod_rows = -(-(bsz + 1) // SUBLANES) * SUBLANES
    cc = jnp.zeros((mod_rows, d), jnp.float32).at[:bsz].set(c).at[mod_rows - 1].set(c_ctx)
    mod = _modulation(cc, w_ada, b_ada).reshape(depth, mod_rows, 3, d)

    w_qk, w_main, w_vg = _split_w_in(w_in)
    w_o = w_out.astype(jnp.bfloat16)
    cos, sin = _rope_tables(t_len, ctx_len)
    kscale = jnp.concatenate([jnp.ones((1, M_WIDTH), jnp.float32),
                              jnp.full((1, M_WIDTH), M_HEAD_DIM ** -0.5, jnp.float32)], axis=1)
    gate_bias = jnp.pad(gate_b, ((0, 0), (0, LANES - N_GATES)))[:, None, :]
    norm_g3, conv_b3, head_g3 = norm_g[:, None, :], conv_b[:, None, :], head_g[:, None, :]

    for l in range(depth):
        last = l == depth - 1
        q, kx, v, az, mqk, mv, moz, gates = _projection(
            l, hc, hx, mod, norm_g3, w_qk, w_main, w_vg, cos, sin, conv_w, conv_b3, kscale, gate_bias)
        att_x = _attention(l, sink, q, kx, v, az, ctx_len=ctx_len, with_window=True)
        hf, hb = _mlstm(mqk, mv, gates, ctx_len=ctx_len)
        out_args = (hf, hb, moz, head_g3, w_o, mod, final_g[None])
        if last:
            return _output(l, hx, att_x, *out_args, row0=ctx_len, is_ctx=False, final=True)
        att_c = _attention(l, sink, q, kx, v, az, ctx_len=ctx_len, with_window=False)
        hc, hx = (_output(l, hc, att_c, *out_args, row0=0, is_ctx=True, final=False),
                  _output(l, hx, att_x, *out_args, row0=ctx_len, is_ctx=False, final=False))
```

```python
import functools
import math

import jax
import jax.numpy as jnp
from jax import lax
from jax.experimental import pallas as pl
from jax.experimental.pallas import tpu as pltpu

D_MODEL = 1024
GRID_W = 64
ATT_HEADS = 8
ATT_KV_HEADS = 2
ATT_GROUP = ATT_HEADS // ATT_KV_HEADS
ATT_HEAD_DIM = 64
ATT_WIDTH = ATT_HEADS * ATT_HEAD_DIM
KV_WIDTH = ATT_KV_HEADS * ATT_HEAD_DIM
WINDOW = 128
BLOCK = 128
ROPE_BASE = 10000.0
M_HEADS = 4
M_HEAD_DIM = 128
M_WIDTH = M_HEADS * M_HEAD_DIM
CONV_K = 5
CHUNK = 128
N_DIR = 2
N_GATES = 2 * N_DIR * M_HEADS
EPS = 1e-6

LANES = 128
SUBLANES = 8
HALO = SUBLANES
ROW_TILE = 256
BATCH_PER_STEP = 8
PROJ_BATCH = 4
PROJ_STACK = 1
VMEM_LIMIT = 56 * 1024 * 1024

W_IN_MAIN_START = ATT_WIDTH + 2 * KV_WIDTH
C_AZ = 0
C_MQK = C_AZ + ATT_WIDTH
C_MV = C_MQK + 2 * M_WIDTH
C_MOZ = C_MV + M_WIDTH
N_MAIN = C_MOZ + 2 * M_WIDTH
KV_GROUP_COLS = ATT_GROUP * ATT_HEAD_DIM
LOG2_E = 1.4426950408889634
Q_PRESCALE = ATT_HEAD_DIM ** -0.5 * LOG2_E


def _sigmoid(x):
    return 1.0 / (1.0 + jnp.exp(-x))


def _silu(x):
    return x * _sigmoid(x)


def _log_sigmoid(x):
    return jnp.minimum(x, 0.0) - jnp.log(1.0 + jnp.exp(-jnp.abs(x)))


def _split_w_in(w_in):
    depth, d, n_in = w_in.shape
    w_t = jnp.swapaxes(w_in, 1, 2)
    quarter = ATT_HEAD_DIM // 4
    k0 = ATT_WIDTH
    v0 = k0 + KV_WIDTH
    wq = w_t[:, :k0].reshape(depth, ATT_KV_HEADS, ATT_GROUP, 2, 2, quarter, d)
    wq = wq.transpose(0, 1, 4, 2, 3, 5, 6).reshape(depth, ATT_WIDTH, d)
    wk = w_t[:, k0:v0].reshape(depth, ATT_KV_HEADS, 1, 2, 2, quarter, d)
    wk = jnp.broadcast_to(wk.transpose(0, 1, 4, 2, 3, 5, 6),
                          (depth, ATT_KV_HEADS, 2, ATT_GROUP, 2, quarter, d)).reshape(depth, ATT_WIDTH, d)
    w_qk = jnp.swapaxes(jnp.concatenate([wq, wk], axis=1), 1, 2).astype(jnp.bfloat16)
    cast_rows = W_IN_MAIN_START
    v_blk = (W_IN_MAIN_START - KV_WIDTH) // KV_WIDTH
    g_blk = (W_IN_MAIN_START + N_MAIN) // N_GATES
    w_main, w_vg = pl.pallas_call(
        _cast_w_kernel,
        grid=(depth, N_MAIN // cast_rows),
        in_specs=[pl.BlockSpec((1, cast_rows, d), lambda l, r: (l, r + 1, 0)),
                  pl.BlockSpec((1, KV_WIDTH, d), lambda l, r: (l, v_blk, 0)),
                  pl.BlockSpec((1, N_GATES, d), lambda l, r: (l, g_blk, 0))],
        out_specs=[pl.BlockSpec((1, d, cast_rows), lambda l, r: (l, 0, r)),
                   pl.BlockSpec((1, d, KV_WIDTH + LANES), lambda l, r: (l, 0, 0))],
        out_shape=(jax.ShapeDtypeStruct((depth, d, N_MAIN), jnp.bfloat16),
                   jax.ShapeDtypeStruct((depth, d, KV_WIDTH + LANES), jnp.bfloat16)),
        compiler_params=pltpu.CompilerParams(dimension_semantics=("arbitrary", "arbitrary")),
        name="cast_projection_weights",
    )(w_t, w_t, w_t)
    return w_qk, w_main, w_vg


def _cast_w_kernel(wm_in, wv_in, wg_in, wm_ref, wvg_ref):
    wm_ref[0] = wm_in[0].T.astype(jnp.bfloat16)

    @pl.when(pl.program_id(1) == 0)
    def _():
        gates = jnp.concatenate([wg_in[0], jnp.zeros((LANES - N_GATES, wg_in.shape[2]), jnp.float32)], axis=0)
        wvg_ref[0, :, :KV_WIDTH] = wv_in[0].T.astype(jnp.bfloat16)
        wvg_ref[0, :, KV_WIDTH:] = gates.T.astype(jnp.bfloat16)


def _rope_tables(t_len, ctx_len):
    rows = t_len // GRID_W
    row = jnp.repeat(jnp.arange(rows), GRID_W).astype(jnp.float32)
    col = jnp.tile(jnp.arange(GRID_W), rows).astype(jnp.float32)
    half = ATT_HEAD_DIM // 2
    inv = ROPE_BASE ** (-jnp.arange(0, half, 2, dtype=jnp.float32) / half)
    ang = jnp.concatenate([row[:, None] * inv, col[:, None] * inv], axis=-1)
    ang = jnp.tile(ang, (1, LANES // half))
    cos = jnp.concatenate([jnp.ones((ctx_len, LANES), jnp.float32), jnp.cos(ang)], axis=0)
    sin = jnp.concatenate([jnp.zeros((ctx_len, LANES), jnp.float32), jnp.sin(ang)], axis=0)
    return cos, sin


def _mod_kernel(c_ref, w_ref, b_ref, o_ref):
    sc = _silu(c_ref[...])
    o_ref[0] = jnp.dot(sc.astype(jnp.bfloat16), w_ref[0].astype(jnp.bfloat16),
                       preferred_element_type=jnp.float32) + b_ref[0]


def _modulation(cc, w_ada, b_ada):
    depth, d, d3 = w_ada.shape
    rows = cc.shape[0]
    nt = d3 // d
    return pl.pallas_call(
        _mod_kernel,
        grid=(depth, nt),
        in_specs=[
            pl.BlockSpec((rows, d), lambda l, n: (0, 0)),
            pl.BlockSpec((1, d, d), lambda l, n: (l, 0, n)),
            pl.BlockSpec((1, 1, d), lambda l, n: (l, 0, n)),
        ],
        out_specs=pl.BlockSpec((1, rows, d), lambda l, n: (l, 0, n)),
        out_shape=jax.ShapeDtypeStruct((depth, rows, d3), jnp.float32),
        compiler_params=pltpu.CompilerParams(dimension_semantics=("arbitrary", "arbitrary")),
        name="adaln_modulation",
    )(cc, w_ada, b_ada.reshape(depth, 1, d3))


def _proj_kernel(*refs, tm, tiles_ctx, tiles_total):
    n_elem = refs[4].shape[0]
    for e0 in range(0, n_elem, PROJ_STACK):
        _proj_group(e0, min(PROJ_STACK, n_elem - e0), *refs, tm=tm, tiles_ctx=tiles_ctx, tiles_total=tiles_total)


def _proj_group(e0, ne, cp_ref, c_ref, cn_ref, xp_ref, x_ref, xn_ref, mod_ref, modc_ref, g_ref,
                wqk_ref, wm_ref, wvg_ref,
                cos_ref, sin_ref, cw_ref, cb_ref,
                ks_ref, gb_ref,
                q_ref, kx_ref, v_ref, az_ref, mqk_ref, mv_ref, moz_ref, gate_ref,
                *, tm, tiles_ctx, tiles_total):
    te = tm + 2 * HALO
    j = pl.program_id(1)
    is_ctx = j < tiles_ctx
    seg_first = jnp.logical_or(j == 0, j == tiles_ctx)
    seg_last = jnp.logical_or(j == tiles_ctx - 1, j == tiles_total - 1)

    ext = []
    for e in range(e0, e0 + ne):
        xe = jnp.concatenate([jnp.where(is_ctx, cp_ref[e], xp_ref[e]), jnp.where(is_ctx, c_ref[e], x_ref[e]),
                              jnp.where(is_ctx, cn_ref[e], xn_ref[e])], axis=0)
        ms = jnp.mean(xe * xe, axis=-1, keepdims=True)
        shift = jnp.where(is_ctx, modc_ref[0, 0:1, :], mod_ref[e, 0:1, :])
        scale = jnp.where(is_ctx, modc_ref[0, 1:2, :], mod_ref[e, 1:2, :])
        gain = g_ref[...] * (1.0 + scale)
        ext.append((xe * lax.rsqrt(ms + EPS) * gain + shift).astype(jnp.bfloat16))
    xn_ext = jnp.concatenate(ext, axis=0)
    xn = jnp.concatenate([x_[HALO:HALO + tm] for x_ in ext], axis=0)

    def proj(rows, w_ref, c0, c1):
        return jnp.dot(rows, w_ref[:, c0:c1], preferred_element_type=jnp.float32)

    row = lax.broadcasted_iota(jnp.int32, (te, 1), 0)
    keep = jnp.logical_and(jnp.logical_or(row >= HALO, jnp.logical_not(seg_first)),
                           jnp.logical_or(row < HALO + tm, jnp.logical_not(seg_last)))
    nb = tm // SUBLANES
    sub = lax.broadcasted_iota(jnp.int32, (1, SUBLANES, 1), 1)

    def conv_chunk(c0, width):
        y_all = proj(xn_ext, wm_ref, C_MQK + c0, C_MQK + c0 + width)
        for i, e in enumerate(range(e0, e0 + ne)):
            ye = jnp.where(keep, y_all[i * te:(i + 1) * te], 0.0).reshape(nb + 2, SUBLANES, width)
            acc = jnp.broadcast_to(cb_ref[:, c0:c0 + width].reshape(1, 1, width), (nb, SUBLANES, width))
            for t in range(CONV_K):
                delta = t - CONV_K // 2
                w_t = cw_ref[t:t + 1, c0:c0 + width].reshape(1, 1, width)
                if delta == 0:
                    tap = ye[1:nb + 1]
                else:
                    rot = pltpu.roll(ye, (-delta) % SUBLANES, axis=1)
                    if delta > 0:
                        tap = jnp.where(sub < SUBLANES - delta, rot[1:nb + 1], rot[2:nb + 2])
                    else:
                        tap = jnp.where(sub >= -delta, rot[1:nb + 1], rot[0:nb])
                acc = acc + w_t * tap
            acc = acc.reshape(tm, width)
            mqk_ref[e, :, c0:c0 + width] = (_silu(acc) * ks_ref[:, c0:c0 + width]).astype(jnp.bfloat16)

    def rope_group(grp):
        is_q = grp < ATT_KV_HEADS
        cos = cos_ref[...] * Q_PRESCALE if is_q else cos_ref[...]
        sin = sin_ref[...] * Q_PRESCALE if is_q else sin_ref[...]
        yq = proj(xn, wqk_ref, grp * KV_GROUP_COLS, (grp + 1) * KV_GROUP_COLS)
        dst = q_ref if is_q else kx_ref
        c0 = (grp % ATT_KV_HEADS) * KV_GROUP_COLS
        for i, e in enumerate(range(e0, e0 + ne)):
            first = yq[i * tm:(i + 1) * tm, :LANES]
            second = yq[i * tm:(i + 1) * tm, LANES:]
            dst[e, :, c0:c0 + LANES] = (first * cos - second * sin).astype(jnp.bfloat16)
            dst[e, :, c0 + LANES:c0 + 2 * LANES] = (second * cos + first * sin).astype(jnp.bfloat16)

    def plain(dst_ref, c0, width, off=0):
        y = proj(xn, wm_ref, c0, c0 + width)
        for i, e in enumerate(range(e0, e0 + ne)):
            dst_ref[e, :, off:off + width] = y[i * tm:(i + 1) * tm].astype(dst_ref.dtype)

    others = [functools.partial(rope_group, g) for g in range(2 * ATT_KV_HEADS)]
    others += [functools.partial(plain, az_ref, C_AZ, ATT_WIDTH),
               functools.partial(plain, mv_ref, C_MV, M_WIDTH),
               functools.partial(plain, moz_ref, C_MOZ, M_WIDTH),
               functools.partial(plain, moz_ref, C_MOZ + M_WIDTH, M_WIDTH, M_WIDTH)]
    per_chunk = 2
    cw = 2 * M_WIDTH * per_chunk // len(others)
    for i in range(len(others) // per_chunk):
        conv_chunk(i * cw, cw)
        for other in others[i * per_chunk:(i + 1) * per_chunk]:
            other()
    y = proj(xn, wvg_ref, 0, KV_WIDTH + LANES)
    for i, e in enumerate(range(e0, e0 + ne)):
        v_ref[e] = y[i * tm:(i + 1) * tm, :KV_WIDTH].astype(jnp.bfloat16)
        gate_ref[e] = y[i * tm:(i + 1) * tm, KV_WIDTH:] + gb_ref[...]


def _tile_with_halo_specs(ne, tm, d, n_rows, tile_of):
    hb = tm // HALO
    last = n_rows // HALO - 1
    return (pl.BlockSpec((ne, HALO, d), lambda b, j: (b, jnp.maximum(tile_of(j) * hb - 1, 0), 0)),
            pl.BlockSpec((ne, tm, d), lambda b, j: (b, tile_of(j), 0)),
            pl.BlockSpec((ne, HALO, d), lambda b, j: (b, jnp.minimum((tile_of(j) + 1) * hb, last), 0)))


def _projection(layer, hc, hx, mod, norm_g, w_qk, w_main, w_vg, cos, sin, conv_w, conv_b, kscale, gate_b):
    bsz, ctx_len, d = hc.shape
    s_len = ctx_len + hx.shape[1]
    tm = ROW_TILE
    ne = math.gcd(bsz, PROJ_BATCH)
    tiles_total = s_len // tm
    tiles_ctx = ctx_len // tm
    ctx_blk = mod.shape[1] - 1

    def row_map(b, j):
        return (b, j, 0)

    def per_layer(*block):
        return pl.BlockSpec((None,) + block, lambda b, j: (layer,) + (0,) * len(block))

    kernel = functools.partial(_proj_kernel, tm=tm, tiles_ctx=tiles_ctx, tiles_total=tiles_total)
    out_shapes = (
        jax.ShapeDtypeStruct((bsz, s_len, ATT_WIDTH), jnp.bfloat16),
        jax.ShapeDtypeStruct((bsz, s_len, ATT_WIDTH), jnp.bfloat16),
        jax.ShapeDtypeStruct((bsz, s_len, KV_WIDTH), jnp.bfloat16),
        jax.ShapeDtypeStruct((bsz, s_len, ATT_WIDTH), jnp.bfloat16),
        jax.ShapeDtypeStruct((bsz, s_len, 2 * M_WIDTH), jnp.bfloat16),
        jax.ShapeDtypeStruct((bsz, s_len, M_WIDTH), jnp.bfloat16),
        jax.ShapeDtypeStruct((bsz, s_len, 2 * M_WIDTH), jnp.bfloat16),
        jax.ShapeDtypeStruct((bsz, s_len, LANES), jnp.float32),
    )
    return pl.pallas_call(
        kernel,
        grid=(bsz // ne, tiles_total),
        in_specs=[
            *_tile_with_halo_specs(ne, tm, d, ctx_len, lambda j: jnp.minimum(j, tiles_ctx - 1)),
            *_tile_with_halo_specs(ne, tm, d, s_len - ctx_len, lambda j: jnp.maximum(j - tiles_ctx, 0)),
            pl.BlockSpec((None, ne, 3, d), lambda b, j: (layer, b, 0, 0)),
            pl.BlockSpec((None, 1, 3, d), lambda b, j: (layer, ctx_blk, 0, 0)),
            per_layer(1, d),
            pl.BlockSpec((None, d, 2 * ATT_WIDTH), lambda b, j: (layer, 0, 0), pipeline_mode=pl.Buffered(1)),
            pl.BlockSpec((None, d, N_MAIN), lambda b, j: (layer, 0, 0), pipeline_mode=pl.Buffered(1)),
            pl.BlockSpec((None, d, KV_WIDTH + LANES), lambda b, j: (layer, 0, 0), pipeline_mode=pl.Buffered(1)),
            pl.BlockSpec((tm, LANES), lambda b, j: (j, 0)),
            pl.BlockSpec((tm, LANES), lambda b, j: (j, 0)),
            per_layer(CONV_K, 2 * M_WIDTH),
            per_layer(1, 2 * M_WIDTH),
            pl.BlockSpec((1, 2 * M_WIDTH), lambda b, j: (0, 0)),
            per_layer(1, LANES),
        ],
        out_specs=[
            pl.BlockSpec((ne, tm, ATT_WIDTH), row_map),
            pl.BlockSpec((ne, tm, ATT_WIDTH), row_map),
            pl.BlockSpec((ne, tm, KV_WIDTH), row_map),
            pl.BlockSpec((ne, tm, ATT_WIDTH), row_map),
            pl.BlockSpec((ne, tm, 2 * M_WIDTH), row_map),
            pl.BlockSpec((ne, tm, M_WIDTH), row_map),
            pl.BlockSpec((ne, tm, 2 * M_WIDTH), row_map),
            pl.BlockSpec((ne, tm, LANES), row_map),
        ],
        out_shape=out_shapes,
        compiler_params=pltpu.CompilerParams(
            dimension_semantics=("arbitrary", "arbitrary"), vmem_limit_bytes=VMEM_LIMIT),
        name="norm_mod_projection",
    )(hc, hc, hc, hx, hx, hx, mod, mod, norm_g, w_qk, w_main, w_vg, cos, sin, conv_w, conv_b, kscale, gate_b)


def _attn_kernel(sink_ref, q_ref, *refs, layer, with_window, blk0, ctx_blocks, total_blocks):
    if with_window:
        kp_ref, kc_ref, kn_ref, kctx_ref, vp_ref, vc_ref, vn_ref, vctx_ref, az_ref, o_ref = refs
    else:
        kctx_ref, vctx_ref, az_ref, o_ref = refs
    ctx_len = ctx_blocks * BLOCK
    win = 3 * BLOCK if with_window else 0
    n_keys = win + ctx_len

    if with_window:
        n = pl.program_id(1) + blk0 - ctx_blocks
        lat_blocks = total_blocks - ctx_blocks
        i = lax.broadcasted_iota(jnp.int32, (BLOCK, win), 0)
        jx = lax.broadcasted_iota(jnp.int32, (BLOCK, win), 1)
        lo = jnp.where(n == 0, BLOCK, 0)
        hi = jnp.where(n == lat_blocks - 1, 2 * BLOCK, win)
        valid = (jx >= i) & (jx <= i + 2 * WINDOW) & (jx >= lo) & (jx < hi)
        bias = jnp.concatenate([jnp.where(valid, 0.0, -jnp.inf)] * ATT_GROUP, axis=0)

    lane_v = lax.broadcasted_iota(jnp.int32, (1, LANES), 1)
    low_half = lane_v < ATT_HEAD_DIM
    lane_q = lax.broadcasted_iota(jnp.int32, (1, KV_GROUP_COLS), 1)
    head_of_qlane = (lane_q % LANES) // (LANES // ATT_GROUP)
    row_head = lax.broadcasted_iota(jnp.int32, (ATT_GROUP * BLOCK, 1), 0) // BLOCK
    ones = jnp.ones((n_keys, LANES), jnp.bfloat16)

    for e_i in range(q_ref.shape[0]):
        if with_window:
            k_all = jnp.concatenate([kp_ref[e_i], kc_ref[e_i], kn_ref[e_i], kctx_ref[e_i]], axis=0)
            v_all = jnp.concatenate([vp_ref[e_i], vc_ref[e_i], vn_ref[e_i], vctx_ref[e_i]], axis=0)
        else:
            k_all, v_all = kctx_ref[e_i], vctx_ref[e_i]
        v_all = v_all.astype(jnp.float32)
        v_rot = pltpu.roll(v_all, ATT_HEAD_DIM, axis=1)
        for kvh in range(ATT_KV_HEADS):
            c0 = kvh * KV_GROUP_COLS
            qg = q_ref[e_i, :, c0:c0 + KV_GROUP_COLS]
            qm = jnp.concatenate(
                [jnp.where(head_of_qlane == g, qg, jnp.zeros_like(qg)) for g in range(ATT_GROUP)], axis=0)
            kx = k_all[:, c0:c0 + KV_GROUP_COLS]
            s = lax.dot_general(qm, kx, (((1,), (1,)), ((), ())), preferred_element_type=jnp.float32)
            sink = jnp.zeros((ATT_GROUP * BLOCK, 1), jnp.float32)
            for g in range(ATT_GROUP):
                sink = jnp.where(row_head == g, sink_ref[layer, kvh * ATT_GROUP + g] * LOG2_E, sink)
            if with_window:
                s_win = s[:, :win] + bias
                s_ctx = s[:, win:]
                m = jnp.maximum(jnp.maximum(jnp.max(s_win, axis=-1, keepdims=True),
                                            jnp.max(s_ctx, axis=-1, keepdims=True)), sink)
                e = jnp.concatenate([jnp.exp2(s_win - m), jnp.exp2(s_ctx - m)], axis=1).astype(jnp.bfloat16)
            else:
                m = jnp.maximum(jnp.max(s, axis=-1, keepdims=True), sink)
                e = jnp.exp2(s - m).astype(jnp.bfloat16)
            own_half = low_half if kvh == 0 else jnp.logical_not(low_half)
            v_two = jnp.where(own_half, v_all, v_rot).astype(jnp.bfloat16)
            res = jnp.dot(e, jnp.concatenate([v_two, ones], axis=1),
                          preferred_element_type=jnp.float32)
            o = res[:, :LANES] / (res[:, LANES:] + jnp.exp2(sink - m))
            az = az_ref[e_i, :, c0:c0 + KV_GROUP_COLS].astype(jnp.float32)
            for pair in range(ATT_GROUP // 2):
                both = jnp.where(low_half, o[2 * pair * BLOCK:(2 * pair + 1) * BLOCK],
                                 o[(2 * pair + 1) * BLOCK:(2 * pair + 2) * BLOCK])
                lo_col = c0 + pair * LANES
                o_ref[e_i, :, lo_col:lo_col + LANES] = (
                    both * _silu(az[:, pair * LANES:(pair + 1) * LANES])).astype(jnp.bfloat16)


def _attention(layer, sink, q, kx, v, az, *, ctx_len, with_window):
    bsz, s_len, _ = q.shape
    ctx_blocks = ctx_len // BLOCK
    total_blocks = s_len // BLOCK
    blk0 = ctx_blocks if with_window else 0
    n_blocks = total_blocks - ctx_blocks if with_window else ctx_blocks
    lo, hi = ctx_blocks, total_blocks - 1
    nb = math.gcd(bsz, BATCH_PER_STEP)

    def at(off):
        return lambda b, j: (b, jnp.clip(j + blk0 + off, lo, hi), 0)

    def cur(b, j):
        return (b, j + blk0, 0)

    def ctx(b, j):
        return (b, 0, 0)

    window_k = [pl.BlockSpec((nb, BLOCK, ATT_WIDTH), at(off)) for off in (-1, 0, 1)] if with_window else []
    window_v = [pl.BlockSpec((nb, BLOCK, KV_WIDTH), at(off)) for off in (-1, 0, 1)] if with_window else []
    n_win = len(window_k)
    kernel = functools.partial(_attn_kernel, layer=layer, with_window=with_window, blk0=blk0,
                               ctx_blocks=ctx_blocks, total_blocks=total_blocks)
    return pl.pallas_call(
        kernel,
        grid=(bsz // nb, n_blocks),
        in_specs=[
            pl.BlockSpec(memory_space=pltpu.SMEM),
            pl.BlockSpec((nb, BLOCK, ATT_WIDTH), cur),
            *window_k,
            pl.BlockSpec((nb, ctx_len, ATT_WIDTH), ctx),
            *window_v,
            pl.BlockSpec((nb, ctx_len, KV_WIDTH), ctx),
            pl.BlockSpec((nb, BLOCK, ATT_WIDTH), cur),
        ],
        out_specs=pl.BlockSpec((nb, BLOCK, ATT_WIDTH), lambda b, j: (b, j, 0)),
        out_shape=jax.ShapeDtypeStruct((bsz, n_blocks * BLOCK, ATT_WIDTH), jnp.bfloat16),
        compiler_params=pltpu.CompilerParams(
            dimension_semantics=("arbitrary", "arbitrary"), vmem_limit_bytes=VMEM_LIMIT),
        name="window_ctx_attention" if with_window else "ctx_attention",
    )(sink, q, *([kx] * (n_win + 1)), *([v] * (n_win + 1)), az)


STATE_ROWS = M_HEAD_DIM + 2 * SUBLANES


def _split2(x):
    hi = x.astype(jnp.bfloat16)
    lo = (x - hi.astype(jnp.float32)).astype(jnp.bfloat16)
    return hi, lo


def _mlstm_kernel(qkf_ref, vf_ref, gf_ref, qkb_ref, vb_ref, gb_ref, hf_ref, hb_ref, c_ref, m_ref):
    @pl.when(pl.program_id(1) == 0)
    def _():
        c_ref[...] = jnp.zeros_like(c_ref)
        m_ref[...] = jnp.full_like(m_ref, -jnp.inf)

    rows = lax.broadcasted_iota(jnp.int32, (CHUNK, CHUNK), 0)
    cols = lax.broadcasted_iota(jnp.int32, (CHUNK, CHUNK), 1)
    le = rows <= cols
    ge = rows >= cols
    tri_le = jnp.where(le, 1.0, 0.0).astype(jnp.bfloat16)
    tri_ge = jnp.where(ge, 1.0, 0.0).astype(jnp.bfloat16)

    for e in range(qkf_ref.shape[0]):
        _mlstm_element(e, qkf_ref, vf_ref, gf_ref, qkb_ref, vb_ref, gb_ref, hf_ref, hb_ref, c_ref, m_ref,
                       le, ge, tri_le, tri_ge)


def _mlstm_element(e, qkf_ref, vf_ref, gf_ref, qkb_ref, vb_ref, gb_ref, hf_ref, hb_ref, c_ref, m_ref,
                   le, ge, tri_le, tri_ge):
    nst = N_DIR * M_HEADS
    lane = lax.broadcasted_iota(jnp.int32, (1, LANES), 1)
    fwd_lane = lane % nst < M_HEADS
    gates = jnp.where(fwd_lane, gf_ref[e], gb_ref[e])
    gates_t = gates.T

    i8 = gates_t[0:nst]
    lf8 = _log_sigmoid(gates_t[nst:2 * nst])
    pre8 = sum(jnp.dot(p, tri_le, preferred_element_type=jnp.float32) for p in _split2(lf8))
    tot8 = pre8[:, CHUNK - 1:CHUNK]
    is_fwd = lax.broadcasted_iota(jnp.int32, (nst, 1), 0) < M_HEADS
    b8 = jnp.where(is_fwd, pre8, tot8 - pre8 + lf8)
    r8 = i8 - b8
    g8 = tot8 + r8
    m_prev = m_ref[e]
    m_new = jnp.maximum(tot8 + m_prev, jnp.max(g8, axis=1, keepdims=True))
    a8 = jnp.exp(tot8 + m_prev - m_new)
    w8 = jnp.exp(g8 - m_new)
    inter8 = b8 + m_prev
    m_ref[e] = m_new

    ones_row = jnp.where(lax.broadcasted_iota(jnp.int32, (2 * SUBLANES, CHUNK), 0) == 0, 1.0, 0.0
                         ).astype(jnp.bfloat16)

    lf_c = _log_sigmoid(gates)
    pre_c = sum(jnp.dot(tri_ge, p, preferred_element_type=jnp.float32) for p in _split2(lf_c))
    b_c = jnp.where(fwd_lane, pre_c, pre_c[CHUNK - 1:CHUNK, :] - pre_c + lf_c)
    r_c = pltpu.roll(gates, nst, axis=1) - b_c

    for d, (qk_ref, v_ref, h_ref) in enumerate(((qkf_ref, vf_ref, hf_ref), (qkb_ref, vb_ref, hb_ref))):
        valid_t = le if d == 0 else ge

        heads = range(M_HEADS)
        rs = [d * M_HEADS + hd for hd in heads]
        qs = [qk_ref[e, :, hd * M_HEAD_DIM:(hd + 1) * M_HEAD_DIM] for hd in heads]
        ks = [qk_ref[e, :, M_WIDTH + hd * M_HEAD_DIM:M_WIDTH + (hd + 1) * M_HEAD_DIM] for hd in heads]
        vts = [jnp.concatenate([v_ref[e, :, hd * M_HEAD_DIM:(hd + 1) * M_HEAD_DIM].T,
                                ones_row], axis=0) for hd in heads]
        cs = [c_ref[e, r] for r in rs]

        qts = [q.T for q in qs]
        st = [jnp.dot(k, qt, preferred_element_type=jnp.float32)
              for k, qt in zip(ks, qts)]
        dt = [jnp.where(valid_t, r_c[:, nst + r:nst + r + 1] + b8[r:r + 1, :], -jnp.inf) for r in rs]
        m_t = [jnp.maximum(inter8[r:r + 1, :], jnp.max(x, axis=0, keepdims=True)) for r, x in zip(rs, dt)]
        w_i = [jnp.exp(inter8[r:r + 1, :] - m) for r, m in zip(rs, m_t)]
        pt = [s_ * jnp.exp(x - m) for s_, x, m in zip(st, dt, m_t)]
        both = [jnp.dot(jnp.concatenate([vt, c.astype(jnp.bfloat16)], axis=1),
                        jnp.concatenate([p, qt * w], axis=0).astype(jnp.bfloat16),
                        preferred_element_type=jnp.float32)
                for vt, c, p, qt, w in zip(vts, cs, pt, qts, w_i)]
        for hd in heads:
            den = both[hd][M_HEAD_DIM:M_HEAD_DIM + 1]
            ht = both[hd][:M_HEAD_DIM] / jnp.maximum(jnp.abs(den), jnp.exp(-m_t[hd]))
            h_ref[e, :, hd * M_HEAD_DIM:(hd + 1) * M_HEAD_DIM] = ht.T.astype(h_ref.dtype)
        for hd, r in zip(heads, rs):
            vw = (vts[hd] * w8[r:r + 1, :]).astype(jnp.bfloat16)
            c_ref[e, r] = a8[r:r + 1, :] * cs[hd] + jnp.dot(vw, ks[hd], preferred_element_type=jnp.float32)


def _mlstm(mqk, mv, gates, *, ctx_len):
    bsz, s_len, _ = mqk.shape
    nc = s_len // CHUNK
    ncc = ctx_len // CHUNK

    def fwd(b, j):
        return (b, j, 0)

    def bwd(b, j):
        return (b, jnp.where(j < ncc, ncc - 1 - j, nc - 1 + ncc - j), 0)

    nstate = N_DIR * M_HEADS
    nb = math.gcd(bsz, BATCH_PER_STEP)
    return pl.pallas_call(
        _mlstm_kernel,
        grid=(bsz // nb, nc),
        in_specs=[
            pl.BlockSpec((nb, CHUNK, 2 * M_WIDTH), fwd),
            pl.BlockSpec((nb, CHUNK, M_WIDTH), fwd),
            pl.BlockSpec((nb, CHUNK, LANES), fwd),
            pl.BlockSpec((nb, CHUNK, 2 * M_WIDTH), bwd),
            pl.BlockSpec((nb, CHUNK, M_WIDTH), bwd),
            pl.BlockSpec((nb, CHUNK, LANES), bwd),
        ],
        out_specs=[
            pl.BlockSpec((nb, CHUNK, M_WIDTH), fwd),
            pl.BlockSpec((nb, CHUNK, M_WIDTH), bwd),
        ],
        out_shape=(jax.ShapeDtypeStruct((bsz, s_len, M_WIDTH), jnp.bfloat16),
                   jax.ShapeDtypeStruct((bsz, s_len, M_WIDTH), jnp.bfloat16)),
        scratch_shapes=[
            pltpu.VMEM((nb, nstate, STATE_ROWS, M_HEAD_DIM), jnp.float32),
            pltpu.VMEM((nb, nstate, LANES), jnp.float32),
        ],
        compiler_params=pltpu.CompilerParams(
            dimension_semantics=("arbitrary", "arbitrary"), vmem_limit_bytes=VMEM_LIMIT),
        name="mlstm_bidirectional_scan",
    )(mqk, mv, gates, mqk, mv, gates)


def _out_kernel(x_ref, att_ref, hf_ref, hb_ref, moz_ref, hg_ref, w_ref, mod_ref, *rest, final):
    if final:
        fg_ref, o_ref = rest
    else:
        (o_ref,) = rest
    ne, tm, _ = x_ref.shape
    mxs = []
    for e in range(ne):
        mo = moz_ref[e, :, :M_WIDTH].astype(jnp.float32)
        mz = moz_ref[e, :, M_WIDTH:].astype(jnp.float32)
        hs = _sigmoid(mo) * (hf_ref[e].astype(jnp.float32) + hb_ref[e].astype(jnp.float32))
        parts = []
        for hd in range(M_HEADS):
            hh = hs[:, hd * M_HEAD_DIM:(hd + 1) * M_HEAD_DIM]
            ms = jnp.mean(hh * hh, axis=-1, keepdims=True)
            parts.append(hh * lax.rsqrt(ms + EPS))
        mxs.append((jnp.concatenate(parts, axis=1) * hg_ref[...] * _silu(mz)).astype(jnp.bfloat16))
    att = jnp.concatenate([att_ref[e] for e in range(ne)], axis=0)
    out = (jnp.dot(att, w_ref[:ATT_WIDTH, :], preferred_element_type=jnp.float32)
           + jnp.dot(jnp.concatenate(mxs, axis=0), w_ref[ATT_WIDTH:, :], preferred_element_type=jnp.float32))
    for e in range(ne):
        gate = mod_ref[e if mod_ref.shape[0] > 1 else 0, 2:3, :]
        xnew = x_ref[e] + gate * out[e * tm:(e + 1) * tm]
        if final:
            ms = jnp.mean(xnew * xnew, axis=-1, keepdims=True)
            o_ref[e] = xnew * lax.rsqrt(ms + EPS) * fg_ref[...]
        else:
            o_ref[e] = xnew


def _output(layer, hseg, att, hf, hb, moz, head_g, w_out, mod, final_g, *, row0, is_ctx, final):
    bsz, seg_len, d = hseg.shape
    tm = ROW_TILE
    ne = math.gcd(bsz, PROJ_BATCH)
    t0 = row0 // tm

    def seg_map(b, j):
        return (b, j, 0)

    def row_map(b, j):
        return (b, j + t0, 0)

    if is_ctx:
        mod_spec = pl.BlockSpec((None, 1, 3, d), lambda b, j: (layer, mod.shape[1] - 1, 0, 0))
    else:
        mod_spec = pl.BlockSpec((None, ne, 3, d), lambda b, j: (layer, b, 0, 0))
    in_specs = [
        pl.BlockSpec((ne, tm, d), seg_map),
        pl.BlockSpec((ne, tm, ATT_WIDTH), seg_map),
        pl.BlockSpec((ne, tm, M_WIDTH), row_map),
        pl.BlockSpec((ne, tm, M_WIDTH), row_map),
        pl.BlockSpec((ne, tm, 2 * M_WIDTH), row_map),
        pl.BlockSpec((None, 1, M_WIDTH), lambda b, j: (layer, 0, 0)),
        pl.BlockSpec((None, d, d), lambda b, j: (layer, 0, 0), pipeline_mode=pl.Buffered(1)),
        mod_spec,
    ]
    args = [hseg, att, hf, hb, moz, head_g, w_out, mod]
    if final:
        in_specs.append(pl.BlockSpec((1, d), lambda b, j: (0, 0)))
        args.append(final_g)
    return pl.pallas_call(
        functools.partial(_out_kernel, final=final),
        grid=(bsz // ne, seg_len // tm),
        in_specs=in_specs,
        out_specs=pl.BlockSpec((ne, tm, d), seg_map),
        out_shape=jax.ShapeDtypeStruct(hseg.shape, jnp.float32),
        compiler_params=pltpu.CompilerParams(
            dimension_semantics=("arbitrary", "arbitrary"), vmem_limit_bytes=VMEM_LIMIT),
        name="gate_norm_out_projection",
    )(*args)


def _ctx_fused_kernel(sink_ref, q_ref, kctx_ref, vctx_ref, az_ref, x_ref, hf_ref, hb_ref, moz_ref, hg_ref, w_ref,
                      mod_ref, o_ref, att_ref, *, layer, ctx_blocks, total_blocks):
    _attn_kernel(sink_ref, q_ref, kctx_ref, vctx_ref, az_ref, att_ref, layer=layer, with_window=False, blk0=0,
                 ctx_blocks=ctx_blocks, total_blocks=total_blocks)
    _out_kernel(x_ref, att_ref, hf_ref, hb_ref, moz_ref, hg_ref, w_ref, mod_ref, o_ref, final=False)


def _ctx_attention_output(layer, sink, q, kx, v, az, hc, hf, hb, moz, head_g, w_out, mod):
    bsz, ctx_len, d = hc.shape
    ctx_blocks = ctx_len // BLOCK
    total_blocks = q.shape[1] // BLOCK
    nb = math.gcd(bsz, BATCH_PER_STEP)

    def blk(b, j):
        return (b, j, 0)

    def whole(b, j):
        return (b, 0, 0)

    kernel = functools.partial(_ctx_fused_kernel, layer=layer, ctx_blocks=ctx_blocks, total_blocks=total_blocks)
    return pl.pallas_call(
        kernel,
        grid=(bsz // nb, ctx_blocks),
        in_specs=[
            pl.BlockSpec(memory_space=pltpu.SMEM),
            pl.BlockSpec((nb, BLOCK, ATT_WIDTH), blk),
            pl.BlockSpec((nb, ctx_len, ATT_WIDTH), whole),
            pl.BlockSpec((nb, ctx_len, KV_WIDTH), whole),
            pl.BlockSpec((nb, BLOCK, ATT_WIDTH), blk),
            pl.BlockSpec((nb, BLOCK, d), blk),
            pl.BlockSpec((nb, BLOCK, M_WIDTH), blk),
            pl.BlockSpec((nb, BLOCK, M_WIDTH), blk),
            pl.BlockSpec((nb, BLOCK, 2 * M_WIDTH), blk),
            pl.BlockSpec((None, 1, M_WIDTH), lambda b, j: (layer, 0, 0)),
            pl.BlockSpec((None, d, d), lambda b, j: (layer, 0, 0), pipeline_mode=pl.Buffered(1)),
            pl.BlockSpec((None, 1, 3, d), lambda b, j: (layer, mod.shape[1] - 1, 0, 0)),
        ],
        out_specs=pl.BlockSpec((nb, BLOCK, d), blk),
        out_shape=jax.ShapeDtypeStruct(hc.shape, jnp.float32),
        scratch_shapes=[pltpu.VMEM((nb, BLOCK, ATT_WIDTH), jnp.bfloat16)],
        compiler_params=pltpu.CompilerParams(
            dimension_semantics=("arbitrary", "arbitrary"), vmem_limit_bytes=VMEM_LIMIT),
        name="ctx_attention_out_projection",
    )(sink, q, kx, v, az, hc, hf, hb, moz, head_g, w_out, mod)


def kernel(x, c, ctx, c_ctx, w_ada, b_ada, norm_g, w_in, conv_w, conv_b, gate_b, sink, head_g, w_out, final_g):
    bsz, t_len, d = x.shape
    ctx_len = ctx.shape[1]
    depth = w_in.shape[0]
    assert d == D_MODEL and t_len % ROW_TILE == 0 and ctx_len % ROW_TILE == 0 and t_len % GRID_W == 0

    hc, hx = ctx, x
    mod_rows = -(-(bsz + 1) // SUBLANES) * SUBLANES
    cc = jnp.zeros((mod_rows, d), jnp.float32).at[:bsz].set(c).at[mod_rows - 1].set(c_ctx)
    mod = _modulation(cc, w_ada, b_ada).reshape(depth, mod_rows, 3, d)

    w_qk, w_main, w_vg = _split_w_in(w_in)
    w_o = w_out.astype(jnp.bfloat16)
    cos, sin = _rope_tables(t_len, ctx_len)
    kscale = jnp.concatenate([jnp.ones((1, M_WIDTH), jnp.float32),
                              jnp.full((1, M_WIDTH), M_HEAD_DIM ** -0.5, jnp.float32)], axis=1)
    gate_bias = jnp.pad(gate_b, ((0, 0), (0, LANES - N_GATES)))[:, None, :]
    norm_g3, conv_b3, head_g3 = norm_g[:, None, :], conv_b[:, None, :], head_g[:, None, :]

    for l in range(depth):
        last = l == depth - 1
        q, kx, v, az, mqk, mv, moz, gates = _projection(
            l, hc, hx, mod, norm_g3, w_qk, w_main, w_vg, cos, sin, conv_w, conv_b3, kscale, gate_bias)
        att_x = _attention(l, sink, q, kx, v, az, ctx_len=ctx_len, with_window=True)
        hf, hb = _mlstm(mqk, mv, gates, ctx_len=ctx_len)
        out_args = (hf, hb, moz, head_g3, w_o, mod, final_g[None])
        if last:
            return _output(l, hx, att_x, *out_args, row0=ctx_len, is_ctx=False, final=True)
        hc, hx = (_ctx_attention_output(l, sink, q, kx, v, az, hc, hf, hb, moz, head_g3, w_o, mod),
                  _output(l, hx, att_x, *out_args, row0=ctx_len, is_ctx=False, final=False))
```

```python
import functools
import math

import jax
import jax.numpy as jnp
from jax import lax
from jax.experimental import pallas as pl
from jax.experimental.pallas import tpu as pltpu

D_MODEL = 1024
GRID_W = 64
ATT_HEADS = 8
ATT_KV_HEADS = 2
ATT_GROUP = ATT_HEADS // ATT_KV_HEADS
ATT_HEAD_DIM = 64
ATT_WIDTH = ATT_HEADS * ATT_HEAD_DIM
KV_WIDTH = ATT_KV_HEADS * ATT_HEAD_DIM
WINDOW = 128
BLOCK = 128
ROPE_BASE = 10000.0
M_HEADS = 4
M_HEAD_DIM = 128
M_WIDTH = M_HEADS * M_HEAD_DIM
CONV_K = 5
CHUNK = 128
N_DIR = 2
N_GATES = 2 * N_DIR * M_HEADS
EPS = 1e-6

LANES = 128
SUBLANES = 8
HALO = SUBLANES
ROW_TILE = 256
BATCH_PER_STEP = 8
PROJ_BATCH = 4
PROJ_STACK = 1
VMEM_LIMIT = 56 * 1024 * 1024

W_IN_MAIN_START = ATT_WIDTH + 2 * KV_WIDTH
C_AZ = 0
C_MQK = C_AZ + ATT_WIDTH
C_MV = C_MQK + 2 * M_WIDTH
C_MOZ = C_MV + M_WIDTH
N_MAIN = C_MOZ + 2 * M_WIDTH
KV_GROUP_COLS = ATT_GROUP * ATT_HEAD_DIM
LOG2_E = 1.4426950408889634
Q_PRESCALE = ATT_HEAD_DIM ** -0.5 * LOG2_E


def _sigmoid(x):
    return 1.0 / (1.0 + jnp.exp(-x))


def _silu(x):
    return x * _sigmoid(x)


def _log_sigmoid(x):
    return jnp.minimum(x, 0.0) - jnp.log(1.0 + jnp.exp(-jnp.abs(x)))


def _split_w_in(w_in):
    depth, d, n_in = w_in.shape
    w_t = jnp.swapaxes(w_in, 1, 2)
    quarter = ATT_HEAD_DIM // 4
    k0 = ATT_WIDTH
    v0 = k0 + KV_WIDTH
    wq = w_t[:, :k0].reshape(depth, ATT_KV_HEADS, ATT_GROUP, 2, 2, quarter, d)
    wq = wq.transpose(0, 1, 4, 2, 3, 5, 6).reshape(depth, ATT_WIDTH, d)
    wk = w_t[:, k0:v0].reshape(depth, ATT_KV_HEADS, 1, 2, 2, quarter, d)
    wk = jnp.broadcast_to(wk.transpose(0, 1, 4, 2, 3, 5, 6),
                          (depth, ATT_KV_HEADS, 2, ATT_GROUP, 2, quarter, d)).reshape(depth, ATT_WIDTH, d)
    w_qk = jnp.swapaxes(jnp.concatenate([wq, wk], axis=1), 1, 2).astype(jnp.bfloat16)
    cast_rows = W_IN_MAIN_START
    v_blk = (W_IN_MAIN_START - KV_WIDTH) // KV_WIDTH
    g_blk = (W_IN_MAIN_START + N_MAIN) // N_GATES
    w_main, w_vg = pl.pallas_call(
        _cast_w_kernel,
        grid=(depth, N_MAIN // cast_rows),
        in_specs=[pl.BlockSpec((1, cast_rows, d), lambda l, r: (l, r + 1, 0)),
                  pl.BlockSpec((1, KV_WIDTH, d), lambda l, r: (l, v_blk, 0)),
                  pl.BlockSpec((1, N_GATES, d), lambda l, r: (l, g_blk, 0))],
        out_specs=[pl.BlockSpec((1, d, cast_rows), lambda l, r: (l, 0, r)),
                   pl.BlockSpec((1, d, KV_WIDTH + LANES), lambda l, r: (l, 0, 0))],
        out_shape=(jax.ShapeDtypeStruct((depth, d, N_MAIN), jnp.bfloat16),
                   jax.ShapeDtypeStruct((depth, d, KV_WIDTH + LANES), jnp.bfloat16)),
        compiler_params=pltpu.CompilerParams(dimension_semantics=("arbitrary", "arbitrary")),
        name="cast_projection_weights",
    )(w_t, w_t, w_t)
    return w_qk, w_main, w_vg


def _cast_w_kernel(wm_in, wv_in, wg_in, wm_ref, wvg_ref):
    wm_ref[0] = wm_in[0].T.astype(jnp.bfloat16)

    @pl.when(pl.program_id(1) == 0)
    def _():
        gates = jnp.concatenate([wg_in[0], jnp.zeros((LANES - N_GATES, wg_in.shape[2]), jnp.float32)], axis=0)
        wvg_ref[0, :, :KV_WIDTH] = wv_in[0].T.astype(jnp.bfloat16)
        wvg_ref[0, :, KV_WIDTH:] = gates.T.astype(jnp.bfloat16)


def _rope_tables(t_len, ctx_len):
    rows = t_len // GRID_W
    row = jnp.repeat(jnp.arange(rows), GRID_W).astype(jnp.float32)
    col = jnp.tile(jnp.arange(GRID_W), rows).astype(jnp.float32)
    half = ATT_HEAD_DIM // 2
    inv = ROPE_BASE ** (-jnp.arange(0, half, 2, dtype=jnp.float32) / half)
    ang = jnp.concatenate([row[:, None] * inv, col[:, None] * inv], axis=-1)
    ang = jnp.tile(ang, (1, LANES // half))
    cos = jnp.concatenate([jnp.ones((ctx_len, LANES), jnp.float32), jnp.cos(ang)], axis=0)
    sin = jnp.concatenate([jnp.zeros((ctx_len, LANES), jnp.float32), jnp.sin(ang)], axis=0)
    return cos, sin


def _mod_kernel(c_ref, w_ref, b_ref, o_ref):
    sc = _silu(c_ref[...])
    o_ref[0] = jnp.dot(sc.astype(jnp.bfloat16), w_ref[0].astype(jnp.bfloat16),
                       preferred_element_type=jnp.float32) + b_ref[0]


def _modulation(cc, w_ada, b_ada):
    depth, d, d3 = w_ada.shape
    rows = cc.shape[0]
    nt = d3 // d
    return pl.pallas_call(
        _mod_kernel,
        grid=(depth, nt),
        in_specs=[
            pl.BlockSpec((rows, d), lambda l, n: (0, 0)),
            pl.BlockSpec((1, d, d), lambda l, n: (l, 0, n)),
            pl.BlockSpec((1, 1, d), lambda l, n: (l, 0, n)),
        ],
        out_specs=pl.BlockSpec((1, rows, d), lambda l, n: (l, 0, n)),
        out_shape=jax.ShapeDtypeStruct((depth, rows, d3), jnp.float32),
        compiler_params=pltpu.CompilerParams(dimension_semantics=("arbitrary", "arbitrary")),
        name="adaln_modulation",
    )(cc, w_ada, b_ada.reshape(depth, 1, d3))


def _proj_kernel(*refs, tm, tiles_ctx, tiles_total):
    n_elem = refs[4].shape[0]
    for e0 in range(0, n_elem, PROJ_STACK):
        _proj_group(e0, min(PROJ_STACK, n_elem - e0), *refs, tm=tm, tiles_ctx=tiles_ctx, tiles_total=tiles_total)


def _proj_group(e0, ne, cp_ref, c_ref, cn_ref, xp_ref, x_ref, xn_ref, mod_ref, modc_ref, g_ref,
                wqk_ref, wm_ref, wvg_ref,
                cos_ref, sin_ref, cw_ref, cb_ref,
                ks_ref, gb_ref,
                q_ref, kx_ref, v_ref, az_ref, mqk_ref, mv_ref, moz_ref, gate_ref,
                *, tm, tiles_ctx, tiles_total):
    te = tm + 2 * HALO
    j = pl.program_id(1)
    is_ctx = j < tiles_ctx
    seg_first = jnp.logical_or(j == 0, j == tiles_ctx)
    seg_last = jnp.logical_or(j == tiles_ctx - 1, j == tiles_total - 1)

    ext = []
    for e in range(e0, e0 + ne):
        xe = jnp.concatenate([jnp.where(is_ctx, cp_ref[e], xp_ref[e]), jnp.where(is_ctx, c_ref[e], x_ref[e]),
                              jnp.where(is_ctx, cn_ref[e], xn_ref[e])], axis=0)
        ms = jnp.mean(xe * xe, axis=-1, keepdims=True)
        shift = jnp.where(is_ctx, modc_ref[0, 0:1, :], mod_ref[e, 0:1, :])
        scale = jnp.where(is_ctx, modc_ref[0, 1:2, :], mod_ref[e, 1:2, :])
        gain = g_ref[...] * (1.0 + scale)
        ext.append((xe * lax.rsqrt(ms + EPS) * gain + shift).astype(jnp.bfloat16))
    xn_ext = jnp.concatenate(ext, axis=0)
    xn = jnp.concatenate([x_[HALO:HALO + tm] for x_ in ext], axis=0)

    def proj(rows, w_ref, c0, c1):
        return jnp.dot(rows, w_ref[:, c0:c1], preferred_element_type=jnp.float32)

    row = lax.broadcasted_iota(jnp.int32, (te, 1), 0)
    keep = jnp.logical_and(jnp.logical_or(row >= HALO, jnp.logical_not(seg_first)),
                           jnp.logical_or(row < HALO + tm, jnp.logical_not(seg_last)))
    nb = tm // SUBLANES
    sub = lax.broadcasted_iota(jnp.int32, (1, SUBLANES, 1), 1)

    def conv_chunk(c0, width):
        y_all = proj(xn_ext, wm_ref, C_MQK + c0, C_MQK + c0 + width)
        for i, e in enumerate(range(e0, e0 + ne)):
            ye = jnp.where(keep, y_all[i * te:(i + 1) * te], 0.0).reshape(nb + 2, SUBLANES, width)
            acc = jnp.broadcast_to(cb_ref[:, c0:c0 + width].reshape(1, 1, width), (nb, SUBLANES, width))
            for t in range(CONV_K):
                delta = t - CONV_K // 2
                w_t = cw_ref[t:t + 1, c0:c0 + width].reshape(1, 1, width)
                if delta == 0:
                    tap = ye[1:nb + 1]
                else:
                    rot = pltpu.roll(ye, (-delta) % SUBLANES, axis=1)
                    if delta > 0:
                        tap = jnp.where(sub < SUBLANES - delta, rot[1:nb + 1], rot[2:nb + 2])
                    else:
                        tap = jnp.where(sub >= -delta, rot[1:nb + 1], rot[0:nb])
                acc = acc + w_t * tap
            acc = acc.reshape(tm, width)
            mqk_ref[e, :, c0:c0 + width] = (_silu(acc) * ks_ref[:, c0:c0 + width]).astype(jnp.bfloat16)

    def rope_group(grp):
        is_q = grp < ATT_KV_HEADS
        cos = cos_ref[...] * Q_PRESCALE if is_q else cos_ref[...]
        sin = sin_ref[...] * Q_PRESCALE if is_q else sin_ref[...]
        yq = proj(xn, wqk_ref, grp * KV_GROUP_COLS, (grp + 1) * KV_GROUP_COLS)
        dst = q_ref if is_q else kx_ref
        c0 = (grp % ATT_KV_HEADS) * KV_GROUP_COLS
        for i, e in enumerate(range(e0, e0 + ne)):
            first = yq[i * tm:(i + 1) * tm, :LANES]
            second = yq[i * tm:(i + 1) * tm, LANES:]
            dst[e, :, c0:c0 + LANES] = (first * cos - second * sin).astype(jnp.bfloat16)
            dst[e, :, c0 + LANES:c0 + 2 * LANES] = (second * cos + first * sin).astype(jnp.bfloat16)

    def plain(dst_ref, c0, width, off=0):
        y = proj(xn, wm_ref, c0, c0 + width)
        for i, e in enumerate(range(e0, e0 + ne)):
            dst_ref[e, :, off:off + width] = y[i * tm:(i + 1) * tm].astype(dst_ref.dtype)

    others = [functools.partial(rope_group, g) for g in range(2 * ATT_KV_HEADS)]
    others += [functools.partial(plain, az_ref, C_AZ, ATT_WIDTH),
               functools.partial(plain, mv_ref, C_MV, M_WIDTH),
               functools.partial(plain, moz_ref, C_MOZ, M_WIDTH),
               functools.partial(plain, moz_ref, C_MOZ + M_WIDTH, M_WIDTH, M_WIDTH)]
    per_chunk = 2
    cw = 2 * M_WIDTH * per_chunk // len(others)
    for i in range(len(others) // per_chunk):
        conv_chunk(i * cw, cw)
        for other in others[i * per_chunk:(i + 1) * per_chunk]:
            other()
    y = proj(xn, wvg_ref, 0, KV_WIDTH + LANES)
    for i, e in enumerate(range(e0, e0 + ne)):
        v_ref[e] = y[i * tm:(i + 1) * tm, :KV_WIDTH].astype(jnp.bfloat16)
        gate_ref[e] = y[i * tm:(i + 1) * tm, KV_WIDTH:] + gb_ref[...]


def _tile_with_halo_specs(ne, tm, d, n_rows, tile_of):
    hb = tm // HALO
    last = n_rows // HALO - 1
    return (pl.BlockSpec((ne, HALO, d), lambda b, j: (b, jnp.maximum(tile_of(j) * hb - 1, 0), 0)),
            pl.BlockSpec((ne, tm, d), lambda b, j: (b, tile_of(j), 0)),
            pl.BlockSpec((ne, HALO, d), lambda b, j: (b, jnp.minimum((tile_of(j) + 1) * hb, last), 0)))


def _projection(layer, hc, hx, mod, norm_g, w_qk, w_main, w_vg, cos, sin, conv_w, conv_b, kscale, gate_b):
    bsz, ctx_len, d = hc.shape
    s_len = ctx_len + hx.shape[1]
    tm = ROW_TILE
    ne = math.gcd(bsz, PROJ_BATCH)
    tiles_total = s_len // tm
    tiles_ctx = ctx_len // tm
    ctx_blk = mod.shape[1] - 1

    def row_map(b, j):
        return (b, j, 0)

    def per_layer(*block):
        return pl.BlockSpec((None,) + block, lambda b, j: (layer,) + (0,) * len(block))

    kernel = functools.partial(_proj_kernel, tm=tm, tiles_ctx=tiles_ctx, tiles_total=tiles_total)
    out_shapes = (
        jax.ShapeDtypeStruct((bsz, s_len, ATT_WIDTH), jnp.bfloat16),
        jax.ShapeDtypeStruct((bsz, s_len, ATT_WIDTH), jnp.bfloat16),
        jax.ShapeDtypeStruct((bsz, s_len, KV_WIDTH), jnp.bfloat16),
        jax.ShapeDtypeStruct((bsz, s_len, ATT_WIDTH), jnp.bfloat16),
        jax.ShapeDtypeStruct((bsz, s_len, 2 * M_WIDTH), jnp.bfloat16),
        jax.ShapeDtypeStruct((bsz, s_len, M_WIDTH), jnp.bfloat16),
        jax.ShapeDtypeStruct((bsz, s_len, 2 * M_WIDTH), jnp.bfloat16),
        jax.ShapeDtypeStruct((bsz, s_len, LANES), jnp.float32),
    )
    return pl.pallas_call(
        kernel,
        grid=(bsz // ne, tiles_total),
        in_specs=[
            *_tile_with_halo_specs(ne, tm, d, ctx_len, lambda j: jnp.minimum(j, tiles_ctx - 1)),
            *_tile_with_halo_specs(ne, tm, d, s_len - ctx_len, lambda j: jnp.maximum(j - tiles_ctx, 0)),
            pl.BlockSpec((None, ne, 3, d), lambda b, j: (layer, b, 0, 0)),
            pl.BlockSpec((None, 1, 3, d), lambda b, j: (layer, ctx_blk, 0, 0)),
            per_layer(1, d),
            pl.BlockSpec((None, d, 2 * ATT_WIDTH), lambda b, j: (layer, 0, 0), pipeline_mode=pl.Buffered(1)),
            pl.BlockSpec((None, d, N_MAIN), lambda b, j: (layer, 0, 0), pipeline_mode=pl.Buffered(1)),
            pl.BlockSpec((None, d, KV_WIDTH + LANES), lambda b, j: (layer, 0, 0), pipeline_mode=pl.Buffered(1)),
            pl.BlockSpec((tm, LANES), lambda b, j: (j, 0)),
            pl.BlockSpec((tm, LANES), lambda b, j: (j, 0)),
            per_layer(CONV_K, 2 * M_WIDTH),
            per_layer(1, 2 * M_WIDTH),
            pl.BlockSpec((1, 2 * M_WIDTH), lambda b, j: (0, 0)),
            per_layer(1, LANES),
        ],
        out_specs=[
            pl.BlockSpec((ne, tm, ATT_WIDTH), row_map),
            pl.BlockSpec((ne, tm, ATT_WIDTH), row_map),
            pl.BlockSpec((ne, tm, KV_WIDTH), row_map),
            pl.BlockSpec((ne, tm, ATT_WIDTH), row_map),
            pl.BlockSpec((ne, tm, 2 * M_WIDTH), row_map),
            pl.BlockSpec((ne, tm, M_WIDTH), row_map),
            pl.BlockSpec((ne, tm, 2 * M_WIDTH), row_map),
            pl.BlockSpec((ne, tm, LANES), row_map),
        ],
        out_shape=out_shapes,
        compiler_params=pltpu.CompilerParams(
            dimension_semantics=("arbitrary", "arbitrary"), vmem_limit_bytes=VMEM_LIMIT),
        name="norm_mod_projection",
    )(hc, hc, hc, hx, hx, hx, mod, mod, norm_g, w_qk, w_main, w_vg, cos, sin, conv_w, conv_b, kscale, gate_b)


def _attn_kernel(sink_ref, q_ref, *refs, layer, with_window, blk0, ctx_blocks, total_blocks):
    if with_window:
        kp_ref, kc_ref, kn_ref, kctx_ref, vp_ref, vc_ref, vn_ref, vctx_ref, az_ref, o_ref = refs
    else:
        kctx_ref, vctx_ref, az_ref, o_ref = refs
    ctx_len = ctx_blocks * BLOCK
    win = 3 * BLOCK if with_window else 0
    n_keys = win + ctx_len

    if with_window:
        n = pl.program_id(1) + blk0 - ctx_blocks
        lat_blocks = total_blocks - ctx_blocks
        i = lax.broadcasted_iota(jnp.int32, (BLOCK, win), 0)
        jx = lax.broadcasted_iota(jnp.int32, (BLOCK, win), 1)
        lo = jnp.where(n == 0, BLOCK, 0)
        hi = jnp.where(n == lat_blocks - 1, 2 * BLOCK, win)
        valid = (jx >= i) & (jx <= i + 2 * WINDOW) & (jx >= lo) & (jx < hi)
        bias = jnp.concatenate([jnp.where(valid, 0.0, -jnp.inf)] * ATT_GROUP, axis=0)

    lane_v = lax.broadcasted_iota(jnp.int32, (1, LANES), 1)
    low_half = lane_v < ATT_HEAD_DIM
    lane_q = lax.broadcasted_iota(jnp.int32, (1, KV_GROUP_COLS), 1)
    head_of_qlane = (lane_q % LANES) // (LANES // ATT_GROUP)
    row_head = lax.broadcasted_iota(jnp.int32, (ATT_GROUP * BLOCK, 1), 0) // BLOCK
    ones = jnp.ones((n_keys, LANES), jnp.bfloat16)

    for e_i in range(q_ref.shape[0]):
        if with_window:
            k_all = jnp.concatenate([kp_ref[e_i], kc_ref[e_i], kn_ref[e_i], kctx_ref[e_i]], axis=0)
            v_all = jnp.concatenate([vp_ref[e_i], vc_ref[e_i], vn_ref[e_i], vctx_ref[e_i]], axis=0)
        else:
            k_all, v_all = kctx_ref[e_i], vctx_ref[e_i]
        v_all = v_all.astype(jnp.float32)
        v_rot = pltpu.roll(v_all, ATT_HEAD_DIM, axis=1)
        for kvh in range(ATT_KV_HEADS):
            c0 = kvh * KV_GROUP_COLS
            qg = q_ref[e_i, :, c0:c0 + KV_GROUP_COLS]
            qm = jnp.concatenate(
                [jnp.where(head_of_qlane == g, qg, jnp.zeros_like(qg)) for g in range(ATT_GROUP)], axis=0)
            kx = k_all[:, c0:c0 + KV_GROUP_COLS]
            s = lax.dot_general(qm, kx, (((1,), (1,)), ((), ())), preferred_element_type=jnp.float32)
            sink = jnp.zeros((ATT_GROUP * BLOCK, 1), jnp.float32)
            for g in range(ATT_GROUP):
                sink = jnp.where(row_head == g, sink_ref[layer, kvh * ATT_GROUP + g] * LOG2_E, sink)
            if with_window:
                s_win = s[:, :win] + bias
                s_ctx = s[:, win:]
                m = jnp.maximum(jnp.maximum(jnp.max(s_win, axis=-1, keepdims=True),
                                            jnp.max(s_ctx, axis=-1, keepdims=True)), sink)
                e = jnp.concatenate([jnp.exp2(s_win - m), jnp.exp2(s_ctx - m)], axis=1).astype(jnp.bfloat16)
            else:
                m = jnp.maximum(jnp.max(s, axis=-1, keepdims=True), sink)
                e = jnp.exp2(s - m).astype(jnp.bfloat16)
            own_half = low_half if kvh == 0 else jnp.logical_not(low_half)
            v_two = jnp.where(own_half, v_all, v_rot).astype(jnp.bfloat16)
            res = jnp.dot(e, jnp.concatenate([v_two, ones], axis=1),
                          preferred_element_type=jnp.float32)
            o = res[:, :LANES] / (res[:, LANES:] + jnp.exp2(sink - m))
            az = az_ref[e_i, :, c0:c0 + KV_GROUP_COLS].astype(jnp.float32)
            for pair in range(ATT_GROUP // 2):
                both = jnp.where(low_half, o[2 * pair * BLOCK:(2 * pair + 1) * BLOCK],
                                 o[(2 * pair + 1) * BLOCK:(2 * pair + 2) * BLOCK])
                lo_col = c0 + pair * LANES
                o_ref[e_i, :, lo_col:lo_col + LANES] = (
                    both * _silu(az[:, pair * LANES:(pair + 1) * LANES])).astype(jnp.bfloat16)


def _attention(layer, sink, q, kx, v, az, *, ctx_len, with_window):
    bsz, s_len, _ = q.shape
    ctx_blocks = ctx_len // BLOCK
    total_blocks = s_len // BLOCK
    blk0 = ctx_blocks if with_window else 0
    n_blocks = total_blocks - ctx_blocks if with_window else ctx_blocks
    lo, hi = ctx_blocks, total_blocks - 1
    nb = math.gcd(bsz, BATCH_PER_STEP)

    def at(off):
        return lambda b, j: (b, jnp.clip(j + blk0 + off, lo, hi), 0)

    def cur(b, j):
        return (b, j + blk0, 0)

    def ctx(b, j):
        return (b, 0, 0)

    window_k = [pl.BlockSpec((nb, BLOCK, ATT_WIDTH), at(off)) for off in (-1, 0, 1)] if with_window else []
    window_v = [pl.BlockSpec((nb, BLOCK, KV_WIDTH), at(off)) for off in (-1, 0, 1)] if with_window else []
    n_win = len(window_k)
    kernel = functools.partial(_attn_kernel, layer=layer, with_window=with_window, blk0=blk0,
                               ctx_blocks=ctx_blocks, total_blocks=total_blocks)
    return pl.pallas_call(
        kernel,
        grid=(bsz // nb, n_blocks),
        in_specs=[
            pl.BlockSpec(memory_space=pltpu.SMEM),
            pl.BlockSpec((nb, BLOCK, ATT_WIDTH), cur),
            *window_k,
            pl.BlockSpec((nb, ctx_len, ATT_WIDTH), ctx),
            *window_v,
            pl.BlockSpec((nb, ctx_len, KV_WIDTH), ctx),
            pl.BlockSpec((nb, BLOCK, ATT_WIDTH), cur),
        ],
        out_specs=pl.BlockSpec((nb, BLOCK, ATT_WIDTH), lambda b, j: (b, j, 0)),
        out_shape=jax.ShapeDtypeStruct((bsz, n_blocks * BLOCK, ATT_WIDTH), jnp.bfloat16),
        compiler_params=pltpu.CompilerParams(
            dimension_semantics=("arbitrary", "arbitrary"), vmem_limit_bytes=VMEM_LIMIT),
        name="window_ctx_attention" if with_window else "ctx_attention",
    )(sink, q, *([kx] * (n_win + 1)), *([v] * (n_win + 1)), az)


STATE_ROWS = M_HEAD_DIM + 2 * SUBLANES


def _split2(x):
    hi = x.astype(jnp.bfloat16)
    lo = (x - hi.astype(jnp.float32)).astype(jnp.bfloat16)
    return hi, lo


def _mlstm_kernel(qkf_ref, vf_ref, gf_ref, qkb_ref, vb_ref, gb_ref, hf_ref, hb_ref, c_ref, m_ref):
    @pl.when(pl.program_id(1) == 0)
    def _():
        c_ref[...] = jnp.zeros_like(c_ref)
        m_ref[...] = jnp.full_like(m_ref, -jnp.inf)

    rows = lax.broadcasted_iota(jnp.int32, (CHUNK, CHUNK), 0)
    cols = lax.broadcasted_iota(jnp.int32, (CHUNK, CHUNK), 1)
    le = rows <= cols
    ge = rows >= cols
    tri_le = jnp.where(le, 1.0, 0.0).astype(jnp.bfloat16)
    tri_ge = jnp.where(ge, 1.0, 0.0).astype(jnp.bfloat16)

    for e in range(qkf_ref.shape[0]):
        _mlstm_element(e, qkf_ref, vf_ref, gf_ref, qkb_ref, vb_ref, gb_ref, hf_ref, hb_ref, c_ref, m_ref,
                       le, ge, tri_le, tri_ge)


def _mlstm_element(e, qkf_ref, vf_ref, gf_ref, qkb_ref, vb_ref, gb_ref, hf_ref, hb_ref, c_ref, m_ref,
                   le, ge, tri_le, tri_ge):
    nst = N_DIR * M_HEADS
    lane = lax.broadcasted_iota(jnp.int32, (1, LANES), 1)
    fwd_lane = lane % nst < M_HEADS
    gates = jnp.where(fwd_lane, gf_ref[e], gb_ref[e])
    gates_t = gates.T

    i8 = gates_t[0:nst]
    lf8 = _log_sigmoid(gates_t[nst:2 * nst])
    pre8 = sum(jnp.dot(p, tri_le, preferred_element_type=jnp.float32) for p in _split2(lf8))
    tot8 = pre8[:, CHUNK - 1:CHUNK]
    is_fwd = lax.broadcasted_iota(jnp.int32, (nst, 1), 0) < M_HEADS
    b8 = jnp.where(is_fwd, pre8, tot8 - pre8 + lf8)
    r8 = i8 - b8
    g8 = tot8 + r8
    m_prev = m_ref[e]
    m_new = jnp.maximum(tot8 + m_prev, jnp.max(g8, axis=1, keepdims=True))
    a8 = jnp.exp(tot8 + m_prev - m_new)
    w8 = jnp.exp(g8 - m_new)
    inter8 = b8 + m_prev
    m_ref[e] = m_new

    ones_row = jnp.where(lax.broadcasted_iota(jnp.int32, (2 * SUBLANES, CHUNK), 0) == 0, 1.0, 0.0
                         ).astype(jnp.bfloat16)

    lf_c = _log_sigmoid(gates)
    pre_c = sum(jnp.dot(tri_ge, p, preferred_element_type=jnp.float32) for p in _split2(lf_c))
    b_c = jnp.where(fwd_lane, pre_c, pre_c[CHUNK - 1:CHUNK, :] - pre_c + lf_c)
    r_c = pltpu.roll(gates, nst, axis=1) - b_c

    for d, (qk_ref, v_ref, h_ref) in enumerate(((qkf_ref, vf_ref, hf_ref), (qkb_ref, vb_ref, hb_ref))):
        valid_t = le if d == 0 else ge

        heads = range(M_HEADS)
        rs = [d * M_HEADS + hd for hd in heads]
        qs = [qk_ref[e, :, hd * M_HEAD_DIM:(hd + 1) * M_HEAD_DIM] for hd in heads]
        ks = [qk_ref[e, :, M_WIDTH + hd * M_HEAD_DIM:M_WIDTH + (hd + 1) * M_HEAD_DIM] for hd in heads]
        vts = [jnp.concatenate([v_ref[e, :, hd * M_HEAD_DIM:(hd + 1) * M_HEAD_DIM].T,
                                ones_row], axis=0) for hd in heads]
        cs = [c_ref[e, r] for r in rs]

        qts = [q.T for q in qs]
        st = [jnp.dot(k, qt, preferred_element_type=jnp.float32)
              for k, qt in zip(ks, qts)]
        dt = [jnp.where(valid_t, r_c[:, nst + r:nst + r + 1] + b8[r:r + 1, :], -jnp.inf) for r in rs]
        m_t = [jnp.maximum(inter8[r:r + 1, :], jnp.max(x, axis=0, keepdims=True)) for r, x in zip(rs, dt)]
        w_i = [jnp.exp(inter8[r:r + 1, :] - m) for r, m in zip(rs, m_t)]
        pt = [s_ * jnp.exp(x - m) for s_, x, m in zip(st, dt, m_t)]
        both = [jnp.dot(jnp.concatenate([vt, c.astype(jnp.bfloat16)], axis=1),
                        jnp.concatenate([p, qt * w], axis=0).astype(jnp.bfloat16),
                        preferred_element_type=jnp.float32)
                for vt, c, p, qt, w in zip(vts, cs, pt, qts, w_i)]
        for hd in heads:
            den = both[hd][M_HEAD_DIM:M_HEAD_DIM + 1]
            ht = both[hd][:M_HEAD_DIM] / jnp.maximum(jnp.abs(den), jnp.exp(-m_t[hd]))
            h_ref[e, :, hd * M_HEAD_DIM:(hd + 1) * M_HEAD_DIM] = ht.T.astype(h_ref.dtype)
        for hd, r in zip(heads, rs):
            vw = (vts[hd] * w8[r:r + 1, :]).astype(jnp.bfloat16)
            c_ref[e, r] = a8[r:r + 1, :] * cs[hd] + jnp.dot(vw, ks[hd], preferred_element_type=jnp.float32)


def _mlstm(mqk, mv, gates, *, ctx_len):
    bsz, s_len, _ = mqk.shape
    nc = s_len // CHUNK
    ncc = ctx_len // CHUNK

    def fwd(b, j):
        return (b, j, 0)

    def bwd(b, j):
        return (b, jnp.where(j < ncc, ncc - 1 - j, nc - 1 + ncc - j), 0)

    nstate = N_DIR * M_HEADS
    nb = math.gcd(bsz, BATCH_PER_STEP)
    return pl.pallas_call(
        _mlstm_kernel,
        grid=(bsz // nb, nc),
        in_specs=[
            pl.BlockSpec((nb, CHUNK, 2 * M_WIDTH), fwd),
            pl.BlockSpec((nb, CHUNK, M_WIDTH), fwd),
            pl.BlockSpec((nb, CHUNK, LANES), fwd),
            pl.BlockSpec((nb, CHUNK, 2 * M_WIDTH), bwd),
            pl.BlockSpec((nb, CHUNK, M_WIDTH), bwd),
            pl.BlockSpec((nb, CHUNK, LANES), bwd),
        ],
        out_specs=[
            pl.BlockSpec((nb, CHUNK, M_WIDTH), fwd),
            pl.BlockSpec((nb, CHUNK, M_WIDTH), bwd),
        ],
        out_shape=(jax.ShapeDtypeStruct((bsz, s_len, M_WIDTH), jnp.bfloat16),
                   jax.ShapeDtypeStruct((bsz, s_len, M_WIDTH), jnp.bfloat16)),
        scratch_shapes=[
            pltpu.VMEM((nb, nstate, STATE_ROWS, M_HEAD_DIM), jnp.float32),
            pltpu.VMEM((nb, nstate, LANES), jnp.float32),
        ],
        compiler_params=pltpu.CompilerParams(
            dimension_semantics=("arbitrary", "arbitrary"), vmem_limit_bytes=VMEM_LIMIT),
        name="mlstm_bidirectional_scan",
    )(mqk, mv, gates, mqk, mv, gates)


def _out_kernel(x_ref, att_ref, hf_ref, hb_ref, moz_ref, hg_ref, w_ref, mod_ref, *rest, final):
    if final:
        fg_ref, o_ref = rest
    else:
        (o_ref,) = rest
    ne, tm, _ = x_ref.shape
    mxs = []
    for e in range(ne):
        mo = moz_ref[e, :, :M_WIDTH].astype(jnp.float32)
        mz = moz_ref[e, :, M_WIDTH:].astype(jnp.float32)
        hs = _sigmoid(mo) * (hf_ref[e].astype(jnp.float32) + hb_ref[e].astype(jnp.float32))
        parts = []
        for hd in range(M_HEADS):
            hh = hs[:, hd * M_HEAD_DIM:(hd + 1) * M_HEAD_DIM]
            ms = jnp.mean(hh * hh, axis=-1, keepdims=True)
            parts.append(hh * lax.rsqrt(ms + EPS))
        mxs.append((jnp.concatenate(parts, axis=1) * hg_ref[...] * _silu(mz)).astype(jnp.bfloat16))
    att = jnp.concatenate([att_ref[e] for e in range(ne)], axis=0)
    out = (jnp.dot(att, w_ref[:ATT_WIDTH, :], preferred_element_type=jnp.float32)
           + jnp.dot(jnp.concatenate(mxs, axis=0), w_ref[ATT_WIDTH:, :], preferred_element_type=jnp.float32))
    for e in range(ne):
        gate = mod_ref[e if mod_ref.shape[0] > 1 else 0, 2:3, :]
        xnew = x_ref[e] + gate * out[e * tm:(e + 1) * tm]
        if final:
            ms = jnp.mean(xnew * xnew, axis=-1, keepdims=True)
            o_ref[e] = xnew * lax.rsqrt(ms + EPS) * fg_ref[...]
        else:
            o_ref[e] = xnew


def _output(layer, hseg, att, hf, hb, moz, head_g, w_out, mod, final_g, *, row0, is_ctx, final):
    bsz, seg_len, d = hseg.shape
    tm = ROW_TILE
    ne = math.gcd(bsz, PROJ_BATCH)
    t0 = row0 // tm

    def seg_map(b, j):
        return (b, j, 0)

    def row_map(b, j):
        return (b, j + t0, 0)

    if is_ctx:
        mod_spec = pl.BlockSpec((None, 1, 3, d), lambda b, j: (layer, mod.shape[1] - 1, 0, 0))
    else:
        mod_spec = pl.BlockSpec((None, ne, 3, d), lambda b, j: (layer, b, 0, 0))
    in_specs = [
        pl.BlockSpec((ne, tm, d), seg_map),
        pl.BlockSpec((ne, tm, ATT_WIDTH), seg_map),
        pl.BlockSpec((ne, tm, M_WIDTH), row_map),
        pl.BlockSpec((ne, tm, M_WIDTH), row_map),
        pl.BlockSpec((ne, tm, 2 * M_WIDTH), row_map),
        pl.BlockSpec((None, 1, M_WIDTH), lambda b, j: (layer, 0, 0)),
        pl.BlockSpec((None, d, d), lambda b, j: (layer, 0, 0), pipeline_mode=pl.Buffered(1)),
        mod_spec,
    ]
    args = [hseg, att, hf, hb, moz, head_g, w_out, mod]
    if final:
        in_specs.append(pl.BlockSpec((1, d), lambda b, j: (0, 0)))
        args.append(final_g)
    return pl.pallas_call(
        functools.partial(_out_kernel, final=final),
        grid=(bsz // ne, seg_len // tm),
        in_specs=in_specs,
        out_specs=pl.BlockSpec((ne, tm, d), seg_map),
        out_shape=jax.ShapeDtypeStruct(hseg.shape, jnp.float32),
        compiler_params=pltpu.CompilerParams(
            dimension_semantics=("arbitrary", "arbitrary"), vmem_limit_bytes=VMEM_LIMIT),
        name="gate_norm_out_projection",
    )(*args)


def _ctx_fused_kernel(sink_ref, q_ref, kctx_ref, vctx_ref, az_ref, x_ref, hf_ref, hb_ref, moz_ref, hg_ref, w_ref,
                      mod_ref, o_ref, att_ref, *, layer, ctx_blocks, total_blocks):
    _attn_kernel(sink_ref, q_ref, kctx_ref, vctx_ref, az_ref, att_ref, layer=layer, with_window=False, blk0=0,
                 ctx_blocks=ctx_blocks, total_blocks=total_blocks)
    _out_kernel(x_ref, att_ref, hf_ref, hb_ref, moz_ref, hg_ref, w_ref, mod_ref, o_ref, final=False)


def _ctx_attention_output(layer, sink, q, kx, v, az, hc, hf, hb, moz, head_g, w_out, mod):
    bsz, ctx_len, d = hc.shape
    ctx_blocks = ctx_len // BLOCK
    total_blocks = q.shape[1] // BLOCK
    nb = math.gcd(bsz, BATCH_PER_STEP)

    def blk(b, j):
        return (b, j, 0)

    def whole(b, j):
        return (b, 0, 0)

    kernel = functools.partial(_ctx_fused_kernel, layer=layer, ctx_blocks=ctx_blocks, total_blocks=total_blocks)
    return pl.pallas_call(
        kernel,
        grid=(bsz // nb, ctx_blocks),
        in_specs=[
            pl.BlockSpec(memory_space=pltpu.SMEM),
            pl.BlockSpec((nb, BLOCK, ATT_WIDTH), blk),
            pl.BlockSpec((nb, ctx_len, ATT_WIDTH), whole),
            pl.BlockSpec((nb, ctx_len, KV_WIDTH), whole),
            pl.BlockSpec((nb, BLOCK, ATT_WIDTH), blk),
            pl.BlockSpec((nb, BLOCK, d), blk),
            pl.BlockSpec((nb, BLOCK, M_WIDTH), blk),
            pl.BlockSpec((nb, BLOCK, M_WIDTH), blk),
            pl.BlockSpec((nb, BLOCK, 2 * M_WIDTH), blk),
            pl.BlockSpec((None, 1, M_WIDTH), lambda b, j: (layer, 0, 0)),
            pl.BlockSpec((None, d, d), lambda b, j: (layer, 0, 0), pipeline_mode=pl.Buffered(1)),
            pl.BlockSpec((None, 1, 3, d), lambda b, j: (layer, mod.shape[1] - 1, 0, 0)),
        ],
        out_specs=pl.BlockSpec((nb, BLOCK, d), blk),
        out_shape=jax.ShapeDtypeStruct(hc.shape, jnp.float32),
        scratch_shapes=[pltpu.VMEM((nb, BLOCK, ATT_WIDTH), jnp.bfloat16)],
        compiler_params=pltpu.CompilerParams(
            dimension_semantics=("arbitrary", "arbitrary"), vmem_limit_bytes=VMEM_LIMIT),
        name="ctx_attention_out_projection",
    )(sink, q, kx, v, az, hc, hf, hb, moz, head_g, w_out, mod)


def _lat_fused_kernel(sink_ref, q_ref, kp_ref, kc_ref, kn_ref, kctx_ref, vp_ref, vc_ref, vn_ref, vctx_ref, az_ref,
                      x_ref, hf_ref, hb_ref, moz_ref, hg_ref, w_ref, mod_ref, *rest,
                      layer, final, ctx_blocks, total_blocks):
    att_ref = rest[-1]
    _attn_kernel(sink_ref, q_ref, kp_ref, kc_ref, kn_ref, kctx_ref, vp_ref, vc_ref, vn_ref, vctx_ref, az_ref, att_ref,
                 layer=layer, with_window=True, blk0=ctx_blocks, ctx_blocks=ctx_blocks, total_blocks=total_blocks)
    _out_kernel(x_ref, att_ref, hf_ref, hb_ref, moz_ref, hg_ref, w_ref, mod_ref, *rest[:-1], final=final)


def _lat_attention_output(layer, sink, q, kx, v, az, hx, hf, hb, moz, head_g, w_out, mod, final_g, *, ctx_len, final):
    bsz, t_len, d = hx.shape
    ctx_blocks = ctx_len // BLOCK
    total_blocks = q.shape[1] // BLOCK
    lo, hi = ctx_blocks, total_blocks - 1
    nb = math.gcd(bsz, BATCH_PER_STEP)

    def at(off):
        return lambda b, j: (b, jnp.clip(j + ctx_blocks + off, lo, hi), 0)

    def cur(b, j):
        return (b, j + ctx_blocks, 0)

    def whole(b, j):
        return (b, 0, 0)

    def seg(b, j):
        return (b, j, 0)

    in_specs = [
        pl.BlockSpec(memory_space=pltpu.SMEM),
        pl.BlockSpec((nb, BLOCK, ATT_WIDTH), cur),
        *[pl.BlockSpec((nb, BLOCK, ATT_WIDTH), at(off)) for off in (-1, 0, 1)],
        pl.BlockSpec((nb, ctx_len, ATT_WIDTH), whole),
        *[pl.BlockSpec((nb, BLOCK, KV_WIDTH), at(off)) for off in (-1, 0, 1)],
        pl.BlockSpec((nb, ctx_len, KV_WIDTH), whole),
        pl.BlockSpec((nb, BLOCK, ATT_WIDTH), cur),
        pl.BlockSpec((nb, BLOCK, d), seg),
        pl.BlockSpec((nb, BLOCK, M_WIDTH), cur),
        pl.BlockSpec((nb, BLOCK, M_WIDTH), cur),
        pl.BlockSpec((nb, BLOCK, 2 * M_WIDTH), cur),
        pl.BlockSpec((None, 1, M_WIDTH), lambda b, j: (layer, 0, 0)),
        pl.BlockSpec((None, d, d), lambda b, j: (layer, 0, 0), pipeline_mode=pl.Buffered(1)),
        pl.BlockSpec((None, nb, 3, d), lambda b, j: (layer, b, 0, 0)),
    ]
    args = [sink, q, kx, kx, kx, kx, v, v, v, v, az, hx, hf, hb, moz, head_g, w_out, mod]
    if final:
        in_specs.append(pl.BlockSpec((1, d), lambda b, j: (0, 0)))
        args.append(final_g)
    kernel = functools.partial(_lat_fused_kernel, layer=layer, final=final, ctx_blocks=ctx_blocks,
                               total_blocks=total_blocks)
    return pl.pallas_call(
        kernel,
        grid=(bsz // nb, total_blocks - ctx_blocks),
        in_specs=in_specs,
        out_specs=pl.BlockSpec((nb, BLOCK, d), seg),
        out_shape=jax.ShapeDtypeStruct(hx.shape, jnp.float32),
        scratch_shapes=[pltpu.VMEM((nb, BLOCK, ATT_WIDTH), jnp.bfloat16)],
        compiler_params=pltpu.CompilerParams(
            dimension_semantics=("arbitrary", "arbitrary"), vmem_limit_bytes=VMEM_LIMIT),
        name="window_attention_out_projection",
    )(*args)


def kernel(x, c, ctx, c_ctx, w_ada, b_ada, norm_g, w_in, conv_w, conv_b, gate_b, sink, head_g, w_out, final_g):
    bsz, t_len, d = x.shape
    ctx_len = ctx.shape[1]
    depth = w_in.shape[0]
    assert d == D_MODEL and t_len % ROW_TILE == 0 and ctx_len % ROW_TILE == 0 and t_len % GRID_W == 0

    hc, hx = ctx, x
    mod_rows = -(-(bsz + 1) // SUBLANES) * SUBLANES
    cc = jnp.zeros((mod_rows, d), jnp.float32).at[:bsz].set(c).at[mod_rows - 1].set(c_ctx)
    mod = _modulation(cc, w_ada, b_ada).reshape(depth, mod_rows, 3, d)

    w_qk, w_main, w_vg = _split_w_in(w_in)
    w_o = w_out.astype(jnp.bfloat16)
    cos, sin = _rope_tables(t_len, ctx_len)
    kscale = jnp.concatenate([jnp.ones((1, M_WIDTH), jnp.float32),
                              jnp.full((1, M_WIDTH), M_HEAD_DIM ** -0.5, jnp.float32)], axis=1)
    gate_bias = jnp.pad(gate_b, ((0, 0), (0, LANES - N_GATES)))[:, None, :]
    norm_g3, conv_b3, head_g3 = norm_g[:, None, :], conv_b[:, None, :], head_g[:, None, :]

    for l in range(depth):
        last = l == depth - 1
        q, kx, v, az, mqk, mv, moz, gates = _projection(
            l, hc, hx, mod, norm_g3, w_qk, w_main, w_vg, cos, sin, conv_w, conv_b3, kscale, gate_bias)
        hf, hb = _mlstm(mqk, mv, gates, ctx_len=ctx_len)
        new_hx = _lat_attention_output(l, sink, q, kx, v, az, hx, hf, hb, moz, head_g3, w_o, mod, final_g[None],
                                       ctx_len=ctx_len, final=last)
        if last:
            return new_hx
        hc = _ctx_attention_output(l, sink, q, kx, v, az, hc, hf, hb, moz, head_g3, w_o, mod)
        hx = new_hx
```
